```python
import math
import jax
import jax.numpy as jnp
from jax import lax
import numpy as np

D_MODEL = 2048
BATCH = 16
SEQ = 256
DEPTH = 2
DEC_BATCH = 2
DEC_SEQ = 4096
PAST_LEN = 512

GRID_W = 64
W_A = 1024
NA_HEADS = 8
NA_HEAD_DIM = 128
W_B = NA_HEADS * NA_HEAD_DIM
W_C = 1024
W_D = 1024
NA_WIN_R = 8
NA_WIN_C = 16
SHORT_K = 3
CONF_K = 31
D_FF = 5632
HY_EMB = 33
HY_HIDDEN = 64
HY_FAST_PCT = 0.3
HY_SLOW_PCT = 1.5
HY_TARGET = 1e-2
Q_BLOCK = 128
EPS = 1e-6
NEG_INF = -1e30
N_EVEN = (DEPTH + 1) // 2
N_ODD = DEPTH // 2

kernel_name = "hybrid_diffusion_prefix_step"


def _rmsnorm(x, g):
    xf = x.astype(jnp.float32)
    y = xf * lax.rsqrt(jnp.mean(xf * xf, axis=-1, keepdims=True) + EPS)
    return (y * g.astype(jnp.float32)).astype(x.dtype)


def _layernorm(x, g, b):
    xf = x.astype(jnp.float32)
    mu = jnp.mean(xf, axis=-1, keepdims=True)
    var = jnp.mean(jnp.square(xf - mu), axis=-1, keepdims=True)
    y = (xf - mu) * lax.rsqrt(var + EPS)
    return (y * g.astype(jnp.float32) + b.astype(jnp.float32)).astype(x.dtype)


def _dwconv(x, w, b=None):
    k, ch = w.shape
    y = lax.conv_general_dilated(x, w[:, None, :].astype(x.dtype), (1,), [(k // 2, k // 2)],
                                 dimension_numbers=("NWC", "WIO", "NWC"), feature_group_count=ch)
    if b is not None:
        y = y + b.astype(x.dtype)
    return y


def _adaln(cvec, w, b):
    m = jax.nn.silu(cvec) @ w + b
    return m.reshape(cvec.shape[0], 6, D_MODEL)


def _modulate(h, shift, scale):
    return h * (1 + scale[:, None, :]) + shift[:, None, :]


def _ctx_attention(q, k, v):
    bsz, L, H, Dh = q.shape
    scale = Dh ** -0.5
    qb = q.reshape(bsz, L // Q_BLOCK, Q_BLOCK, H, Dh).transpose(1, 0, 2, 3, 4)

    def blk(qi):
        s = jnp.einsum("bqhd,bkhd->bhqk", qi, k, preferred_element_type=jnp.float32) * scale
        p = jax.nn.softmax(s, axis=-1).astype(v.dtype)
        return jnp.einsum("bhqk,bkhd->bqhd", p, v)

    o = lax.map(blk, qb)
    return o.transpose(1, 0, 2, 3, 4).reshape(bsz, L, H * Dh)


def _neighbourhood_attention(q, k, v, ctx_k, ctx_v, rpb):
    bsz, L, H, Dh = q.shape
    rows = L // GRID_W
    wr = min(NA_WIN_R, rows)
    qc = NA_WIN_C
    n_cb = GRID_W // qc
    slab = 2 * qc
    n_loc = wr * slab
    scale = Dh ** -0.5
    kg = k.reshape(bsz, rows, GRID_W, H, Dh)
    vg = v.reshape(bsz, rows, GRID_W, H, Dh)
    q_rows = q.reshape(bsz, rows, n_cb, qc, H, Dh).transpose(1, 0, 2, 3, 4, 5)
    r_idx = jnp.arange(rows)
    row_start = jnp.clip(r_idx - wr // 2, 0, rows - wr)
    cols = jnp.arange(GRID_W).reshape(n_cb, qc)
    col_start = jnp.clip(cols - qc // 2, 0, GRID_W - qc)
    slab_cols = (jnp.clip(jnp.arange(n_cb) * qc - qc // 2, 0, GRID_W - slab)[:, None]
                 + jnp.arange(slab))
    rel = slab_cols[:, None, :] - col_start[..., None]
    col_ok = (rel >= 0) & (rel < qc)
    dc_idx = jnp.clip(slab_cols[:, None, :] - cols[..., None] + NA_WIN_C - 1, 0, 2 * NA_WIN_C - 2)

    def one_row(args):
        q_r, r0, r = args
        k_band = lax.dynamic_slice_in_dim(kg, r0, wr, axis=1)[:, :, slab_cols]
        v_band = lax.dynamic_slice_in_dim(vg, r0, wr, axis=1)[:, :, slab_cols]
        s_loc = jnp.einsum("bnqhd,brnshd->bhnqrs", q_r, k_band, preferred_element_type=jnp.float32) * scale
        dr_idx = r0 + jnp.arange(wr) - r + NA_WIN_R - 1
        bias = rpb[:, dr_idx[None, None, :, None], dc_idx[:, :, None, :]].astype(jnp.float32)
        s_loc = jnp.where(col_ok[:, :, None, :], s_loc + bias, NEG_INF)
        s_ctx = jnp.einsum("bnqhd,bkhd->bhnqk", q_r, ctx_k, preferred_element_type=jnp.float32) * scale
        s = jnp.concatenate([s_loc.reshape(bsz, H, n_cb, qc, n_loc), s_ctx], axis=-1)
        p = jax.nn.softmax(s, axis=-1).astype(v.dtype)
        p_loc = p[..., :n_loc].reshape(bsz, H, n_cb, qc, wr, slab)
        return (jnp.einsum("bhnqrs,brnshd->bnqhd", p_loc, v_band)
                + jnp.einsum("bhnqk,bkhd->bnqhd", p[..., n_loc:], ctx_v))

    o = lax.map(one_row, (q_rows, row_start, r_idx))
    return o.transpose(1, 0, 2, 3, 4, 5).reshape(bsz, L, H * Dh)


def _even_mixer(h, w_in, conv_a, q_norm, k_norm, rpb, w_out, ctx_kv):
    bsz, L, _ = h.shape
    a_b, a_c, a_x, q, k, v = jnp.split(
        h @ w_in, [W_A, 2 * W_A, 3 * W_A, 3 * W_A + W_B, 3 * W_A + 2 * W_B], axis=-1)
    y_a = a_b * _dwconv(a_c * a_x, conv_a)
    q = _rmsnorm(q.reshape(bsz, L, NA_HEADS, NA_HEAD_DIM), q_norm)
    k = _rmsnorm(k.reshape(bsz, L, NA_HEADS, NA_HEAD_DIM), k_norm)
    v = v.reshape(bsz, L, NA_HEADS, NA_HEAD_DIM)
    if ctx_kv is None:
        y_b = _ctx_attention(q, k, v)
        kv = (k, v)
    else:
        y_b = _neighbourhood_attention(q, k, v, ctx_kv[0], ctx_kv[1], rpb)
        kv = None
    return jnp.concatenate([y_a, y_b], axis=-1) @ w_out, kv


def _hyena_filter(L, w1, b1, f1, w2, b2, f2, w3):
    f32 = jnp.float32
    t = jnp.linspace(0.0, 1.0, L, dtype=f32)[:, None]
    bands = (HY_EMB - 1) // 2
    ang = 2 * math.pi * jnp.arange(L, dtype=f32)[:, None] / L
    freqs = jnp.linspace(1e-4, bands - 1, bands, dtype=f32)[None, :]
    z = jnp.concatenate([t, jnp.cos(freqs * ang), -jnp.sin(freqs * ang)], axis=-1)
    hid = jnp.sin(f1.astype(f32) * (z @ w1.astype(f32) + b1.astype(f32)))
    hid = jnp.sin(f2.astype(f32) * (hid @ w2.astype(f32) + b2.astype(f32)))
    hf = (hid @ w3.astype(f32)).reshape(L, 2, W_D)
    max_decay = math.log(HY_TARGET) / HY_FAST_PCT
    min_decay = math.log(HY_TARGET) / HY_SLOW_PCT
    deltas = jnp.linspace(min_decay, max_decay, W_D, dtype=f32)
    hf = hf * jnp.exp(-t * jnp.abs(deltas))[:, None, :]
    fwd, bwd = hf[:, 0], hf[:, 1]
    return jnp.concatenate([fwd, jnp.zeros((1, W_D), f32), bwd[:0:-1]], axis=0)


def _bidir_fftconv(u, filt, bias):
    L = u.shape[1]
    uf = u.astype(jnp.float32)
    spec = jnp.fft.rfft(uf, n=2 * L, axis=1) * jnp.fft.rfft(filt, n=2 * L, axis=0)[None]
    y = jnp.fft.irfft(spec, n=2 * L, axis=1)[:, :L]
    return (y + uf * bias.astype(jnp.float32)).astype(u.dtype)


def _odd_mixer(h, w_in, conf_dw, conf_dw_b, conf_ln_g, conf_ln_b, hy_short, hy_short_b,
               hy_w1, hy_b1, hy_f1, hy_w2, hy_b2, hy_f2, hy_w3, hy_bias, w_out):
    L = h.shape[1]
    c_a, c_g, hy = jnp.split(h @ w_in, [W_C, 2 * W_C], axis=-1)
    u = c_a * jax.nn.sigmoid(c_g)
    u = jax.nn.silu(_layernorm(_dwconv(u, conf_dw, conf_dw_b), conf_ln_g, conf_ln_b))
    x0, x1, vv = jnp.split(_dwconv(hy, hy_short, hy_short_b), [W_D, 2 * W_D], axis=-1)
    filt = _hyena_filter(L, hy_w1, hy_b1, hy_f1, hy_w2, hy_b2, hy_f2, hy_w3)
    z = x0 * _bidir_fftconv(vv * x1, filt, hy_bias)
    return jnp.concatenate([u, z], axis=-1) @ w_out


def _conv_ffn(h, w_in, conv, w_out):
    a, g = jnp.split(h @ w_in, 2, axis=-1)
    return (jax.nn.gelu(_dwconv(a, conv)) * g) @ w_out


def setup_inputs(seed: int = 0) -> dict:
    key = jax.random.key(seed)
    keys = iter(jax.random.split(key, 64))

    def nrm(shape, std):
        return jax.random.normal(next(keys), shape, jnp.float32) * std

    def gain(shape):
        return 1.0 + nrm(shape, 0.01)

    hd = NA_HEAD_DIM
    return {
        "x_prompt": nrm((BATCH, SEQ, D_MODEL), 1.0),
        "x_sample": nrm((DEC_BATCH, DEC_SEQ, D_MODEL), 1.0),
        "cache_k": nrm((DEC_BATCH, N_EVEN, PAST_LEN, NA_HEADS, hd), 1.0),
        "cache_v": nrm((DEC_BATCH, N_EVEN, PAST_LEN, NA_HEADS, hd), 1.0),
        "c": nrm((DEC_BATCH, D_MODEL), 1.0),
        "c_ctx": nrm((D_MODEL,), 1.0),
        "ada_w": nrm((DEPTH, D_MODEL, 6 * D_MODEL), 0.5 * D_MODEL ** -0.5),
        "ada_b": nrm((DEPTH, 6 * D_MODEL), 0.02),
        "norm_mix": gain((DEPTH, D_MODEL)),
        "norm_ffn": gain((DEPTH, D_MODEL)),
        "e_w_in": nrm((N_EVEN, D_MODEL, 3 * W_A + 3 * W_B), D_MODEL ** -0.5),
        "e_conv_a": nrm((N_EVEN, SHORT_K, W_A), SHORT_K ** -0.5),
        "e_q_norm": gain((N_EVEN, hd)),
        "e_k_norm": gain((N_EVEN, hd)),
        "e_rpb": nrm((N_EVEN, NA_HEADS, 2 * NA_WIN_R - 1, 2 * NA_WIN_C - 1), 0.02),
        "e_w_out": nrm((N_EVEN, W_A + W_B, D_MODEL), (W_A + W_B) ** -0.5),
        "o_w_in": nrm((N_ODD, D_MODEL, 2 * W_C + 3 * W_D), D_MODEL ** -0.5),
        "o_conf_dw": nrm((N_ODD, CONF_K, W_C), CONF_K ** -0.5),
        "o_conf_dw_b": nrm((N_ODD, W_C), 0.02),
        "o_conf_ln_g": gain((N_ODD, W_C)),
        "o_conf_ln_b": nrm((N_ODD, W_C), 0.02),
        "o_hy_short": nrm((N_ODD, SHORT_K, 3 * W_D), SHORT_K ** -0.5),
        "o_hy_short_b": nrm((N_ODD, 3 * W_D), 0.02),
        "o_hy_w1": nrm((N_ODD, HY_EMB, HY_HIDDEN), HY_EMB ** -0.5),
        "o_hy_b1": nrm((N_ODD, HY_HIDDEN), 0.02),
        "o_hy_f1": gain((N_ODD, HY_HIDDEN)),
        "o_hy_w2": nrm((N_ODD, HY_HIDDEN, HY_HIDDEN), HY_HIDDEN ** -0.5),
        "o_hy_b2": nrm((N_ODD, HY_HIDDEN), 0.02),
        "o_hy_f2": gain((N_ODD, HY_HIDDEN)),
        "o_hy_w3": nrm((N_ODD, HY_HIDDEN, 2 * W_D), 0.1 * HY_HIDDEN ** -0.5),
        "o_hy_bias": nrm((N_ODD, W_D), 1.0),
        "o_w_out": nrm((N_ODD, W_C + W_D, D_MODEL), (W_C + W_D) ** -0.5),
        "ffn_in": nrm((DEPTH, D_MODEL, 2 * D_FF), D_MODEL ** -0.5),
        "ffn_conv": nrm((DEPTH, SHORT_K, D_FF), SHORT_K ** -0.5),
        "ffn_out": nrm((DEPTH, D_FF, D_MODEL), D_FF ** -0.5),
    }


def reference(x_prompt, x_sample, cache_k, cache_v, c, c_ctx, ada_w, ada_b, norm_mix, norm_ffn,
              e_w_in, e_conv_a, e_q_norm, e_k_norm, e_rpb, e_w_out,
              o_w_in, o_conf_dw, o_conf_dw_b, o_conf_ln_g, o_conf_ln_b, o_hy_short, o_hy_short_b,
              o_hy_w1, o_hy_b1, o_hy_f1, o_hy_w2, o_hy_b2, o_hy_f2, o_hy_w3, o_hy_bias, o_w_out,
              ffn_in, ffn_conv, ffn_out):
    xp, xs = x_prompt, x_sample
    ks_new, vs_new = [], []
    for layer in range(DEPTH):
        j = layer // 2
        mod_p = _adaln(c_ctx[None, :], ada_w[layer], ada_b[layer])
        mod_s = _adaln(c, ada_w[layer], ada_b[layer])
        hp = _modulate(_rmsnorm(xp, norm_mix[layer]), mod_p[:, 0], mod_p[:, 1])
        hs = _modulate(_rmsnorm(xs, norm_mix[layer]), mod_s[:, 0], mod_s[:, 1])
        if layer % 2 == 0:
            ev = (e_w_in[j], e_conv_a[j], e_q_norm[j], e_k_norm[j], e_rpb[j], e_w_out[j])
            mp, (kp, vp) = _even_mixer(hp, *ev, None)
            ms, _ = _even_mixer(hs, *ev, (cache_k[:, j], cache_v[:, j]))
            ks_new.append(kp)
            vs_new.append(vp)
        else:
            od = (o_w_in[j], o_conf_dw[j], o_conf_dw_b[j], o_conf_ln_g[j], o_conf_ln_b[j],
                  o_hy_short[j], o_hy_short_b[j], o_hy_w1[j], o_hy_b1[j], o_hy_f1[j],
                  o_hy_w2[j], o_hy_b2[j], o_hy_f2[j], o_hy_w3[j], o_hy_bias[j], o_w_out[j])
            mp = _odd_mixer(hp, *od)
            ms = _odd_mixer(hs, *od)
        xp = xp + mod_p[:, 2][:, None, :] * mp
        xs = xs + mod_s[:, 2][:, None, :] * ms
        hp = _modulate(_rmsnorm(xp, norm_ffn[layer]), mod_p[:, 3], mod_p[:, 4])
        hs = _modulate(_rmsnorm(xs, norm_ffn[layer]), mod_s[:, 3], mod_s[:, 4])
        xp = xp + mod_p[:, 5][:, None, :] * _conv_ffn(hp, ffn_in[layer], ffn_conv[layer], ffn_out[layer])
        xs = xs + mod_s[:, 5][:, None, :] * _conv_ffn(hs, ffn_in[layer], ffn_conv[layer], ffn_out[layer])
    new_cache_k = jnp.stack(ks_new, axis=1)
    new_cache_v = jnp.stack(vs_new, axis=1)
    return (xp, xs, new_cache_k, new_cache_v)
```

```python
import functools
import math

import jax
import jax.numpy as jnp
from jax import lax
from jax.experimental import pallas as pl
from jax.experimental.pallas import tpu as pltpu

D_MODEL = 2048
BATCH = 16
SEQ = 256
DEPTH = 2
DEC_BATCH = 2
DEC_SEQ = 4096
PAST_LEN = 512
GRID_W = 64
W_A = 1024
NA_HEADS = 8
NA_HEAD_DIM = 128
W_B = NA_HEADS * NA_HEAD_DIM
W_C = 1024
W_D = 1024
NA_WIN_R = 8
NA_WIN_C = 16
SHORT_K = 3
CONF_K = 31
D_FF = 5632
HY_EMB = 33
HY_HIDDEN = 64
HY_FAST_PCT = 0.3
HY_SLOW_PCT = 1.5
HY_TARGET = 1e-2
Q_BLOCK = 128
EPS = 1e-6
NEG_INF = -1e30

VMEM_LIMIT_BYTES = 56 * 1024 * 1024


def _mm_kernel(x_ref, w_ref, o_ref):
    x = x_ref[...].astype(jnp.bfloat16)
    w = w_ref[...].astype(jnp.bfloat16)
    o_ref[...] = jnp.dot(x, w, preferred_element_type=jnp.float32)


def _mm(x, w, tm=512, tn=512):
    m, k = x.shape
    _, n = w.shape
    tm = min(tm, m)
    tn = min(tn, n)
    return pl.pallas_call(
        _mm_kernel,
        grid=(m // tm, n // tn),
        in_specs=[pl.BlockSpec((tm, k), lambda i, j: (i, 0)),
                  pl.BlockSpec((k, tn), lambda i, j: (0, j))],
        out_specs=pl.BlockSpec((tm, tn), lambda i, j: (i, j)),
        out_shape=jax.ShapeDtypeStruct((m, n), jnp.float32),
        compiler_params=pltpu.CompilerParams(
            dimension_semantics=("parallel", "parallel"),
            vmem_limit_bytes=VMEM_LIMIT_BYTES),
    )(x, w)


def _mm3(x, w, **kw):
    b, l, k = x.shape
    return _mm(x.reshape(b * l, k), w, **kw).reshape(b, l, w.shape[1])


def _rmsnorm(x, g):
    xf = x.astype(jnp.float32)
    y = xf * lax.rsqrt(jnp.mean(xf * xf, axis=-1, keepdims=True) + EPS)
    return (y * g.astype(jnp.float32)).astype(x.dtype)


def _layernorm(x, g, b):
    xf = x.astype(jnp.float32)
    mu = jnp.mean(xf, axis=-1, keepdims=True)
    var = jnp.mean(jnp.square(xf - mu), axis=-1, keepdims=True)
    y = (xf - mu) * lax.rsqrt(var + EPS)
    return (y * g.astype(jnp.float32) + b.astype(jnp.float32)).astype(x.dtype)


def _dwconv(x, w, b=None):
    k, ch = w.shape
    y = lax.conv_general_dilated(x, w[:, None, :].astype(x.dtype), (1,), [(k // 2, k // 2)],
                                 dimension_numbers=("NWC", "WIO", "NWC"), feature_group_count=ch)
    if b is not None:
        y = y + b.astype(x.dtype)
    return y


def _adaln(cvec, w, b):
    s = jax.nn.silu(cvec)
    pad = jnp.zeros((8 - s.shape[0], s.shape[1]), s.dtype)
    m = _mm(jnp.concatenate([s, pad], axis=0), w, tm=8, tn=1024)[:s.shape[0]] + b
    return m.reshape(cvec.shape[0], 6, D_MODEL)


def _modulate(h, shift, scale):
    return h * (1 + scale[:, None, :]) + shift[:, None, :]


def _ctx_attention(q, k, v):
    bsz, L, H, Dh = q.shape
    scale = Dh ** -0.5
    qb = q.reshape(bsz, L // Q_BLOCK, Q_BLOCK, H, Dh).transpose(1, 0, 2, 3, 4)

    def blk(qi):
        s = jnp.einsum("bqhd,bkhd->bhqk", qi, k, preferred_element_type=jnp.float32) * scale
        p = jax.nn.softmax(s, axis=-1).astype(v.dtype)
        return jnp.einsum("bhqk,bkhd->bqhd", p, v)

    o = lax.map(blk, qb)
    return o.transpose(1, 0, 2, 3, 4).reshape(bsz, L, H * Dh)


def _neighbourhood_attention(q, k, v, ctx_k, ctx_v, rpb):
    bsz, L, H, Dh = q.shape
    rows = L // GRID_W
    wr = min(NA_WIN_R, rows)
    qc = NA_WIN_C
    n_cb = GRID_W // qc
    slab = 2 * qc
    n_loc = wr * slab
    scale = Dh ** -0.5
    kg = k.reshape(bsz, rows, GRID_W, H, Dh)
    vg = v.reshape(bsz, rows, GRID_W, H, Dh)
    q_rows = q.reshape(bsz, rows, n_cb, qc, H, Dh).transpose(1, 0, 2, 3, 4, 5)
    r_idx = jnp.arange(rows)
    row_start = jnp.clip(r_idx - wr // 2, 0, rows - wr)
    cols = jnp.arange(GRID_W).reshape(n_cb, qc)
    col_start = jnp.clip(cols - qc // 2, 0, GRID_W - qc)
    slab_cols = (jnp.clip(jnp.arange(n_cb) * qc - qc // 2, 0, GRID_W - slab)[:, None]
                 + jnp.arange(slab))
    rel = slab_cols[:, None, :] - col_start[..., None]
    col_ok = (rel >= 0) & (rel < qc)
    dc_idx = jnp.clip(slab_cols[:, None, :] - cols[..., None] + NA_WIN_C - 1, 0, 2 * NA_WIN_C - 2)

    def one_row(args):
        q_r, r0, r = args
        k_band = lax.dynamic_slice_in_dim(kg, r0, wr, axis=1)[:, :, slab_cols]
        v_band = lax.dynamic_slice_in_dim(vg, r0, wr, axis=1)[:, :, slab_cols]
        s_loc = jnp.einsum("bnqhd,brnshd->bhnqrs", q_r, k_band, preferred_element_type=jnp.float32) * scale
        dr_idx = r0 + jnp.arange(wr) - r + NA_WIN_R - 1
        bias = rpb[:, dr_idx[None, None, :, None], dc_idx[:, :, None, :]].astype(jnp.float32)
        s_loc = jnp.where(col_ok[:, :, None, :], s_loc + bias, NEG_INF)
        s_ctx = jnp.einsum("bnqhd,bkhd->bhnqk", q_r, ctx_k, preferred_element_type=jnp.float32) * scale
        s = jnp.concatenate([s_loc.reshape(bsz, H, n_cb, qc, n_loc), s_ctx], axis=-1)
        p = jax.nn.softmax(s, axis=-1).astype(v.dtype)
        p_loc = p[..., :n_loc].reshape(bsz, H, n_cb, qc, wr, slab)
        return (jnp.einsum("bhnqrs,brnshd->bnqhd", p_loc, v_band)
                + jnp.einsum("bhnqk,bkhd->bnqhd", p[..., n_loc:], ctx_v))

    o = lax.map(one_row, (q_rows, row_start, r_idx))
    return o.transpose(1, 0, 2, 3, 4, 5).reshape(bsz, L, H * Dh)


def _even_mixer(h, w_in, conv_a, q_norm, k_norm, rpb, w_out, ctx_kv):
    bsz, L, _ = h.shape
    a_b, a_c, a_x, q, k, v = jnp.split(
        _mm3(h, w_in), [W_A, 2 * W_A, 3 * W_A, 3 * W_A + W_B, 3 * W_A + 2 * W_B], axis=-1)
    y_a = a_b * _dwconv(a_c * a_x, conv_a)
    q = _rmsnorm(q.reshape(bsz, L, NA_HEADS, NA_HEAD_DIM), q_norm)
    k = _rmsnorm(k.reshape(bsz, L, NA_HEADS, NA_HEAD_DIM), k_norm)
    v = v.reshape(bsz, L, NA_HEADS, NA_HEAD_DIM)
    if ctx_kv is None:
        y_b = _ctx_attention(q, k, v)
        kv = (k, v)
    else:
        y_b = _neighbourhood_attention(q, k, v, ctx_kv[0], ctx_kv[1], rpb)
        kv = None
    return _mm3(jnp.concatenate([y_a, y_b], axis=-1), w_out), kv


def _hyena_filter(L, w1, b1, f1, w2, b2, f2, w3):
    f32 = jnp.float32
    t = jnp.linspace(0.0, 1.0, L, dtype=f32)[:, None]
    bands = (HY_EMB - 1) // 2
    ang = 2 * math.pi * jnp.arange(L, dtype=f32)[:, None] / L
    freqs = jnp.linspace(1e-4, bands - 1, bands, dtype=f32)[None, :]
    z = jnp.concatenate([t, jnp.cos(freqs * ang), -jnp.sin(freqs * ang)], axis=-1)
    hid = jnp.sin(f1.astype(f32) * (z @ w1.astype(f32) + b1.astype(f32)))
    hid = jnp.sin(f2.astype(f32) * (hid @ w2.astype(f32) + b2.astype(f32)))
    hf = (hid @ w3.astype(f32)).reshape(L, 2, W_D)
    max_decay = math.log(HY_TARGET) / HY_FAST_PCT
    min_decay = math.log(HY_TARGET) / HY_SLOW_PCT
    deltas = jnp.linspace(min_decay, max_decay, W_D, dtype=f32)
    hf = hf * jnp.exp(-t * jnp.abs(deltas))[:, None, :]
    fwd, bwd = hf[:, 0], hf[:, 1]
    return jnp.concatenate([fwd, jnp.zeros((1, W_D), f32), bwd[:0:-1]], axis=0)


def _bidir_fftconv(u, filt, bias):
    L = u.shape[1]
    uf = u.astype(jnp.float32)
    spec = jnp.fft.rfft(uf, n=2 * L, axis=1) * jnp.fft.rfft(filt, n=2 * L, axis=0)[None]
    y = jnp.fft.irfft(spec, n=2 * L, axis=1)[:, :L]
    return (y + uf * bias.astype(jnp.float32)).astype(u.dtype)


def _odd_mixer(h, w_in, conf_dw, conf_dw_b, conf_ln_g, conf_ln_b, hy_short, hy_short_b,
               hy_w1, hy_b1, hy_f1, hy_w2, hy_b2, hy_f2, hy_w3, hy_bias, w_out):
    L = h.shape[1]
    c_a, c_g, hy = jnp.split(_mm3(h, w_in), [W_C, 2 * W_C], axis=-1)
    u = c_a * jax.nn.sigmoid(c_g)
    u = jax.nn.silu(_layernorm(_dwconv(u, conf_dw, conf_dw_b), conf_ln_g, conf_ln_b))
    x0, x1, vv = jnp.split(_dwconv(hy, hy_short, hy_short_b), [W_D, 2 * W_D], axis=-1)
    filt = _hyena_filter(L, hy_w1, hy_b1, hy_f1, hy_w2, hy_b2, hy_f2, hy_w3)
    z = x0 * _bidir_fftconv(vv * x1, filt, hy_bias)
    return _mm3(jnp.concatenate([u, z], axis=-1), w_out)


def _conv_ffn(h, w_in, conv, w_out):
    a, g = jnp.split(_mm3(h, w_in), 2, axis=-1)
    return _mm3(jax.nn.gelu(_dwconv(a, conv)) * g, w_out, tm=256)


def kernel(x_prompt, x_sample, cache_k, cache_v, c, c_ctx, ada_w, ada_b, norm_mix, norm_ffn,
           e_w_in, e_conv_a, e_q_norm, e_k_norm, e_rpb, e_w_out,
           o_w_in, o_conf_dw, o_conf_dw_b, o_conf_ln_g, o_conf_ln_b, o_hy_short, o_hy_short_b,
           o_hy_w1, o_hy_b1, o_hy_f1, o_hy_w2, o_hy_b2, o_hy_f2, o_hy_w3, o_hy_bias, o_w_out,
           ffn_in, ffn_conv, ffn_out):
    xp, xs = x_prompt, x_sample
    ks_new, vs_new = [], []
    for layer in range(DEPTH):
        j = layer // 2
        mod_p = _adaln(c_ctx[None, :], ada_w[layer], ada_b[layer])
        mod_s = _adaln(c, ada_w[layer], ada_b[layer])
        hp = _modulate(_rmsnorm(xp, norm_mix[layer]), mod_p[:, 0], mod_p[:, 1])
        hs = _modulate(_rmsnorm(xs, norm_mix[layer]), mod_s[:, 0], mod_s[:, 1])
        if layer % 2 == 0:
            ev = (e_w_in[j], e_conv_a[j], e_q_norm[j], e_k_norm[j], e_rpb[j], e_w_out[j])
            mp, (kp, vp) = _even_mixer(hp, *ev, None)
            ms, _ = _even_mixer(hs, *ev, (cache_k[:, j], cache_v[:, j]))
            ks_new.append(kp)
            vs_new.append(vp)
        else:
            od = (o_w_in[j], o_conf_dw[j], o_conf_dw_b[j], o_conf_ln_g[j], o_conf_ln_b[j],
                  o_hy_short[j], o_hy_short_b[j], o_hy_w1[j], o_hy_b1[j], o_hy_f1[j],
                  o_hy_w2[j], o_hy_b2[j], o_hy_f2[j], o_hy_w3[j], o_hy_bias[j], o_w_out[j])
            mp = _odd_mixer(hp, *od)
            ms = _odd_mixer(hs, *od)
        xp = xp + mod_p[:, 2][:, None, :] * mp
        xs = xs + mod_s[:, 2][:, None, :] * ms
        hp = _modulate(_rmsnorm(xp, norm_ffn[layer]), mod_p[:, 3], mod_p[:, 4])
        hs = _modulate(_rmsnorm(xs, norm_ffn[layer]), mod_s[:, 3], mod_s[:, 4])
        xp = xp + mod_p[:, 5][:, None, :] * _conv_ffn(hp, ffn_in[layer], ffn_conv[layer], ffn_out[layer])
        xs = xs + mod_s[:, 5][:, None, :] * _conv_ffn(hs, ffn_in[layer], ffn_conv[layer], ffn_out[layer])
    new_cache_k = jnp.stack(ks_new, axis=1)
    new_cache_v = jnp.stack(vs_new, axis=1)
    return (xp, xs, new_cache_k, new_cache_v)
```

```python
import functools
import math

import jax
import jax.numpy as jnp
from jax import lax
from jax.experimental import pallas as pl
from jax.experimental.pallas import tpu as pltpu

D_MODEL = 2048
BATCH = 16
SEQ = 256
DEPTH = 2
DEC_BATCH = 2
DEC_SEQ = 4096
PAST_LEN = 512
GRID_W = 64
W_A = 1024
NA_HEADS = 8
NA_HEAD_DIM = 128
W_B = NA_HEADS * NA_HEAD_DIM
W_C = 1024
W_D = 1024
NA_WIN_R = 8
NA_WIN_C = 16
SHORT_K = 3
CONF_K = 31
D_FF = 5632
HY_EMB = 33
HY_HIDDEN = 64
HY_FAST_PCT = 0.3
HY_SLOW_PCT = 1.5
HY_TARGET = 1e-2
Q_BLOCK = 128
EPS = 1e-6
NEG_INF = -1e30

VMEM_LIMIT_BYTES = 56 * 1024 * 1024


def _mm_kernel(x_ref, w_ref, o_ref):
    x = x_ref[...].astype(jnp.bfloat16)
    w = w_ref[...].astype(jnp.bfloat16)
    o_ref[...] = jnp.dot(x, w, preferred_element_type=jnp.float32)


def _mm(x, w, tm=512, tn=512):
    m, k = x.shape
    _, n = w.shape
    tm = min(tm, m)
    tn = min(tn, n)
    return pl.pallas_call(
        _mm_kernel,
        grid=(m // tm, n // tn),
        in_specs=[pl.BlockSpec((tm, k), lambda i, j: (i, 0)),
                  pl.BlockSpec((k, tn), lambda i, j: (0, j))],
        out_specs=pl.BlockSpec((tm, tn), lambda i, j: (i, j)),
        out_shape=jax.ShapeDtypeStruct((m, n), jnp.float32),
        compiler_params=pltpu.CompilerParams(
            dimension_semantics=("parallel", "parallel"),
            vmem_limit_bytes=VMEM_LIMIT_BYTES),
    )(x, w)


def _mm3(x, w, **kw):
    b, l, k = x.shape
    return _mm(x.reshape(b * l, k), w, **kw).reshape(b, l, w.shape[1])


def _rmsnorm(x, g):
    xf = x.astype(jnp.float32)
    y = xf * lax.rsqrt(jnp.mean(xf * xf, axis=-1, keepdims=True) + EPS)
    return (y * g.astype(jnp.float32)).astype(x.dtype)


def _layernorm(x, g, b):
    xf = x.astype(jnp.float32)
    mu = jnp.mean(xf, axis=-1, keepdims=True)
    var = jnp.mean(jnp.square(xf - mu), axis=-1, keepdims=True)
    y = (xf - mu) * lax.rsqrt(var + EPS)
    return (y * g.astype(jnp.float32) + b.astype(jnp.float32)).astype(x.dtype)


def _dwconv(x, w, b=None):
    k, ch = w.shape
    y = lax.conv_general_dilated(x, w[:, None, :].astype(x.dtype), (1,), [(k // 2, k // 2)],
                                 dimension_numbers=("NWC", "WIO", "NWC"), feature_group_count=ch)
    if b is not None:
        y = y + b.astype(x.dtype)
    return y


def _adaln(cvec, w, b):
    s = jax.nn.silu(cvec)
    pad = jnp.zeros((8 - s.shape[0], s.shape[1]), s.dtype)
    m = _mm(jnp.concatenate([s, pad], axis=0), w, tm=8, tn=1024)[:s.shape[0]] + b
    return m.reshape(cvec.shape[0], 6, D_MODEL)


def _modulate(h, shift, scale):
    return h * (1 + scale[:, None, :]) + shift[:, None, :]


def _ctx_attention(q, k, v):
    bsz, L, H, Dh = q.shape
    scale = Dh ** -0.5
    qb = q.reshape(bsz, L // Q_BLOCK, Q_BLOCK, H, Dh).transpose(1, 0, 2, 3, 4)

    def blk(qi):
        s = jnp.einsum("bqhd,bkhd->bhqk", qi, k, preferred_element_type=jnp.float32) * scale
        p = jax.nn.softmax(s, axis=-1).astype(v.dtype)
        return jnp.einsum("bhqk,bkhd->bqhd", p, v)

    o = lax.map(blk, qb)
    return o.transpose(1, 0, 2, 3, 4).reshape(bsz, L, H * Dh)


def _neighbourhood_attention(q, k, v, ctx_k, ctx_v, rpb):
    bsz, L, H, Dh = q.shape
    rows = L // GRID_W
    wr = min(NA_WIN_R, rows)
    qc = NA_WIN_C
    n_cb = GRID_W // qc
    slab = 2 * qc
    n_loc = wr * slab
    scale = Dh ** -0.5
    kg = k.reshape(bsz, rows, GRID_W, H, Dh)
    vg = v.reshape(bsz, rows, GRID_W, H, Dh)
    q_rows = q.reshape(bsz, rows, n_cb, qc, H, Dh).transpose(1, 0, 2, 3, 4, 5)
    r_idx = jnp.arange(rows)
    row_start = jnp.clip(r_idx - wr // 2, 0, rows - wr)
    cols = jnp.arange(GRID_W).reshape(n_cb, qc)
    col_start = jnp.clip(cols - qc // 2, 0, GRID_W - qc)
    slab_cols = (jnp.clip(jnp.arange(n_cb) * qc - qc // 2, 0, GRID_W - slab)[:, None]
                 + jnp.arange(slab))
    rel = slab_cols[:, None, :] - col_start[..., None]
    col_ok = (rel >= 0) & (rel < qc)
    dc_idx = jnp.clip(slab_cols[:, None, :] - cols[..., None] + NA_WIN_C - 1, 0, 2 * NA_WIN_C - 2)

    def one_row(args):
        q_r, r0, r = args
        k_band = lax.dynamic_slice_in_dim(kg, r0, wr, axis=1)[:, :, slab_cols]
        v_band = lax.dynamic_slice_in_dim(vg, r0, wr, axis=1)[:, :, slab_cols]
        s_loc = jnp.einsum("bnqhd,brnshd->bhnqrs", q_r, k_band, preferred_element_type=jnp.float32) * scale
        dr_idx = r0 + jnp.arange(wr) - r + NA_WIN_R - 1
        bias = rpb[:, dr_idx[None, None, :, None], dc_idx[:, :, None, :]].astype(jnp.float32)
        s_loc = jnp.where(col_ok[:, :, None, :], s_loc + bias, NEG_INF)
        s_ctx = jnp.einsum("bnqhd,bkhd->bhnqk", q_r, ctx_k, preferred_element_type=jnp.float32) * scale
        s = jnp.concatenate([s_loc.reshape(bsz, H, n_cb, qc, n_loc), s_ctx], axis=-1)
        p = jax.nn.softmax(s, axis=-1).astype(v.dtype)
        p_loc = p[..., :n_loc].reshape(bsz, H, n_cb, qc, wr, slab)
        return (jnp.einsum("bhnqrs,brnshd->bnqhd", p_loc, v_band)
                + jnp.einsum("bhnqk,bkhd->bnqhd", p[..., n_loc:], ctx_v))

    o = lax.map(one_row, (q_rows, row_start, r_idx))
    return o.transpose(1, 0, 2, 3, 4, 5).reshape(bsz, L, H * Dh)


ROWS = DEC_SEQ // GRID_W
NA_QROWS = 8
NA_KROWS = 16
NA_NQ = NA_QROWS * GRID_W
NA_NK = NA_KROWS * GRID_W
NA_STEPS = ROWS // NA_QROWS
QKV_COL0 = 3 * W_A // NA_HEAD_DIM
HEAD_SCALE = NA_HEAD_DIM ** -0.5


def _head_rmsnorm(x, g):
    return x * lax.rsqrt(jnp.mean(x * x, axis=-1, keepdims=True) + EPS) * g


def _na_key_row0(j):
    return jnp.clip(j * NA_QROWS - NA_WIN_R // 2, 0, ROWS - NA_KROWS)


def _na_bias_tables(rpb):
    tables = []
    for j in (0, NA_STEPS // 2, NA_STEPS - 1):
        r = j * NA_QROWS + jnp.arange(NA_QROWS)
        kr = _na_key_row0(j) + jnp.arange(NA_KROWS)
        r0 = jnp.clip(r - NA_WIN_R // 2, 0, ROWS - NA_WIN_R)
        row_ok = (kr[None, :] >= r0[:, None]) & (kr[None, :] < r0[:, None] + NA_WIN_R)
        dr = jnp.clip(kr[None, :] - r[:, None] + NA_WIN_R - 1, 0, 2 * NA_WIN_R - 2)
        qc = jnp.arange(GRID_W)
        kc = jnp.arange(GRID_W)
        cs = jnp.clip(qc - NA_WIN_C // 2, 0, GRID_W - NA_WIN_C)
        col_ok = (kc[None, :] >= cs[:, None]) & (kc[None, :] < cs[:, None] + NA_WIN_C)
        dc = jnp.clip(kc[None, :] - qc[:, None] + NA_WIN_C - 1, 0, 2 * NA_WIN_C - 2)
        bias = rpb[:, dr[:, None, :, None], dc[None, :, None, :]].astype(jnp.float32)
        ok = row_ok[:, None, :, None] & col_ok[None, :, None, :]
        tables.append(jnp.where(ok[None], bias, NEG_INF).reshape(NA_HEADS, NA_NQ, NA_NK))
    return jnp.stack(tables, axis=1)


def _na_kernel(q_ref, k_ref, v_ref, ck_ref, cv_ref, bias_ref, qg_ref, kg_ref, o_ref,
               kn_ref, vb_ref, ckb_ref, cvb_ref):
    j = pl.program_id(2)

    @pl.when(j == 0)
    def _():
        kn_ref[...] = _head_rmsnorm(k_ref[...], kg_ref[...]).astype(jnp.bfloat16)
        vb_ref[...] = v_ref[...].astype(jnp.bfloat16)
        ckb_ref[...] = ck_ref[...].astype(jnp.bfloat16)
        cvb_ref[...] = cv_ref[...].astype(jnp.bfloat16)

    q = (_head_rmsnorm(q_ref[...], qg_ref[...]) * HEAD_SCALE).astype(jnp.bfloat16)
    tok0 = pl.multiple_of(_na_key_row0(j) * GRID_W, GRID_W)
    k_win = kn_ref[pl.ds(tok0, NA_NK), :]
    v_win = vb_ref[pl.ds(tok0, NA_NK), :]
    nt = (((1,), (1,)), ((), ()))
    s_loc = lax.dot_general(q, k_win, nt, preferred_element_type=jnp.float32) + bias_ref[...]
    s_ctx = lax.dot_general(q, ckb_ref[...], nt, preferred_element_type=jnp.float32)
    m = jnp.maximum(jnp.max(s_loc, axis=-1, keepdims=True), jnp.max(s_ctx, axis=-1, keepdims=True))
    p_loc = jnp.exp(s_loc - m)
    p_ctx = jnp.exp(s_ctx - m)
    denom = jnp.sum(p_loc, axis=-1, keepdims=True) + jnp.sum(p_ctx, axis=-1, keepdims=True)
    o = (jnp.dot(p_loc.astype(jnp.bfloat16), v_win, preferred_element_type=jnp.float32)
         + jnp.dot(p_ctx.astype(jnp.bfloat16), cvb_ref[...], preferred_element_type=jnp.float32))
    o_ref[...] = o / denom


def _na_attention(qkv, cache_k, cache_v, rpb, q_norm, k_norm):
    bias = _na_bias_tables(rpb)
    ck = cache_k.reshape(DEC_BATCH, PAST_LEN, W_B)
    cv = cache_v.reshape(DEC_BATCH, PAST_LEN, W_B)
    hd = NA_HEAD_DIM

    def bias_case(b, h, j):
        return (h, jnp.where(j == 0, 0, jnp.where(j == NA_STEPS - 1, 2, 1)), 0, 0)

    return pl.pallas_call(
        _na_kernel,
        grid=(DEC_BATCH, NA_HEADS, NA_STEPS),
        in_specs=[
            pl.BlockSpec((NA_NQ, hd), lambda b, h, j: (b * NA_STEPS + j, QKV_COL0 + h)),
            pl.BlockSpec((DEC_SEQ, hd), lambda b, h, j: (b, QKV_COL0 + NA_HEADS + h)),
            pl.BlockSpec((DEC_SEQ, hd), lambda b, h, j: (b, QKV_COL0 + 2 * NA_HEADS + h)),
            pl.BlockSpec((None, PAST_LEN, hd), lambda b, h, j: (b, 0, h)),
            pl.BlockSpec((None, PAST_LEN, hd), lambda b, h, j: (b, 0, h)),
            pl.BlockSpec((None, None, NA_NQ, NA_NK), bias_case),
            pl.BlockSpec((1, hd), lambda b, h, j: (0, 0)),
            pl.BlockSpec((1, hd), lambda b, h, j: (0, 0)),
        ],
        out_specs=pl.BlockSpec((NA_NQ, hd), lambda b, h, j: (b * NA_STEPS + j, h)),
        out_shape=jax.ShapeDtypeStruct((DEC_BATCH * DEC_SEQ, W_B), jnp.float32),
        scratch_shapes=[pltpu.VMEM((DEC_SEQ, hd), jnp.bfloat16),
                        pltpu.VMEM((DEC_SEQ, hd), jnp.bfloat16),
                        pltpu.VMEM((PAST_LEN, hd), jnp.bfloat16),
                        pltpu.VMEM((PAST_LEN, hd), jnp.bfloat16)],
        compiler_params=pltpu.CompilerParams(
            dimension_semantics=("parallel", "parallel", "arbitrary"),
            vmem_limit_bytes=VMEM_LIMIT_BYTES),
        name="na_attention",
    )(qkv, qkv, qkv, ck, cv, bias, q_norm.reshape(1, hd), k_norm.reshape(1, hd))


def _ctx_kernel(q_ref, k_ref, v_ref, qg_ref, kg_ref, o_ref, kn_ref, vo_ref):
    q = (_head_rmsnorm(q_ref[...], qg_ref[...]) * HEAD_SCALE).astype(jnp.bfloat16)
    kn = _head_rmsnorm(k_ref[...], kg_ref[...])
    v = v_ref[...]
    kn_ref[...] = kn
    vo_ref[...] = v
    nt = (((1,), (1,)), ((), ()))
    s = lax.dot_general(q, kn.astype(jnp.bfloat16), nt, preferred_element_type=jnp.float32)
    p = jnp.exp(s - jnp.max(s, axis=-1, keepdims=True))
    denom = jnp.sum(p, axis=-1, keepdims=True)
    o = jnp.dot(p.astype(jnp.bfloat16), v.astype(jnp.bfloat16), preferred_element_type=jnp.float32)
    o_ref[...] = o / denom


def _ctx_attention_call(qkv, q_norm, k_norm):
    hd = NA_HEAD_DIM
    out = jax.ShapeDtypeStruct((BATCH * SEQ, W_B), jnp.float32)
    ospec = pl.BlockSpec((SEQ, hd), lambda b, h: (b, h))
    return pl.pallas_call(
        _ctx_kernel,
        grid=(BATCH, NA_HEADS),
        in_specs=[
            pl.BlockSpec((SEQ, hd), lambda b, h: (b, QKV_COL0 + h)),
            pl.BlockSpec((SEQ, hd), lambda b, h: (b, QKV_COL0 + NA_HEADS + h)),
            pl.BlockSpec((SEQ, hd), lambda b, h: (b, QKV_COL0 + 2 * NA_HEADS + h)),
            pl.BlockSpec((1, hd), lambda b, h: (0, 0)),
            pl.BlockSpec((1, hd), lambda b, h: (0, 0)),
        ],
        out_specs=[ospec, ospec, ospec],
        out_shape=[out, out, out],
        compiler_params=pltpu.CompilerParams(
            dimension_semantics=("parallel", "parallel"),
            vmem_limit_bytes=VMEM_LIMIT_BYTES),
        name="ctx_attention",
    )(qkv, qkv, qkv, q_norm.reshape(1, hd), k_norm.reshape(1, hd))


def _even_mixer(h, w_in, conv_a, q_norm, k_norm, rpb, w_out, ctx_kv):
    bsz, L, _ = h.shape
    proj = _mm(h.reshape(bsz * L, D_MODEL), w_in)
    a_b, a_c, a_x = (proj[:, i * W_A:(i + 1) * W_A].reshape(bsz, L, W_A) for i in range(3))
    y_a = a_b * _dwconv(a_c * a_x, conv_a)
    if ctx_kv is None:
        y_b, kn, v = _ctx_attention_call(proj, q_norm, k_norm)
        kv = (kn.reshape(bsz, L, NA_HEADS, NA_HEAD_DIM), v.reshape(bsz, L, NA_HEADS, NA_HEAD_DIM))
    else:
        y_b = _na_attention(proj, ctx_kv[0], ctx_kv[1], rpb, q_norm, k_norm)
        kv = None
    return _mm3(jnp.concatenate([y_a, y_b.reshape(bsz, L, W_B)], axis=-1), w_out), kv


def _hyena_filter(L, w1, b1, f1, w2, b2, f2, w3):
    f32 = jnp.float32
    t = jnp.linspace(0.0, 1.0, L, dtype=f32)[:, None]
    bands = (HY_EMB - 1) // 2
    ang = 2 * math.pi * jnp.arange(L, dtype=f32)[:, None] / L
    freqs = jnp.linspace(1e-4, bands - 1, bands, dtype=f32)[None, :]
    z = jnp.concatenate([t, jnp.cos(freqs * ang), -jnp.sin(freqs * ang)], axis=-1)
    hid = jnp.sin(f1.astype(f32) * (z @ w1.astype(f32) + b1.astype(f32)))
    hid = jnp.sin(f2.astype(f32) * (hid @ w2.astype(f32) + b2.astype(f32)))
    hf = (hid @ w3.astype(f32)).reshape(L, 2, W_D)
    max_decay = math.log(HY_TARGET) / HY_FAST_PCT
    min_decay = math.log(HY_TARGET) / HY_SLOW_PCT
    deltas = jnp.linspace(min_decay, max_decay, W_D, dtype=f32)
    hf = hf * jnp.exp(-t * jnp.abs(deltas))[:, None, :]
    fwd, bwd = hf[:, 0], hf[:, 1]
    return jnp.concatenate([fwd, jnp.zeros((1, W_D), f32), bwd[:0:-1]], axis=0)


def _bidir_fftconv(u, filt, bias):
    L = u.shape[1]
    uf = u.astype(jnp.float32)
    spec = jnp.fft.rfft(uf, n=2 * L, axis=1) * jnp.fft.rfft(filt, n=2 * L, axis=0)[None]
    y = jnp.fft.irfft(spec, n=2 * L, axis=1)[:, :L]
    return (y + uf * bias.astype(jnp.float32)).astype(u.dtype)


def _odd_mixer(h, w_in, conf_dw, conf_dw_b, conf_ln_g, conf_ln_b, hy_short, hy_short_b,
               hy_w1, hy_b1, hy_f1, hy_w2, hy_b2, hy_f2, hy_w3, hy_bias, w_out):
    L = h.shape[1]
    c_a, c_g, hy = jnp.split(_mm3(h, w_in), [W_C, 2 * W_C], axis=-1)
    u = c_a * jax.nn.sigmoid(c_g)
    u = jax.nn.silu(_layernorm(_dwconv(u, conf_dw, conf_dw_b), conf_ln_g, conf_ln_b))
    x0, x1, vv = jnp.split(_dwconv(hy, hy_short, hy_short_b), [W_D, 2 * W_D], axis=-1)
    filt = _hyena_filter(L, hy_w1, hy_b1, hy_f1, hy_w2, hy_b2, hy_f2, hy_w3)
    z = x0 * _bidir_fftconv(vv * x1, filt, hy_bias)
    return _mm3(jnp.concatenate([u, z], axis=-1), w_out)


def _conv_ffn(h, w_in, conv, w_out):
    a, g = jnp.split(_mm3(h, w_in), 2, axis=-1)
    return _mm3(jax.nn.gelu(_dwconv(a, conv)) * g, w_out, tm=256)


def kernel(x_prompt, x_sample, cache_k, cache_v, c, c_ctx, ada_w, ada_b, norm_mix, norm_ffn,
           e_w_in, e_conv_a, e_q_norm, e_k_norm, e_rpb, e_w_out,
           o_w_in, o_conf_dw, o_conf_dw_b, o_conf_ln_g, o_conf_ln_b, o_hy_short, o_hy_short_b,
           o_hy_w1, o_hy_b1, o_hy_f1, o_hy_w2, o_hy_b2, o_hy_f2, o_hy_w3, o_hy_bias, o_w_out,
           ffn_in, ffn_conv, ffn_out):
    xp, xs = x_prompt, x_sample
    ks_new, vs_new = [], []
    for layer in range(DEPTH):
        j = layer // 2
        mod_p = _adaln(c_ctx[None, :], ada_w[layer], ada_b[layer])
        mod_s = _adaln(c, ada_w[layer], ada_b[layer])
        hp = _modulate(_rmsnorm(xp, norm_mix[layer]), mod_p[:, 0], mod_p[:, 1])
        hs = _modulate(_rmsnorm(xs, norm_mix[layer]), mod_s[:, 0], mod_s[:, 1])
        if layer % 2 == 0:
            ev = (e_w_in[j], e_conv_a[j], e_q_norm[j], e_k_norm[j], e_rpb[j], e_w_out[j])
            mp, (kp, vp) = _even_mixer(hp, *ev, None)
            ms, _ = _even_mixer(hs, *ev, (cache_k[:, j], cache_v[:, j]))
            ks_new.append(kp)
            vs_new.append(vp)
        else:
            od = (o_w_in[j], o_conf_dw[j], o_conf_dw_b[j], o_conf_ln_g[j], o_conf_ln_b[j],
                  o_hy_short[j], o_hy_short_b[j], o_hy_w1[j], o_hy_b1[j], o_hy_f1[j],
                  o_hy_w2[j], o_hy_b2[j], o_hy_f2[j], o_hy_w3[j], o_hy_bias[j], o_w_out[j])
            mp = _odd_mixer(hp, *od)
            ms = _odd_mixer(hs, *od)
        xp = xp + mod_p[:, 2][:, None, :] * mp
        xs = xs + mod_s[:, 2][:, None, :] * ms
        hp = _modulate(_rmsnorm(xp, norm_ffn[layer]), mod_p[:, 3], mod_p[:, 4])
        hs = _modulate(_rmsnorm(xs, norm_ffn[layer]), mod_s[:, 3], mod_s[:, 4])
        xp = xp + mod_p[:, 5][:, None, :] * _conv_ffn(hp, ffn_in[layer], ffn_conv[layer], ffn_out[layer])
        xs = xs + mod_s[:, 5][:, None, :] * _conv_ffn(hs, ffn_in[layer], ffn_conv[layer], ffn_out[layer])
    new_cache_k = jnp.stack(ks_new, axis=1)
    new_cache_v = jnp.stack(vs_new, axis=1)
    return (xp, xs, new_cache_k, new_cache_v)
```

```python
import functools
import math

import jax
import jax.numpy as jnp
from jax import lax
from jax.experimental import pallas as pl
from jax.experimental.pallas import tpu as pltpu

D_MODEL = 2048
BATCH = 16
SEQ = 256
DEPTH = 2
DEC_BATCH = 2
DEC_SEQ = 4096
PAST_LEN = 512
GRID_W = 64
W_A = 1024
NA_HEADS = 8
NA_HEAD_DIM = 128
W_B = NA_HEADS * NA_HEAD_DIM
W_C = 1024
W_D = 1024
NA_WIN_R = 8
NA_WIN_C = 16
SHORT_K = 3
CONF_K = 31
D_FF = 5632
HY_EMB = 33
HY_HIDDEN = 64
HY_FAST_PCT = 0.3
HY_SLOW_PCT = 1.5
HY_TARGET = 1e-2
EPS = 1e-6
NEG_INF = -1e30

N_P = BATCH * SEQ
N_S = DEC_BATCH * DEC_SEQ
N_TOK = N_P + N_S
SEG = DEC_SEQ
N_SEG_PAD = 8
N_MOD = 6 * D_MODEL
HALO = 16
VMEM_LIMIT_BYTES = 56 * 1024 * 1024
BF16 = jnp.bfloat16
F32 = jnp.float32


def _params(*sem):
    return pltpu.CompilerParams(dimension_semantics=sem, vmem_limit_bytes=VMEM_LIMIT_BYTES)


def _seq_len(row0):
    return jnp.where(row0 < N_P, SEQ, DEC_SEQ)


def _mod_row(mod_ref, row0):
    return mod_ref[pl.ds(row0 // SEG, 1), :]


def _mod_spec(layer, blk, tn, index_pos):
    per = D_MODEL // tn

    def imap(*idx):
        j = idx[index_pos] if index_pos is not None else 0
        return (layer, 0, blk * per + j)

    return pl.BlockSpec((None, N_SEG_PAD, tn), imap)


def _adaln_kernel(c_ref, w_ref, b_ref, o_ref):
    c = c_ref[...]
    s = (c * jax.nn.sigmoid(c)).astype(BF16)
    o_ref[...] = jnp.dot(s, w_ref[...].astype(BF16), preferred_element_type=F32) + b_ref[...]


def _adaln(cvec, ada_w, ada_b, tn=1024):
    return pl.pallas_call(
        _adaln_kernel,
        grid=(DEPTH, N_MOD // tn),
        in_specs=[pl.BlockSpec((N_SEG_PAD, D_MODEL), lambda l, j: (0, 0)),
                  pl.BlockSpec((None, D_MODEL, tn), lambda l, j: (l, 0, j)),
                  pl.BlockSpec((None, 1, tn), lambda l, j: (l, 0, j))],
        out_specs=pl.BlockSpec((None, N_SEG_PAD, tn), lambda l, j: (l, 0, j)),
        out_shape=jax.ShapeDtypeStruct((DEPTH, N_SEG_PAD, N_MOD), F32),
        compiler_params=_params("arbitrary", "arbitrary"),
        name="adaln",
    )(cvec, ada_w, ada_b.reshape(DEPTH, 1, N_MOD))


def _norm_mod(x, g, shift, scale):
    y = x * lax.rsqrt(jnp.mean(x * x, axis=-1, keepdims=True) + EPS) * g
    return y * (1.0 + scale) + shift


def _norm_mod_kernel(x_ref, g_ref, sh_ref, sc_ref, o_ref, *, tr):
    row0 = pl.program_id(0) * tr
    o_ref[...] = _norm_mod(x_ref[...], g_ref[...], _mod_row(sh_ref, row0), _mod_row(sc_ref, row0)).astype(BF16)


def _norm_mod_call(x, g, mod, layer, blk, tr=512):
    return pl.pallas_call(
        functools.partial(_norm_mod_kernel, tr=tr),
        grid=(N_TOK // tr,),
        in_specs=[pl.BlockSpec((tr, D_MODEL), lambda i: (i, 0)),
                  pl.BlockSpec((1, D_MODEL), lambda i: (0, 0)),
                  _mod_spec(layer, blk, D_MODEL, None),
                  _mod_spec(layer, blk + 1, D_MODEL, None)],
        out_specs=pl.BlockSpec((tr, D_MODEL), lambda i: (i, 0)),
        out_shape=jax.ShapeDtypeStruct((N_TOK, D_MODEL), BF16),
        compiler_params=_params("arbitrary"),
        name="norm_mod",
    )(x, g.reshape(1, D_MODEL), mod, mod)


def _cast_kernel(w_ref, o_ref):
    o_ref[...] = w_ref[...].astype(BF16)


def _cast_bf16(w, tr=512):
    k, n = w.shape
    return pl.pallas_call(
        _cast_kernel,
        grid=(k // tr,),
        in_specs=[pl.BlockSpec((tr, n), lambda i: (i, 0))],
        out_specs=pl.BlockSpec((tr, n), lambda i: (i, 0)),
        out_shape=jax.ShapeDtypeStruct((k, n), BF16),
        compiler_params=_params("arbitrary"),
        name="cast_bf16",
    )(w)


def _proj_kernel(h_ref, w_ref, o_ref, wb_ref):
    @pl.when(pl.program_id(1) == 0)
    def _():
        wb_ref[...] = w_ref[...].astype(BF16)

    o_ref[...] = jnp.dot(h_ref[...], wb_ref[...], preferred_element_type=F32).astype(o_ref.dtype)


def _proj(h, w, tm=1024, tn=1024):
    m, k = h.shape
    n = w.shape[1]
    return pl.pallas_call(
        _proj_kernel,
        grid=(n // tn, m // tm),
        in_specs=[pl.BlockSpec((tm, k), lambda j, i: (i, 0)),
                  pl.BlockSpec((k, tn), lambda j, i: (0, j))],
        out_specs=pl.BlockSpec((tm, tn), lambda j, i: (i, j)),
        out_shape=jax.ShapeDtypeStruct((m, n), BF16),
        scratch_shapes=[pltpu.VMEM((k, tn), BF16)],
        compiler_params=_params("arbitrary", "arbitrary"),
        name="proj",
    )(h, w)


def _shift_rows(p, prev_row, next_row, row0, seq_len):
    n = p.shape[0]
    ridx = lax.broadcasted_iota(jnp.int32, (n, 1), 0)
    pos = (row0 + ridx) & (seq_len - 1)
    p_prev = jnp.where(ridx == 0, prev_row, pltpu.roll(p, 1, 0))
    p_prev = jnp.where(pos == 0, 0.0, p_prev)
    p_next = jnp.where(ridx == n - 1, next_row, pltpu.roll(p, n - 1, 0))
    p_next = jnp.where(pos == seq_len - 1, 0.0, p_next)
    return p_prev, p_next


def _sconv_kernel(ab_ref, ac_ref, ax_ref, pc_ref, px_ref, nc_ref, nx_ref, w_ref, o_ref, *, tr):
    row0 = pl.program_id(0) * tr
    p = ac_ref[...].astype(F32) * ax_ref[...].astype(F32)
    prev_row = pc_ref[HALO - 1:HALO, :].astype(F32) * px_ref[HALO - 1:HALO, :].astype(F32)
    next_row = nc_ref[0:1, :].astype(F32) * nx_ref[0:1, :].astype(F32)
    p_prev, p_next = _shift_rows(p, prev_row, next_row, row0, _seq_len(row0))
    w = w_ref[...]
    conv = w[0:1, :] * p_prev + w[1:2, :] * p + w[2:3, :] * p_next
    o_ref[...] = (ab_ref[...].astype(F32) * conv).astype(BF16)


def _halo_specs(tr, width, col):
    per = tr // HALO
    last = N_TOK // HALO - 1
    prev = pl.BlockSpec((HALO, width), lambda i: (jnp.maximum(i * per - 1, 0), col))
    nxt = pl.BlockSpec((HALO, width), lambda i: (jnp.minimum((i + 1) * per, last), col))
    return prev, nxt


def _short_gated_conv(proj, conv_a, tr=512):
    pc, nc = _halo_specs(tr, W_A, 1)
    px, nx = _halo_specs(tr, W_A, 2)
    return pl.pallas_call(
        functools.partial(_sconv_kernel, tr=tr),
        grid=(N_TOK // tr,),
        in_specs=[pl.BlockSpec((tr, W_A), lambda i: (i, 0)),
                  pl.BlockSpec((tr, W_A), lambda i: (i, 1)),
                  pl.BlockSpec((tr, W_A), lambda i: (i, 2)),
                  pc, px, nc, nx,
                  pl.BlockSpec((SHORT_K, W_A), lambda i: (0, 0))],
        out_specs=pl.BlockSpec((tr, W_A), lambda i: (i, 0)),
        out_shape=jax.ShapeDtypeStruct((N_TOK, W_A), BF16),
        compiler_params=_params("arbitrary"),
        name="short_gated_conv",
    )(proj, proj, proj, proj, proj, proj, proj, conv_a)


ROWS = DEC_SEQ // GRID_W
NA_QROWS = 8
NA_KROWS = 16
NA_NQ = NA_QROWS * GRID_W
NA_NK = NA_KROWS * GRID_W
NA_STEPS = ROWS // NA_QROWS
QKV_COL0 = 3 * W_A // NA_HEAD_DIM
HEAD_SCALE = NA_HEAD_DIM ** -0.5
NT_DIMS = (((1,), (1,)), ((), ()))


def _head_rmsnorm(x, g):
    return x * lax.rsqrt(jnp.mean(x * x, axis=-1, keepdims=True) + EPS) * g


def _na_key_row0(j):
    return jnp.clip(j * NA_QROWS - NA_WIN_R // 2, 0, ROWS - NA_KROWS)


def _na_bias_tables(rpb):
    n_dr = 2 * NA_WIN_R - 1
    rows = []
    for qc in range(GRID_W):
        cs = min(max(qc - NA_WIN_C // 2, 0), GRID_W - NA_WIN_C)
        lo = cs - qc + NA_WIN_C - 1
        rows.append(jnp.pad(rpb[:, :, lo:lo + NA_WIN_C].astype(F32),
                            ((0, 0), (0, 0), (cs, GRID_W - NA_WIN_C - cs)), constant_values=NEG_INF))
    col = jnp.stack(rows, axis=2)
    masked = jnp.full((NA_HEADS, GRID_W, GRID_W), NEG_INF, F32)
    tables = []
    for j in (0, NA_STEPS // 2, NA_STEPS - 1):
        kr0 = min(max(j * NA_QROWS - NA_WIN_R // 2, 0), ROWS - NA_KROWS)
        per_a = []
        for a in range(NA_QROWS):
            r = j * NA_QROWS + a
            r0 = min(max(r - NA_WIN_R // 2, 0), ROWS - NA_WIN_R)
            per_b = []
            for b in range(NA_KROWS):
                kr = kr0 + b
                ok = r0 <= kr < r0 + NA_WIN_R
                per_b.append(col[:, kr - r + NA_WIN_R - 1] if ok else masked)
            per_a.append(jnp.concatenate(per_b, axis=-1))
        tables.append(jnp.concatenate(per_a, axis=1))
    assert n_dr == rpb.shape[1]
    return jnp.stack(tables, axis=1)


def _na_kernel(q_ref, k_ref, v_ref, ck_ref, cv_ref, bias_ref, qg_ref, kg_ref, o_ref,
               kn_ref, ckb_ref, cvb_ref):
    j = pl.program_id(2)

    @pl.when(j == 0)
    def _():
        kn_ref[...] = _head_rmsnorm(k_ref[...].astype(F32), kg_ref[...]).astype(BF16)
        ckb_ref[...] = ck_ref[...].astype(BF16)
        cvb_ref[...] = cv_ref[...].astype(BF16)

    q = (_head_rmsnorm(q_ref[...].astype(F32), qg_ref[...]) * HEAD_SCALE).astype(BF16)
    tok0 = pl.multiple_of(_na_key_row0(j) * GRID_W, GRID_W)
    k_win = kn_ref[pl.ds(tok0, NA_NK), :]
    v_win = v_ref[pl.ds(tok0, NA_NK), :]
    s_loc = lax.dot_general(q, k_win, NT_DIMS, preferred_element_type=F32) + bias_ref[...]
    s_ctx = lax.dot_general(q, ckb_ref[...], NT_DIMS, preferred_element_type=F32)
    m = jnp.maximum(jnp.max(s_loc, axis=-1, keepdims=True), jnp.max(s_ctx, axis=-1, keepdims=True))
    p_loc = jnp.exp(s_loc - m)
    p_ctx = jnp.exp(s_ctx - m)
    denom = jnp.sum(p_loc, axis=-1, keepdims=True) + jnp.sum(p_ctx, axis=-1, keepdims=True)
    o = (jnp.dot(p_loc.astype(BF16), v_win, preferred_element_type=F32)
         + jnp.dot(p_ctx.astype(BF16), cvb_ref[...], preferred_element_type=F32))
    o_ref[...] = (o / denom).astype(BF16)


def _na_attention(proj, cache_k, cache_v, rpb, q_norm, k_norm):
    bias = _na_bias_tables(rpb)
    ck = cache_k.reshape(DEC_BATCH, PAST_LEN, W_B)
    cv = cache_v.reshape(DEC_BATCH, PAST_LEN, W_B)
    hd = NA_HEAD_DIM
    q_blk0 = N_P // NA_NQ
    kv_blk0 = N_P // DEC_SEQ

    def bias_case(b, h, j):
        return (h, jnp.where(j == 0, 0, jnp.where(j == NA_STEPS - 1, 2, 1)), 0, 0)

    return pl.pallas_call(
        _na_kernel,
        grid=(DEC_BATCH, NA_HEADS, NA_STEPS),
        in_specs=[
            pl.BlockSpec((NA_NQ, hd), lambda b, h, j: (q_blk0 + b * NA_STEPS + j, QKV_COL0 + h)),
            pl.BlockSpec((DEC_SEQ, hd), lambda b, h, j: (kv_blk0 + b, QKV_COL0 + NA_HEADS + h)),
            pl.BlockSpec((DEC_SEQ, hd), lambda b, h, j: (kv_blk0 + b, QKV_COL0 + 2 * NA_HEADS + h)),
            pl.BlockSpec((None, PAST_LEN, hd), lambda b, h, j: (b, 0, h)),
            pl.BlockSpec((None, PAST_LEN, hd), lambda b, h, j: (b, 0, h)),
            pl.BlockSpec((None, None, NA_NQ, NA_NK), bias_case),
            pl.BlockSpec((1, hd), lambda b, h, j: (0, 0)),
            pl.BlockSpec((1, hd), lambda b, h, j: (0, 0)),
        ],
        out_specs=pl.BlockSpec((NA_NQ, hd), lambda b, h, j: (b * NA_STEPS + j, h)),
        out_shape=jax.ShapeDtypeStruct((N_S, W_B), BF16),
        scratch_shapes=[pltpu.VMEM((DEC_SEQ, hd), BF16),
                        pltpu.VMEM((PAST_LEN, hd), BF16),
                        pltpu.VMEM((PAST_LEN, hd), BF16)],
        compiler_params=_params("arbitrary", "arbitrary", "arbitrary"),
        name="na_attention",
    )(proj, proj, proj, ck, cv, bias, q_norm.reshape(1, hd), k_norm.reshape(1, hd))


def _ctx_kernel(q_ref, k_ref, v_ref, qg_ref, kg_ref, o_ref, kn_ref, vo_ref):
    q = (_head_rmsnorm(q_ref[...].astype(F32), qg_ref[...]) * HEAD_SCALE).astype(BF16)
    kn = _head_rmsnorm(k_ref[...].astype(F32), kg_ref[...])
    kn_ref[...] = kn
    vo_ref[...] = v_ref[...].astype(F32)
    s = lax.dot_general(q, kn.astype(BF16), NT_DIMS, preferred_element_type=F32)
    p = jnp.exp(s - jnp.max(s, axis=-1, keepdims=True))
    denom = jnp.sum(p, axis=-1, keepdims=True)
    o = jnp.dot(p.astype(BF16), v_ref[...], preferred_element_type=F32)
    o_ref[...] = (o / denom).astype(BF16)


def _ctx_attention(proj, q_norm, k_norm):
    hd = NA_HEAD_DIM
    ospec = pl.BlockSpec((SEQ, hd), lambda b, h: (b, h))
    return pl.pallas_call(
        _ctx_kernel,
        grid=(BATCH, NA_HEADS),
        in_specs=[
            pl.BlockSpec((SEQ, hd), lambda b, h: (b, QKV_COL0 + h)),
            pl.BlockSpec((SEQ, hd), lambda b, h: (b, QKV_COL0 + NA_HEADS + h)),
            pl.BlockSpec((SEQ, hd), lambda b, h: (b, QKV_COL0 + 2 * NA_HEADS + h)),
            pl.BlockSpec((1, hd), lambda b, h: (0, 0)),
            pl.BlockSpec((1, hd), lambda b, h: (0, 0)),
        ],
        out_specs=[ospec, ospec, ospec],
        out_shape=[jax.ShapeDtypeStruct((N_P, W_B), BF16),
                   jax.ShapeDtypeStruct((N_P, W_B), F32),
                   jax.ShapeDtypeStruct((N_P, W_B), F32)],
        compiler_params=_params("arbitrary", "arbitrary"),
        name="ctx_attention",
    )(proj, proj, proj, q_norm.reshape(1, hd), k_norm.reshape(1, hd))


def _out_proj_kernel(ya_ref, yb_ref, w_ref, x_ref, gate_ref, g_ref, sh_ref, sc_ref, xo_ref, ho_ref, *, tm, ka):
    row0 = pl.program_id(0) * tm
    o = (jnp.dot(ya_ref[...], w_ref[0:ka, :], preferred_element_type=F32)
         + jnp.dot(yb_ref[...], w_ref[ka:, :], preferred_element_type=F32))
    x_new = x_ref[...] + _mod_row(gate_ref, row0) * o
    xo_ref[...] = x_new
    ho_ref[...] = _norm_mod(x_new, g_ref[...], _mod_row(sh_ref, row0), _mod_row(sc_ref, row0)).astype(BF16)


def _out_proj(ya, yb, w_bf16, x, mod, layer, g_ffn, tm=512):
    ka, kb = ya.shape[1], yb.shape[1]
    return pl.pallas_call(
        functools.partial(_out_proj_kernel, tm=tm, ka=ka),
        grid=(N_TOK // tm,),
        in_specs=[pl.BlockSpec((tm, ka), lambda i: (i, 0)),
                  pl.BlockSpec((tm, kb), lambda i: (i, 0)),
                  pl.BlockSpec((ka + kb, D_MODEL), lambda i: (0, 0)),
                  pl.BlockSpec((tm, D_MODEL), lambda i: (i, 0)),
                  _mod_spec(layer, 2, D_MODEL, None),
                  pl.BlockSpec((1, D_MODEL), lambda i: (0, 0)),
                  _mod_spec(layer, 3, D_MODEL, None),
                  _mod_spec(layer, 4, D_MODEL, None)],
        out_specs=[pl.BlockSpec((tm, D_MODEL), lambda i: (i, 0)),
                   pl.BlockSpec((tm, D_MODEL), lambda i: (i, 0))],
        out_shape=[jax.ShapeDtypeStruct((N_TOK, D_MODEL), F32),
                   jax.ShapeDtypeStruct((N_TOK, D_MODEL), BF16)],
        compiler_params=_params("arbitrary"),
        name="out_proj",
    )(ya, yb, w_bf16, x, mod, g_ffn.reshape(1, D_MODEL), mod, mod)


GELU_C = math.sqrt(2.0 / math.pi)


def _gelu_tanh(x):
    return 0.5 * x * (1.0 + jnp.tanh(GELU_C * (x + 0.044715 * (x * x * x))))


def _ffn_in_kernel(h_ref, hp_ref, hn_ref, wa_ref, wg_ref, cw_ref, o_ref, wab_ref, wgb_ref, *, tm):
    i = pl.program_id(1)

    @pl.when(i == 0)
    def _():
        wab_ref[...] = wa_ref[...].astype(BF16)
        wgb_ref[...] = wg_ref[...].astype(BF16)

    row0 = i * tm
    seq_len = _seq_len(row0)
    h = h_ref[...]
    h_ext = jnp.concatenate([hp_ref[...], h, hn_ref[...]], axis=0)
    a_ext = jnp.dot(h_ext, wab_ref[...], preferred_element_type=F32)
    g = jnp.dot(h, wgb_ref[...], preferred_element_type=F32)
    n = tm + 2 * HALO
    a_mid = a_ext[HALO:HALO + tm, :]
    a_prev = pltpu.roll(a_ext, 1, 0)[HALO:HALO + tm, :]
    a_next = pltpu.roll(a_ext, n - 1, 0)[HALO:HALO + tm, :]
    pos = (row0 + lax.broadcasted_iota(jnp.int32, (tm, 1), 0)) & (seq_len - 1)
    a_prev = jnp.where(pos == 0, 0.0, a_prev)
    a_next = jnp.where(pos == seq_len - 1, 0.0, a_next)
    cw = cw_ref[...]
    conv = cw[0:1, :] * a_prev + cw[1:2, :] * a_mid + cw[2:3, :] * a_next
    o_ref[...] = (_gelu_tanh(conv) * g).astype(BF16)


def _ffn_in(h, w_in, conv, tm=1024, tn=512):
    per = tm // HALO
    last = N_TOK // HALO - 1
    ng = D_FF // tn
    return pl.pallas_call(
        functools.partial(_ffn_in_kernel, tm=tm),
        grid=(ng, N_TOK // tm),
        in_specs=[pl.BlockSpec((tm, D_MODEL), lambda j, i: (i, 0)),
                  pl.BlockSpec((HALO, D_MODEL), lambda j, i: (jnp.maximum(i * per - 1, 0), 0)),
                  pl.BlockSpec((HALO, D_MODEL), lambda j, i: (jnp.minimum((i + 1) * per, last), 0)),
                  pl.BlockSpec((D_MODEL, tn), lambda j, i: (0, j)),
                  pl.BlockSpec((D_MODEL, tn), lambda j, i: (0, ng + j)),
                  pl.BlockSpec((SHORT_K, tn), lambda j, i: (0, j))],
        out_specs=pl.BlockSpec((tm, tn), lambda j, i: (i, j)),
        out_shape=jax.ShapeDtypeStruct((N_TOK, D_FF), BF16),
        scratch_shapes=[pltpu.VMEM((D_MODEL, tn), BF16), pltpu.VMEM((D_MODEL, tn), BF16)],
        compiler_params=_params("arbitrary", "arbitrary"),
        name="ffn_in",
    )(h, h, h, w_in, w_in, conv)


def _ffn_out_kernel(u_ref, w_ref, x_ref, gate_ref, o_ref, *, tm):
    row0 = pl.program_id(1) * tm
    o = jnp.dot(u_ref[...], w_ref[...], preferred_element_type=F32)
    o_ref[...] = x_ref[...] + _mod_row(gate_ref, row0) * o


def _ffn_out(u, w_bf16, x, mod, layer, tm=512, tn=1024):
    return pl.pallas_call(
        functools.partial(_ffn_out_kernel, tm=tm),
        grid=(D_MODEL // tn, N_TOK // tm),
        in_specs=[pl.BlockSpec((tm, D_FF), lambda j, i: (i, 0)),
                  pl.BlockSpec((D_FF, tn), lambda j, i: (0, j)),
                  pl.BlockSpec((tm, tn), lambda j, i: (i, j)),
                  _mod_spec(layer, 5, tn, 0)],
        out_specs=pl.BlockSpec((tm, tn), lambda j, i: (i, j)),
        out_shape=jax.ShapeDtypeStruct((N_TOK, D_MODEL), F32),
        compiler_params=_params("arbitrary", "arbitrary"),
        name="ffn_out",
    )(u, w_bf16, x, mod)


def _layernorm(x, g, b):
    mu = jnp.mean(x, axis=-1, keepdims=True)
    var = jnp.mean(jnp.square(x - mu), axis=-1, keepdims=True)
    return (x - mu) * lax.rsqrt(var + EPS) * g + b


def _dwconv(x, w, b=None):
    k, ch = w.shape
    y = lax.conv_general_dilated(x, w[:, None, :].astype(x.dtype), (1,), [(k // 2, k // 2)],
                                 dimension_numbers=("NWC", "WIO", "NWC"), feature_group_count=ch)
    if b is not None:
        y = y + b.astype(x.dtype)
    return y


def _hyena_filter(L, w1, b1, f1, w2, b2, f2, w3):
    t = jnp.linspace(0.0, 1.0, L, dtype=F32)[:, None]
    bands = (HY_EMB - 1) // 2
    ang = 2 * math.pi * jnp.arange(L, dtype=F32)[:, None] / L
    freqs = jnp.linspace(1e-4, bands - 1, bands, dtype=F32)[None, :]
    z = jnp.concatenate([t, jnp.cos(freqs * ang), -jnp.sin(freqs * ang)], axis=-1)
    hid = jnp.sin(f1 * (z @ w1 + b1))
    hid = jnp.sin(f2 * (hid @ w2 + b2))
    hf = (hid @ w3).reshape(L, 2, W_D)
    max_decay = math.log(HY_TARGET) / HY_FAST_PCT
    min_decay = math.log(HY_TARGET) / HY_SLOW_PCT
    deltas = jnp.linspace(min_decay, max_decay, W_D, dtype=F32)
    hf = hf * jnp.exp(-t * jnp.abs(deltas))[:, None, :]
    fwd, bwd = hf[:, 0], hf[:, 1]
    return jnp.concatenate([fwd, jnp.zeros((1, W_D), F32), bwd[:0:-1]], axis=0)


def _bidir_fftconv(u, filt, bias):
    L = u.shape[1]
    spec = jnp.fft.rfft(u, n=2 * L, axis=1) * jnp.fft.rfft(filt, n=2 * L, axis=0)[None]
    y = jnp.fft.irfft(spec, n=2 * L, axis=1)[:, :L]
    return y + u * bias


def _odd_mixer_core(proj, L, conf_dw, conf_dw_b, conf_ln_g, conf_ln_b, hy_short, hy_short_b,
                    hy_w1, hy_b1, hy_f1, hy_w2, hy_b2, hy_f2, hy_w3, hy_bias):
    proj = proj.astype(F32).reshape(-1, L, 2 * W_C + 3 * W_D)
    c_a, c_g, hy = proj[..., :W_C], proj[..., W_C:2 * W_C], proj[..., 2 * W_C:]
    u = c_a * jax.nn.sigmoid(c_g)
    u = jax.nn.silu(_layernorm(_dwconv(u, conf_dw, conf_dw_b), conf_ln_g, conf_ln_b))
    hs = _dwconv(hy, hy_short, hy_short_b)
    x0, x1, vv = hs[..., :W_D], hs[..., W_D:2 * W_D], hs[..., 2 * W_D:]
    filt = _hyena_filter(L, hy_w1, hy_b1, hy_f1, hy_w2, hy_b2, hy_f2, hy_w3)
    z = x0 * _bidir_fftconv(vv * x1, filt, hy_bias)
    n = proj.shape[0] * L
    return u.reshape(n, W_C).astype(BF16), z.reshape(n, W_D).astype(BF16)


def kernel(x_prompt, x_sample, cache_k, cache_v, c, c_ctx, ada_w, ada_b, norm_mix, norm_ffn,
           e_w_in, e_conv_a, e_q_norm, e_k_norm, e_rpb, e_w_out,
           o_w_in, o_conf_dw, o_conf_dw_b, o_conf_ln_g, o_conf_ln_b, o_hy_short, o_hy_short_b,
           o_hy_w1, o_hy_b1, o_hy_f1, o_hy_w2, o_hy_b2, o_hy_f2, o_hy_w3, o_hy_bias, o_w_out,
           ffn_in, ffn_conv, ffn_out):
    x = jnp.concatenate([x_prompt.reshape(N_P, D_MODEL), x_sample.reshape(N_S, D_MODEL)], axis=0)
    cvec = jnp.concatenate([c_ctx[None, :], c, jnp.zeros((N_SEG_PAD - 1 - DEC_BATCH, D_MODEL), F32)], axis=0)
    mod = _adaln(cvec, ada_w, ada_b)
    ks_new, vs_new = [], []
    for layer in range(DEPTH):
        j = layer // 2
        h = _norm_mod_call(x, norm_mix[layer], mod, layer, 0)
        if layer % 2 == 0:
            proj = _proj(h, e_w_in[j])
            ya = _short_gated_conv(proj, e_conv_a[j])
            yb_p, k_p, v_p = _ctx_attention(proj, e_q_norm[j], e_k_norm[j])
            yb_s = _na_attention(proj, cache_k[:, j], cache_v[:, j], e_rpb[j], e_q_norm[j], e_k_norm[j])
            yb = jnp.concatenate([yb_p, yb_s], axis=0)
            ks_new.append(k_p.reshape(BATCH, SEQ, NA_HEADS, NA_HEAD_DIM))
            vs_new.append(v_p.reshape(BATCH, SEQ, NA_HEADS, NA_HEAD_DIM))
            w_out = _cast_bf16(e_w_out[j])
        else:
            proj = _proj(h, o_w_in[j])
            od = (o_conf_dw[j], o_conf_dw_b[j], o_conf_ln_g[j], o_conf_ln_b[j],
                  o_hy_short[j], o_hy_short_b[j], o_hy_w1[j], o_hy_b1[j], o_hy_f1[j],
                  o_hy_w2[j], o_hy_b2[j], o_hy_f2[j], o_hy_w3[j], o_hy_bias[j])
            u_p, z_p = _odd_mixer_core(proj[:N_P], SEQ, *od)
            u_s, z_s = _odd_mixer_core(proj[N_P:], DEC_SEQ, *od)
            ya = jnp.concatenate([u_p, u_s], axis=0)
            yb = jnp.concatenate([z_p, z_s], axis=0)
            w_out = _cast_bf16(o_w_out[j])
        x, h_ffn = _out_proj(ya, yb, w_out, x, mod, layer, norm_ffn[layer])
        u = _ffn_in(h_ffn, ffn_in[layer], ffn_conv[layer])
        x = _ffn_out(u, _cast_bf16(ffn_out[layer]), x, mod, layer)
    xp = x[:N_P].reshape(BATCH, SEQ, D_MODEL)
    xs = x[N_P:].reshape(DEC_BATCH, DEC_SEQ, D_MODEL)
    return (xp, xs, jnp.stack(ks_new, axis=1), jnp.stack(vs_new, axis=1))
```

```python
import functools
import math

import jax
import jax.numpy as jnp
from jax import lax
from jax.experimental import pallas as pl
from jax.experimental.pallas import tpu as pltpu

D_MODEL = 2048
BATCH = 16
SEQ = 256
DEPTH = 2
DEC_BATCH = 2
DEC_SEQ = 4096
PAST_LEN = 512
GRID_W = 64
W_A = 1024
NA_HEADS = 8
NA_HEAD_DIM = 128
W_B = NA_HEADS * NA_HEAD_DIM
W_C = 1024
W_D = 1024
NA_WIN_R = 8
NA_WIN_C = 16
SHORT_K = 3
CONF_K = 31
D_FF = 5632
HY_EMB = 33
HY_HIDDEN = 64
HY_FAST_PCT = 0.3
HY_SLOW_PCT = 1.5
HY_TARGET = 1e-2
EPS = 1e-6
NEG_INF = -1e30

N_P = BATCH * SEQ
N_S = DEC_BATCH * DEC_SEQ
N_TOK = N_P + N_S
SEG = DEC_SEQ
N_SEG_PAD = 8
N_MOD = 6 * D_MODEL
HALO = 16
VMEM_LIMIT_BYTES = 56 * 1024 * 1024
BF16 = jnp.bfloat16
F32 = jnp.float32


def _params(*sem):
    return pltpu.CompilerParams(dimension_semantics=sem, vmem_limit_bytes=VMEM_LIMIT_BYTES)


def _seq_len(row0):
    return jnp.where(row0 < N_P, SEQ, DEC_SEQ)


def _mod_row(mod_ref, row0):
    return mod_ref[pl.ds(row0 // SEG, 1), :]


def _mod_spec(layer, blk, tn, index_pos):
    per = D_MODEL // tn

    def imap(*idx):
        j = idx[index_pos] if index_pos is not None else 0
        return (layer, 0, blk * per + j)

    return pl.BlockSpec((None, N_SEG_PAD, tn), imap)


def _adaln_kernel(c_ref, w_ref, b_ref, o_ref):
    c = c_ref[...]
    s = (c * jax.nn.sigmoid(c)).astype(BF16)
    o_ref[...] = jnp.dot(s, w_ref[...].astype(BF16), preferred_element_type=F32) + b_ref[...]


def _adaln(cvec, ada_w, ada_b, tn=1024):
    return pl.pallas_call(
        _adaln_kernel,
        grid=(DEPTH, N_MOD // tn),
        in_specs=[pl.BlockSpec((N_SEG_PAD, D_MODEL), lambda l, j: (0, 0)),
                  pl.BlockSpec((None, D_MODEL, tn), lambda l, j: (l, 0, j)),
                  pl.BlockSpec((None, 1, tn), lambda l, j: (l, 0, j))],
        out_specs=pl.BlockSpec((None, N_SEG_PAD, tn), lambda l, j: (l, 0, j)),
        out_shape=jax.ShapeDtypeStruct((DEPTH, N_SEG_PAD, N_MOD), F32),
        compiler_params=_params("arbitrary", "arbitrary"),
        name="adaln",
    )(cvec, ada_w, ada_b.reshape(DEPTH, 1, N_MOD))


def _norm_mod(x, g, shift, scale):
    y = x * lax.rsqrt(jnp.mean(x * x, axis=-1, keepdims=True) + EPS) * g
    return y * (1.0 + scale) + shift


def _norm_mod_kernel(x_ref, g_ref, sh_ref, sc_ref, o_ref, *, tr):
    row0 = pl.program_id(0) * tr
    o_ref[...] = _norm_mod(x_ref[...], g_ref[...], _mod_row(sh_ref, row0), _mod_row(sc_ref, row0)).astype(BF16)


def _norm_mod_call(x, g, mod, layer, blk, tr=512):
    return pl.pallas_call(
        functools.partial(_norm_mod_kernel, tr=tr),
        grid=(N_TOK // tr,),
        in_specs=[pl.BlockSpec((tr, D_MODEL), lambda i: (i, 0)),
                  pl.BlockSpec((1, D_MODEL), lambda i: (0, 0)),
                  _mod_spec(layer, blk, D_MODEL, None),
                  _mod_spec(layer, blk + 1, D_MODEL, None)],
        out_specs=pl.BlockSpec((tr, D_MODEL), lambda i: (i, 0)),
        out_shape=jax.ShapeDtypeStruct((N_TOK, D_MODEL), BF16),
        compiler_params=_params("arbitrary"),
        name="norm_mod",
    )(x, g.reshape(1, D_MODEL), mod, mod)


def _cast_kernel(w_ref, o_ref):
    o_ref[...] = w_ref[...].astype(BF16)


def _cast_bf16(w, tr=512):
    k, n = w.shape
    return pl.pallas_call(
        _cast_kernel,
        grid=(k // tr,),
        in_specs=[pl.BlockSpec((tr, n), lambda i: (i, 0))],
        out_specs=pl.BlockSpec((tr, n), lambda i: (i, 0)),
        out_shape=jax.ShapeDtypeStruct((k, n), BF16),
        compiler_params=_params("arbitrary"),
        name="cast_bf16",
    )(w)


def _proj_kernel(h_ref, w_ref, o_ref, wb_ref):
    @pl.when(pl.program_id(1) == 0)
    def _():
        wb_ref[...] = w_ref[...].astype(BF16)

    o_ref[...] = jnp.dot(h_ref[...], wb_ref[...], preferred_element_type=F32).astype(o_ref.dtype)


def _proj(h, w, tm=1024, tn=1024):
    m, k = h.shape
    n = w.shape[1]
    return pl.pallas_call(
        _proj_kernel,
        grid=(n // tn, m // tm),
        in_specs=[pl.BlockSpec((tm, k), lambda j, i: (i, 0)),
                  pl.BlockSpec((k, tn), lambda j, i: (0, j))],
        out_specs=pl.BlockSpec((tm, tn), lambda j, i: (i, j)),
        out_shape=jax.ShapeDtypeStruct((m, n), BF16),
        scratch_shapes=[pltpu.VMEM((k, tn), BF16)],
        compiler_params=_params("arbitrary", "arbitrary"),
        name="proj",
    )(h, w)


def _shift_rows(p, prev_row, next_row, row0, seq_len):
    n = p.shape[0]
    ridx = lax.broadcasted_iota(jnp.int32, (n, 1), 0)
    pos = (row0 + ridx) & (seq_len - 1)
    p_prev = jnp.where(ridx == 0, prev_row, pltpu.roll(p, 1, 0))
    p_prev = jnp.where(pos == 0, 0.0, p_prev)
    p_next = jnp.where(ridx == n - 1, next_row, pltpu.roll(p, n - 1, 0))
    p_next = jnp.where(pos == seq_len - 1, 0.0, p_next)
    return p_prev, p_next


def _sconv_kernel(ab_ref, ac_ref, ax_ref, pc_ref, px_ref, nc_ref, nx_ref, w_ref, o_ref, *, tr):
    row0 = pl.program_id(0) * tr
    p = ac_ref[...].astype(F32) * ax_ref[...].astype(F32)
    prev_row = pc_ref[HALO - 1:HALO, :].astype(F32) * px_ref[HALO - 1:HALO, :].astype(F32)
    next_row = nc_ref[0:1, :].astype(F32) * nx_ref[0:1, :].astype(F32)
    p_prev, p_next = _shift_rows(p, prev_row, next_row, row0, _seq_len(row0))
    w = w_ref[...]
    conv = w[0:1, :] * p_prev + w[1:2, :] * p + w[2:3, :] * p_next
    o_ref[...] = (ab_ref[...].astype(F32) * conv).astype(BF16)


def _halo_specs(tr, width, col):
    per = tr // HALO
    last = N_TOK // HALO - 1
    prev = pl.BlockSpec((HALO, width), lambda i: (jnp.maximum(i * per - 1, 0), col))
    nxt = pl.BlockSpec((HALO, width), lambda i: (jnp.minimum((i + 1) * per, last), col))
    return prev, nxt


def _short_gated_conv(proj, conv_a, tr=512):
    pc, nc = _halo_specs(tr, W_A, 1)
    px, nx = _halo_specs(tr, W_A, 2)
    return pl.pallas_call(
        functools.partial(_sconv_kernel, tr=tr),
        grid=(N_TOK // tr,),
        in_specs=[pl.BlockSpec((tr, W_A), lambda i: (i, 0)),
                  pl.BlockSpec((tr, W_A), lambda i: (i, 1)),
                  pl.BlockSpec((tr, W_A), lambda i: (i, 2)),
                  pc, px, nc, nx,
                  pl.BlockSpec((SHORT_K, W_A), lambda i: (0, 0))],
        out_specs=pl.BlockSpec((tr, W_A), lambda i: (i, 0)),
        out_shape=jax.ShapeDtypeStruct((N_TOK, W_A), BF16),
        compiler_params=_params("arbitrary"),
        name="short_gated_conv",
    )(proj, proj, proj, proj, proj, proj, proj, conv_a)


ROWS = DEC_SEQ // GRID_W
NA_QROWS = 8
NA_KROWS = 16
NA_NQ = NA_QROWS * GRID_W
NA_NK = NA_KROWS * GRID_W
NA_STEPS = ROWS // NA_QROWS
QKV_COL0 = 3 * W_A // NA_HEAD_DIM
HEAD_SCALE = NA_HEAD_DIM ** -0.5
NT_DIMS = (((1,), (1,)), ((), ()))


def _head_rmsnorm(x, g):
    return x * lax.rsqrt(jnp.mean(x * x, axis=-1, keepdims=True) + EPS) * g


def _na_key_row0(j):
    return jnp.clip(j * NA_QROWS - NA_WIN_R // 2, 0, ROWS - NA_KROWS)


def _na_bias_tables(rpb):
    n_dr = 2 * NA_WIN_R - 1
    rows = []
    for qc in range(GRID_W):
        cs = min(max(qc - NA_WIN_C // 2, 0), GRID_W - NA_WIN_C)
        lo = cs - qc + NA_WIN_C - 1
        rows.append(jnp.pad(rpb[:, :, lo:lo + NA_WIN_C].astype(F32),
                            ((0, 0), (0, 0), (cs, GRID_W - NA_WIN_C - cs)), constant_values=NEG_INF))
    col = jnp.stack(rows, axis=2)
    masked = jnp.full((NA_HEADS, GRID_W, GRID_W), NEG_INF, F32)
    tables = []
    for j in (0, NA_STEPS // 2, NA_STEPS - 1):
        kr0 = min(max(j * NA_QROWS - NA_WIN_R // 2, 0), ROWS - NA_KROWS)
        per_a = []
        for a in range(NA_QROWS):
            r = j * NA_QROWS + a
            r0 = min(max(r - NA_WIN_R // 2, 0), ROWS - NA_WIN_R)
            per_b = []
            for b in range(NA_KROWS):
                kr = kr0 + b
                ok = r0 <= kr < r0 + NA_WIN_R
                per_b.append(col[:, kr - r + NA_WIN_R - 1] if ok else masked)
            per_a.append(jnp.concatenate(per_b, axis=-1))
        tables.append(jnp.concatenate(per_a, axis=1))
    assert n_dr == rpb.shape[1]
    return jnp.stack(tables, axis=1)


def _na_kernel(q_ref, k_ref, v_ref, ck_ref, cv_ref, bias_ref, qg_ref, kg_ref, o_ref,
               kn_ref, ckb_ref, cvb_ref):
    j = pl.program_id(2)

    @pl.when(j == 0)
    def _():
        kn_ref[...] = _head_rmsnorm(k_ref[...].astype(F32), kg_ref[...]).astype(BF16)
        ckb_ref[...] = ck_ref[...].astype(BF16)
        cvb_ref[...] = cv_ref[...].astype(BF16)

    q = (_head_rmsnorm(q_ref[...].astype(F32), qg_ref[...]) * HEAD_SCALE).astype(BF16)
    tok0 = pl.multiple_of(_na_key_row0(j) * GRID_W, GRID_W)
    k_win = kn_ref[pl.ds(tok0, NA_NK), :]
    v_win = v_ref[pl.ds(tok0, NA_NK), :]
    s_loc = lax.dot_general(q, k_win, NT_DIMS, preferred_element_type=F32) + bias_ref[...]
    s_ctx = lax.dot_general(q, ckb_ref[...], NT_DIMS, preferred_element_type=F32)
    m = jnp.maximum(jnp.max(s_loc, axis=-1, keepdims=True), jnp.max(s_ctx, axis=-1, keepdims=True))
    p_loc = jnp.exp(s_loc - m)
    p_ctx = jnp.exp(s_ctx - m)
    denom = jnp.sum(p_loc, axis=-1, keepdims=True) + jnp.sum(p_ctx, axis=-1, keepdims=True)
    o = (jnp.dot(p_loc.astype(BF16), v_win, preferred_element_type=F32)
         + jnp.dot(p_ctx.astype(BF16), cvb_ref[...], preferred_element_type=F32))
    o_ref[...] = (o / denom).astype(BF16)


def _na_attention(proj, cache_k, cache_v, rpb, q_norm, k_norm):
    bias = _na_bias_tables(rpb)
    ck = cache_k.reshape(DEC_BATCH, PAST_LEN, W_B)
    cv = cache_v.reshape(DEC_BATCH, PAST_LEN, W_B)
    hd = NA_HEAD_DIM
    q_blk0 = N_P // NA_NQ
    kv_blk0 = N_P // DEC_SEQ

    def bias_case(b, h, j):
        return (h, jnp.where(j == 0, 0, jnp.where(j == NA_STEPS - 1, 2, 1)), 0, 0)

    return pl.pallas_call(
        _na_kernel,
        grid=(DEC_BATCH, NA_HEADS, NA_STEPS),
        in_specs=[
            pl.BlockSpec((NA_NQ, hd), lambda b, h, j: (q_blk0 + b * NA_STEPS + j, QKV_COL0 + h)),
            pl.BlockSpec((DEC_SEQ, hd), lambda b, h, j: (kv_blk0 + b, QKV_COL0 + NA_HEADS + h)),
            pl.BlockSpec((DEC_SEQ, hd), lambda b, h, j: (kv_blk0 + b, QKV_COL0 + 2 * NA_HEADS + h)),
            pl.BlockSpec((None, PAST_LEN, hd), lambda b, h, j: (b, 0, h)),
            pl.BlockSpec((None, PAST_LEN, hd), lambda b, h, j: (b, 0, h)),
            pl.BlockSpec((None, None, NA_NQ, NA_NK), bias_case),
            pl.BlockSpec((1, hd), lambda b, h, j: (0, 0)),
            pl.BlockSpec((1, hd), lambda b, h, j: (0, 0)),
        ],
        out_specs=pl.BlockSpec((NA_NQ, hd), lambda b, h, j: (b * NA_STEPS + j, h)),
        out_shape=jax.ShapeDtypeStruct((N_S, W_B), BF16),
        scratch_shapes=[pltpu.VMEM((DEC_SEQ, hd), BF16),
                        pltpu.VMEM((PAST_LEN, hd), BF16),
                        pltpu.VMEM((PAST_LEN, hd), BF16)],
        compiler_params=_params("arbitrary", "arbitrary", "arbitrary"),
        name="na_attention",
    )(proj, proj, proj, ck, cv, bias, q_norm.reshape(1, hd), k_norm.reshape(1, hd))


def _ctx_kernel(q_ref, k_ref, v_ref, qg_ref, kg_ref, o_ref, kn_ref, vo_ref):
    q = (_head_rmsnorm(q_ref[...].astype(F32), qg_ref[...]) * HEAD_SCALE).astype(BF16)
    kn = _head_rmsnorm(k_ref[...].astype(F32), kg_ref[...])
    kn_ref[...] = kn
    vo_ref[...] = v_ref[...].astype(F32)
    s = lax.dot_general(q, kn.astype(BF16), NT_DIMS, preferred_element_type=F32)
    p = jnp.exp(s - jnp.max(s, axis=-1, keepdims=True))
    denom = jnp.sum(p, axis=-1, keepdims=True)
    o = jnp.dot(p.astype(BF16), v_ref[...], preferred_element_type=F32)
    o_ref[...] = (o / denom).astype(BF16)


def _ctx_attention(proj, q_norm, k_norm):
    hd = NA_HEAD_DIM
    ospec = pl.BlockSpec((SEQ, hd), lambda b, h: (b, h))
    return pl.pallas_call(
        _ctx_kernel,
        grid=(BATCH, NA_HEADS),
        in_specs=[
            pl.BlockSpec((SEQ, hd), lambda b, h: (b, QKV_COL0 + h)),
            pl.BlockSpec((SEQ, hd), lambda b, h: (b, QKV_COL0 + NA_HEADS + h)),
            pl.BlockSpec((SEQ, hd), lambda b, h: (b, QKV_COL0 + 2 * NA_HEADS + h)),
            pl.BlockSpec((1, hd), lambda b, h: (0, 0)),
            pl.BlockSpec((1, hd), lambda b, h: (0, 0)),
        ],
        out_specs=[ospec, ospec, ospec],
        out_shape=[jax.ShapeDtypeStruct((N_P, W_B), BF16),
                   jax.ShapeDtypeStruct((N_P, W_B), F32),
                   jax.ShapeDtypeStruct((N_P, W_B), F32)],
        compiler_params=_params("arbitrary", "arbitrary"),
        name="ctx_attention",
    )(proj, proj, proj, q_norm.reshape(1, hd), k_norm.reshape(1, hd))


def _out_proj_kernel(ya_ref, yb_ref, w_ref, x_ref, gate_ref, g_ref, sh_ref, sc_ref, xo_ref, ho_ref, *, tm, ka):
    row0 = pl.program_id(0) * tm
    o = (jnp.dot(ya_ref[...], w_ref[0:ka, :], preferred_element_type=F32)
         + jnp.dot(yb_ref[...], w_ref[ka:, :], preferred_element_type=F32))
    x_new = x_ref[...] + _mod_row(gate_ref, row0) * o
    xo_ref[...] = x_new
    ho_ref[...] = _norm_mod(x_new, g_ref[...], _mod_row(sh_ref, row0), _mod_row(sc_ref, row0)).astype(BF16)


def _out_proj(ya, yb, w_bf16, x, mod, layer, g_ffn, tm=512):
    ka, kb = ya.shape[1], yb.shape[1]
    return pl.pallas_call(
        functools.partial(_out_proj_kernel, tm=tm, ka=ka),
        grid=(N_TOK // tm,),
        in_specs=[pl.BlockSpec((tm, ka), lambda i: (i, 0)),
                  pl.BlockSpec((tm, kb), lambda i: (i, 0)),
                  pl.BlockSpec((ka + kb, D_MODEL), lambda i: (0, 0)),
                  pl.BlockSpec((tm, D_MODEL), lambda i: (i, 0)),
                  _mod_spec(layer, 2, D_MODEL, None),
                  pl.BlockSpec((1, D_MODEL), lambda i: (0, 0)),
                  _mod_spec(layer, 3, D_MODEL, None),
                  _mod_spec(layer, 4, D_MODEL, None)],
        out_specs=[pl.BlockSpec((tm, D_MODEL), lambda i: (i, 0)),
                   pl.BlockSpec((tm, D_MODEL), lambda i: (i, 0))],
        out_shape=[jax.ShapeDtypeStruct((N_TOK, D_MODEL), F32),
                   jax.ShapeDtypeStruct((N_TOK, D_MODEL), BF16)],
        compiler_params=_params("arbitrary"),
        name="out_proj",
    )(ya, yb, w_bf16, x, mod, g_ffn.reshape(1, D_MODEL), mod, mod)


GELU_C = math.sqrt(2.0 / math.pi)


def _gelu_tanh(x):
    return 0.5 * x * (1.0 + jnp.tanh(GELU_C * (x + 0.044715 * (x * x * x))))


def _ffn_in_kernel(h_ref, hp_ref, hn_ref, wa_ref, wg_ref, cw_ref, o_ref, wab_ref, wgb_ref, *, tm):
    i = pl.program_id(1)

    @pl.when(i == 0)
    def _():
        wab_ref[...] = wa_ref[...].astype(BF16)
        wgb_ref[...] = wg_ref[...].astype(BF16)

    row0 = i * tm
    seq_len = _seq_len(row0)
    h = h_ref[...]
    h_ext = jnp.concatenate([hp_ref[...], h, hn_ref[...]], axis=0)
    a_ext = jnp.dot(h_ext, wab_ref[...], preferred_element_type=F32)
    g = jnp.dot(h, wgb_ref[...], preferred_element_type=F32)
    n = tm + 2 * HALO
    a_mid = a_ext[HALO:HALO + tm, :]
    a_prev = pltpu.roll(a_ext, 1, 0)[HALO:HALO + tm, :]
    a_next = pltpu.roll(a_ext, n - 1, 0)[HALO:HALO + tm, :]
    pos = (row0 + lax.broadcasted_iota(jnp.int32, (tm, 1), 0)) & (seq_len - 1)
    a_prev = jnp.where(pos == 0, 0.0, a_prev)
    a_next = jnp.where(pos == seq_len - 1, 0.0, a_next)
    cw = cw_ref[...]
    conv = cw[0:1, :] * a_prev + cw[1:2, :] * a_mid + cw[2:3, :] * a_next
    o_ref[...] = (_gelu_tanh(conv) * g).astype(BF16)


def _ffn_in(h, w_in, conv, tm=1024, tn=512):
    per = tm // HALO
    last = N_TOK // HALO - 1
    ng = D_FF // tn
    return pl.pallas_call(
        functools.partial(_ffn_in_kernel, tm=tm),
        grid=(ng, N_TOK // tm),
        in_specs=[pl.BlockSpec((tm, D_MODEL), lambda j, i: (i, 0)),
                  pl.BlockSpec((HALO, D_MODEL), lambda j, i: (jnp.maximum(i * per - 1, 0), 0)),
                  pl.BlockSpec((HALO, D_MODEL), lambda j, i: (jnp.minimum((i + 1) * per, last), 0)),
                  pl.BlockSpec((D_MODEL, tn), lambda j, i: (0, j)),
                  pl.BlockSpec((D_MODEL, tn), lambda j, i: (0, ng + j)),
                  pl.BlockSpec((SHORT_K, tn), lambda j, i: (0, j))],
        out_specs=pl.BlockSpec((tm, tn), lambda j, i: (i, j)),
        out_shape=jax.ShapeDtypeStruct((N_TOK, D_FF), BF16),
        scratch_shapes=[pltpu.VMEM((D_MODEL, tn), BF16), pltpu.VMEM((D_MODEL, tn), BF16)],
        compiler_params=_params("arbitrary", "arbitrary"),
        name="ffn_in",
    )(h, h, h, w_in, w_in, conv)


def _ffn_out_kernel(u_ref, w_ref, x_ref, gate_ref, o_ref, *, tm):
    row0 = pl.program_id(1) * tm
    o = jnp.dot(u_ref[...], w_ref[...], preferred_element_type=F32)
    o_ref[...] = x_ref[...] + _mod_row(gate_ref, row0) * o


def _ffn_out(u, w_bf16, x, mod, layer, tm=512, tn=1024):
    return pl.pallas_call(
        functools.partial(_ffn_out_kernel, tm=tm),
        grid=(D_MODEL // tn, N_TOK // tm),
        in_specs=[pl.BlockSpec((tm, D_FF), lambda j, i: (i, 0)),
                  pl.BlockSpec((D_FF, tn), lambda j, i: (0, j)),
                  pl.BlockSpec((tm, tn), lambda j, i: (i, j)),
                  _mod_spec(layer, 5, tn, 0)],
        out_specs=pl.BlockSpec((tm, tn), lambda j, i: (i, j)),
        out_shape=jax.ShapeDtypeStruct((N_TOK, D_MODEL), F32),
        compiler_params=_params("arbitrary", "arbitrary"),
        name="ffn_out",
    )(u, w_bf16, x, mod)


CONF_TR = 256
CONF_PAD = CONF_K // 2


def _conformer_kernel(ca_ref, cg_ref, pa_ref, pg_ref, na_ref, ng_ref, w_ref, b_ref, lg_ref, lb_ref, o_ref):
    tr = CONF_TR
    row0 = pl.program_id(0) * tr
    seq_len = _seq_len(row0)
    pos0 = row0 & (seq_len - 1)

    def glu(a_ref, g_ref):
        return a_ref[...].astype(F32) * jax.nn.sigmoid(g_ref[...].astype(F32))

    prev = jnp.where(pos0 == 0, 0.0, glu(pa_ref, pg_ref))
    nxt = jnp.where(pos0 + tr == seq_len, 0.0, glu(na_ref, ng_ref))
    u_ext = jnp.concatenate([prev, glu(ca_ref, cg_ref), nxt], axis=0)
    w = w_ref[...]
    acc = jnp.zeros((tr, W_C), F32) + b_ref[...]
    base = HALO - CONF_PAD
    for r in range(8):
        n_taps = len(range(r, CONF_K, 8))
        span = tr + 8 * (n_taps - 1)
        u_r = u_ext[base + r:base + r + span, :]
        for a in range(n_taps):
            k = 8 * a + r
            acc = acc + w[k:k + 1, :] * u_r[8 * a:8 * a + tr, :]
    mu = jnp.mean(acc, axis=-1, keepdims=True)
    d = acc - mu
    var = jnp.mean(d * d, axis=-1, keepdims=True)
    y = d * lax.rsqrt(var + EPS) * lg_ref[...] + lb_ref[...]
    o_ref[...] = (y * jax.nn.sigmoid(y)).astype(BF16)


def _conformer(proj, dw, dw_b, ln_g, ln_b):
    pa, na = _halo_specs(CONF_TR, W_C, 0)
    pg, ng = _halo_specs(CONF_TR, W_C, 1)
    row = pl.BlockSpec((1, W_C), lambda i: (0, 0))
    return pl.pallas_call(
        _conformer_kernel,
        grid=(N_TOK // CONF_TR,),
        in_specs=[pl.BlockSpec((CONF_TR, W_C), lambda i: (i, 0)),
                  pl.BlockSpec((CONF_TR, W_C), lambda i: (i, 1)),
                  pa, pg, na, ng,
                  pl.BlockSpec((CONF_K, W_C), lambda i: (0, 0)), row, row, row],
        out_specs=pl.BlockSpec((CONF_TR, W_C), lambda i: (i, 0)),
        out_shape=jax.ShapeDtypeStruct((N_TOK, W_C), BF16),
        compiler_params=_params("arbitrary"),
        name="conformer",
    )(proj, proj, proj, proj, proj, proj, dw, dw_b.reshape(1, W_C), ln_g.reshape(1, W_C), ln_b.reshape(1, W_C))


HY_COL0 = 2 * W_C // W_D


def _hy_short_kernel(*refs, tr):
    (x0_ref, x1_ref, vv_ref, p0_ref, p1_ref, pv_ref, n0_ref, n1_ref, nv_ref,
     w0_ref, w1_ref, wv_ref, b0_ref, b1_ref, bv_ref, x0o_ref, wo_ref) = refs
    row0 = pl.program_id(0) * tr
    seq_len = _seq_len(row0)

    def conv(c_ref, p_ref, n_ref, w_ref, b_ref):
        p = c_ref[...].astype(F32)
        p_prev, p_next = _shift_rows(p, p_ref[HALO - 1:HALO, :].astype(F32), n_ref[0:1, :].astype(F32),
                                     row0, seq_len)
        w = w_ref[...]
        return w[0:1, :] * p_prev + w[1:2, :] * p + w[2:3, :] * p_next + b_ref[...]

    x0o_ref[...] = conv(x0_ref, p0_ref, n0_ref, w0_ref, b0_ref).astype(BF16)
    x1 = conv(x1_ref, p1_ref, n1_ref, w1_ref, b1_ref)
    vv = conv(vv_ref, pv_ref, nv_ref, wv_ref, bv_ref)
    wo_ref[...] = (vv * x1).astype(BF16)


def _hy_short(proj, hy_short, hy_short_b, tr=512):
    main = [pl.BlockSpec((tr, W_D), functools.partial(lambda i, c: (i, c), c=HY_COL0 + c)) for c in range(3)]
    halos = [_halo_specs(tr, W_D, HY_COL0 + c) for c in range(3)]
    wspec = [pl.BlockSpec((SHORT_K, W_D), functools.partial(lambda i, c: (0, c), c=c)) for c in range(3)]
    bspec = [pl.BlockSpec((1, W_D), functools.partial(lambda i, c: (0, c), c=c)) for c in range(3)]
    out = jax.ShapeDtypeStruct((N_TOK, W_D), BF16)
    ospec = pl.BlockSpec((tr, W_D), lambda i: (i, 0))
    b2 = hy_short_b.reshape(1, 3 * W_D)
    return pl.pallas_call(
        functools.partial(_hy_short_kernel, tr=tr),
        grid=(N_TOK // tr,),
        in_specs=main + [h[0] for h in halos] + [h[1] for h in halos] + wspec + bspec,
        out_specs=[ospec, ospec],
        out_shape=[out, out],
        compiler_params=_params("arbitrary"),
        name="hy_short",
    )(*([proj] * 9), hy_short, hy_short, hy_short, b2, b2, b2)


HIGHEST = lax.Precision.HIGHEST
HY_EMB_PAD = 128


def _dft_matrices(L):
    blk = 64
    k = jnp.arange(L, dtype=jnp.int32)[:, None]

    def tables(n):
        m = (k * n[None, :]) % (2 * L)
        ang = m.astype(F32) * (math.pi / L)
        return jnp.cos(ang), jnp.sin(ang)

    c_hi, s_hi = tables(jnp.arange(L // blk, dtype=jnp.int32) * blk)
    c_lo, s_lo = tables(jnp.arange(blk, dtype=jnp.int32))
    c = c_hi[:, :, None] * c_lo[:, None, :] - s_hi[:, :, None] * s_lo[:, None, :]
    s = s_hi[:, :, None] * c_lo[:, None, :] + c_hi[:, :, None] * s_lo[:, None, :]
    return c.reshape(L, L).astype(BF16), s.reshape(L, L).astype(BF16)


def _hy_features(L):
    t = jnp.linspace(0.0, 1.0, L, dtype=F32)[:, None]
    bands = (HY_EMB - 1) // 2
    ang = 2 * math.pi * jnp.arange(L, dtype=F32)[:, None] / L
    freqs = jnp.linspace(1e-4, bands - 1, bands, dtype=F32)[None, :]
    z = jnp.concatenate([t, jnp.cos(freqs * ang), -jnp.sin(freqs * ang)], axis=-1)
    return jnp.pad(z, ((0, 0), (0, HY_EMB_PAD - HY_EMB))), t


def _hy_filter_kernel(z_ref, t_ref, w1_ref, b1_ref, f1_ref, w2_ref, b2_ref, f2_ref, w3_ref, dl_ref,
                      fs_ref, fd_ref, hn_ref, *, tr):
    i = pl.program_id(0)
    hid = jnp.sin(f1_ref[...] * (jnp.dot(z_ref[...], w1_ref[...], precision=HIGHEST,
                                         preferred_element_type=F32) + b1_ref[...]))
    hid = jnp.sin(f2_ref[...] * (jnp.dot(hid, w2_ref[...], precision=HIGHEST,
                                         preferred_element_type=F32) + b2_ref[...]))
    hf = jnp.dot(hid, w3_ref[...], precision=HIGHEST, preferred_element_type=F32)
    decay = jnp.exp(-t_ref[...] * jnp.abs(dl_ref[...]))
    ridx = i * tr + lax.broadcasted_iota(jnp.int32, (tr, 1), 0)
    fwd = hf[:, :W_D] * decay
    bwd = jnp.where(ridx == 0, 0.0, hf[:, W_D:] * decay)
    fsum = fwd + bwd
    fs_ref[...] = fsum.astype(BF16)
    fd_ref[...] = (bwd - fwd).astype(BF16)
    sgn = (1 - 2 * (ridx & 1)).astype(F32)
    part = jnp.sum(fsum * sgn, axis=0, keepdims=True)

    @pl.when(i == 0)
    def _():
        hn_ref[...] = jnp.zeros_like(hn_ref)

    hn_ref[0:1, :] += part


def _hy_filter(L, w1, b1, f1, w2, b2, f2, w3):
    tr = min(L, 512)
    z, t = _hy_features(L)
    max_decay = math.log(HY_TARGET) / HY_FAST_PCT
    min_decay = math.log(HY_TARGET) / HY_SLOW_PCT
    deltas = jnp.linspace(min_decay, max_decay, W_D, dtype=F32)[None, :]
    w1p = jnp.pad(w1, ((0, HY_EMB_PAD - HY_EMB), (0, 0)))
    full = lambda shape: pl.BlockSpec(shape, lambda i: (0, 0))
    return pl.pallas_call(
        functools.partial(_hy_filter_kernel, tr=tr),
        grid=(L // tr,),
        in_specs=[pl.BlockSpec((tr, HY_EMB_PAD), lambda i: (i, 0)),
                  pl.BlockSpec((tr, 1), lambda i: (i, 0)),
                  full((HY_EMB_PAD, HY_HIDDEN)), full((1, HY_HIDDEN)), full((1, HY_HIDDEN)),
                  full((HY_HIDDEN, HY_HIDDEN)), full((1, HY_HIDDEN)), full((1, HY_HIDDEN)),
                  full((HY_HIDDEN, 2 * W_D)), full((1, W_D))],
        out_specs=[pl.BlockSpec((tr, W_D), lambda i: (i, 0)),
                   pl.BlockSpec((tr, W_D), lambda i: (i, 0)),
                   pl.BlockSpec((8, W_D), lambda i: (0, 0))],
        out_shape=[jax.ShapeDtypeStruct((L, W_D), BF16),
                   jax.ShapeDtypeStruct((L, W_D), BF16),
                   jax.ShapeDtypeStruct((8, W_D), F32)],
        compiler_params=_params("arbitrary"),
        name="hy_filter",
    )(z, t, w1p, b1.reshape(1, -1), f1.reshape(1, -1), w2, b2.reshape(1, -1), f2.reshape(1, -1), w3, deltas)


def _hy_spec_kernel(c_ref, s_ref, fs_ref, fd_ref, hc_ref, hs_ref):
    hc_ref[...] = jnp.dot(c_ref[...], fs_ref[...], preferred_element_type=F32)
    hs_ref[...] = jnp.dot(s_ref[...], fd_ref[...], preferred_element_type=F32)


def _hy_spectrum(cmat, smat, fsum, fdif, tk):
    L = cmat.shape[0]
    tile = pl.BlockSpec((tk, L), lambda i: (i, 0))
    full = pl.BlockSpec((L, W_D), lambda i: (0, 0))
    out = pl.BlockSpec((tk, W_D), lambda i: (i, 0))
    return pl.pallas_call(
        _hy_spec_kernel,
        grid=(L // tk,),
        in_specs=[tile, tile, full, full],
        out_specs=[out, out],
        out_shape=[jax.ShapeDtypeStruct((L, W_D), F32)] * 2,
        compiler_params=_params("arbitrary"),
        name="hy_spectrum",
    )(cmat, smat, fsum, fdif)


def _hy_fwd_kernel(c_ref, s_ref, w_ref, hc_ref, hs_ref, hn_ref, yc_ref, ys_ref, yn_ref, *, L, tk):
    kt = pl.program_id(1)
    w = w_ref[...]
    xc = jnp.dot(c_ref[...], w, preferred_element_type=F32)
    xs = jnp.dot(s_ref[...], w, preferred_element_type=F32)
    hc = hc_ref[...]
    hs = hs_ref[...]
    kidx = kt * tk + lax.broadcasted_iota(jnp.int32, (tk, 1), 0)
    om = jnp.where(kidx == 0, 0.5 / L, 1.0 / L)
    yc_ref[...] = (om * (xc * hc + xs * hs)).astype(BF16)
    ys_ref[...] = ((1.0 / L) * (xs * hc - xc * hs)).astype(BF16)

    @pl.when(kt == 0)
    def _():
        n = lax.broadcasted_iota(jnp.int32, (L, 1), 0)
        sgn = (1 - 2 * (n & 1)).astype(F32)
        xn = jnp.sum(w.astype(F32) * sgn, axis=0, keepdims=True)
        yn_ref[...] = jnp.broadcast_to(xn * hn_ref[0:1, :] * (0.5 / L), yn_ref.shape)


def _hy_forward(cmat, smat, w, hc, hs, hn, L, nb, blk0, tk):
    tile = pl.BlockSpec((tk, L), lambda b, k: (k, 0))
    htile = pl.BlockSpec((tk, W_D), lambda b, k: (k, 0))
    out = pl.BlockSpec((tk, W_D), lambda b, k: (b * (L // tk) + k, 0))
    return pl.pallas_call(
        functools.partial(_hy_fwd_kernel, L=L, tk=tk),
        grid=(nb, L // tk),
        in_specs=[tile, tile,
                  pl.BlockSpec((L, W_D), lambda b, k: (blk0 + b, 0)),
                  htile, htile,
                  pl.BlockSpec((8, W_D), lambda b, k: (0, 0))],
        out_specs=[out, out, pl.BlockSpec((None, 8, W_D), lambda b, k: (b, 0, 0))],
        out_shape=[jax.ShapeDtypeStruct((nb * L, W_D), BF16),
                   jax.ShapeDtypeStruct((nb * L, W_D), BF16),
                   jax.ShapeDtypeStruct((nb, 8, W_D), F32)],
        compiler_params=_params("arbitrary", "arbitrary"),
        name="hy_forward",
    )(cmat, smat, w, hc, hs, hn)


def _hy_inv_kernel(c_ref, s_ref, yc_ref, ys_ref, yn_ref, x0_ref, w_ref, bias_ref, z_ref, *, tt):
    ti = pl.program_id(1)
    y = (jnp.dot(c_ref[...], yc_ref[...], preferred_element_type=F32)
         + jnp.dot(s_ref[...], ys_ref[...], preferred_element_type=F32))
    t = ti * tt + lax.broadcasted_iota(jnp.int32, (tt, 1), 0)
    sgn = (1 - 2 * (t & 1)).astype(F32)
    y = y + sgn * yn_ref[0:1, :] + w_ref[...].astype(F32) * bias_ref[...]
    z_ref[...] = (x0_ref[...].astype(F32) * y).astype(BF16)


def _hy_inverse(cmat, smat, yc, ys, yn, x0, w, bias, L, nb, blk0, tt):
    per = L // tt
    tile = pl.BlockSpec((tt, L), lambda b, t: (t, 0))
    seq = pl.BlockSpec((L, W_D), lambda b, t: (b, 0))
    rows_in = pl.BlockSpec((tt, W_D), lambda b, t: ((blk0 + b) * per + t, 0))
    return pl.pallas_call(
        functools.partial(_hy_inv_kernel, tt=tt),
        grid=(nb, per),
        in_specs=[tile, tile, seq, seq,
                  pl.BlockSpec((None, 8, W_D), lambda b, t: (b, 0, 0)),
                  rows_in, rows_in,
                  pl.BlockSpec((1, W_D), lambda b, t: (0, 0))],
        out_specs=pl.BlockSpec((tt, W_D), lambda b, t: (b * per + t, 0)),
        out_shape=jax.ShapeDtypeStruct((nb * L, W_D), BF16),
        compiler_params=_params("arbitrary", "arbitrary"),
        name="hy_inverse",
    )(cmat, smat, yc, ys, yn, x0, w, bias.reshape(1, W_D))


def _hyena_long_conv(x0, w, L, nb, blk0, filt_params, bias):
    tk = min(L, 256)
    cmat, smat = _dft_matrices(L)
    fsum, fdif, hn = _hy_filter(L, *filt_params)
    hc, hs = _hy_spectrum(cmat, smat, fsum, fdif, tk)
    yc, ys, yn = _hy_forward(cmat, smat, w, hc, hs, hn, L, nb, blk0, tk)
    return _hy_inverse(cmat, smat, yc, ys, yn, x0, w, bias, L, nb, blk0, tk)


def kernel(x_prompt, x_sample, cache_k, cache_v, c, c_ctx, ada_w, ada_b, norm_mix, norm_ffn,
           e_w_in, e_conv_a, e_q_norm, e_k_norm, e_rpb, e_w_out,
           o_w_in, o_conf_dw, o_conf_dw_b, o_conf_ln_g, o_conf_ln_b, o_hy_short, o_hy_short_b,
           o_hy_w1, o_hy_b1, o_hy_f1, o_hy_w2, o_hy_b2, o_hy_f2, o_hy_w3, o_hy_bias, o_w_out,
           ffn_in, ffn_conv, ffn_out):
    x = jnp.concatenate([x_prompt.reshape(N_P, D_MODEL), x_sample.reshape(N_S, D_MODEL)], axis=0)
    cvec = jnp.concatenate([c_ctx[None, :], c, jnp.zeros((N_SEG_PAD - 1 - DEC_BATCH, D_MODEL), F32)], axis=0)
    mod = _adaln(cvec, ada_w, ada_b)
    ks_new, vs_new = [], []
    for layer in range(DEPTH):
        j = layer // 2
        h = _norm_mod_call(x, norm_mix[layer], mod, layer, 0)
        if layer % 2 == 0:
            proj = _proj(h, e_w_in[j])
            ya = _short_gated_conv(proj, e_conv_a[j])
            yb_p, k_p, v_p = _ctx_attention(proj, e_q_norm[j], e_k_norm[j])
            yb_s = _na_attention(proj, cache_k[:, j], cache_v[:, j], e_rpb[j], e_q_norm[j], e_k_norm[j])
            yb = jnp.concatenate([yb_p, yb_s], axis=0)
            ks_new.append(k_p.reshape(BATCH, SEQ, NA_HEADS, NA_HEAD_DIM))
            vs_new.append(v_p.reshape(BATCH, SEQ, NA_HEADS, NA_HEAD_DIM))
            w_out = _cast_bf16(e_w_out[j])
        else:
            proj = _proj(h, o_w_in[j])
            ya = _conformer(proj, o_conf_dw[j], o_conf_dw_b[j], o_conf_ln_g[j], o_conf_ln_b[j])
            x0, w = _hy_short(proj, o_hy_short[j], o_hy_short_b[j])
            fp = (o_hy_w1[j], o_hy_b1[j], o_hy_f1[j], o_hy_w2[j], o_hy_b2[j], o_hy_f2[j], o_hy_w3[j])
            z_p = _hyena_long_conv(x0, w, SEQ, BATCH, 0, fp, o_hy_bias[j])
            z_s = _hyena_long_conv(x0, w, DEC_SEQ, DEC_BATCH, N_P // DEC_SEQ, fp, o_hy_bias[j])
            yb = jnp.concatenate([z_p, z_s], axis=0)
            w_out = _cast_bf16(o_w_out[j])
        x, h_ffn = _out_proj(ya, yb, w_out, x, mod, layer, norm_ffn[layer])
        u = _ffn_in(h_ffn, ffn_in[layer], ffn_conv[layer])
        x = _ffn_out(u, _cast_bf16(ffn_out[layer]), x, mod, layer)
    xp = x[:N_P].reshape(BATCH, SEQ, D_MODEL)
    xs = x[N_P:].reshape(DEC_BATCH, DEC_SEQ, D_MODEL)
    return (xp, xs, jnp.stack(ks_new, axis=1), jnp.stack(vs_new, axis=1))
```

```python
import functools
import math

import jax
import jax.numpy as jnp
from jax import lax
from jax.experimental import pallas as pl
from jax.experimental.pallas import tpu as pltpu

D_MODEL = 2048
BATCH = 16
SEQ = 256
DEPTH = 2
DEC_BATCH = 2
DEC_SEQ = 4096
PAST_LEN = 512
GRID_W = 64
W_A = 1024
NA_HEADS = 8
NA_HEAD_DIM = 128
W_B = NA_HEADS * NA_HEAD_DIM
W_C = 1024
W_D = 1024
NA_WIN_R = 8
NA_WIN_C = 16
SHORT_K = 3
CONF_K = 31
D_FF = 5632
HY_EMB = 33
HY_HIDDEN = 64
HY_FAST_PCT = 0.3
HY_SLOW_PCT = 1.5
HY_TARGET = 1e-2
EPS = 1e-6
NEG_INF = -1e30

N_P = BATCH * SEQ
N_S = DEC_BATCH * DEC_SEQ
N_TOK = N_P + N_S
SEG = DEC_SEQ
N_SEG_PAD = 8
N_MOD = 6 * D_MODEL
HALO = 16
VMEM_LIMIT_BYTES = 56 * 1024 * 1024
BF16 = jnp.bfloat16
F32 = jnp.float32


def _params(*sem):
    return pltpu.CompilerParams(dimension_semantics=sem, vmem_limit_bytes=VMEM_LIMIT_BYTES)


def _seq_len(row0):
    return jnp.where(row0 < N_P, SEQ, DEC_SEQ)


def _mod_row(mod_ref, row0):
    return mod_ref[pl.ds(row0 // SEG, 1), :]


def _mod_spec(layer, blk, tn, index_pos):
    per = D_MODEL // tn

    def imap(*idx):
        j = idx[index_pos] if index_pos is not None else 0
        return (layer, 0, blk * per + j)

    return pl.BlockSpec((None, N_SEG_PAD, tn), imap)


def _adaln_kernel(c_ref, w_ref, b_ref, o_ref):
    c = c_ref[...]
    s = (c * jax.nn.sigmoid(c)).astype(BF16)
    o_ref[...] = jnp.dot(s, w_ref[...].astype(BF16), preferred_element_type=F32) + b_ref[...]


def _adaln(cvec, ada_w, ada_b, tn=1024):
    return pl.pallas_call(
        _adaln_kernel,
        grid=(DEPTH, N_MOD // tn),
        in_specs=[pl.BlockSpec((N_SEG_PAD, D_MODEL), lambda l, j: (0, 0)),
                  pl.BlockSpec((None, D_MODEL, tn), lambda l, j: (l, 0, j)),
                  pl.BlockSpec((None, 1, tn), lambda l, j: (l, 0, j))],
        out_specs=pl.BlockSpec((None, N_SEG_PAD, tn), lambda l, j: (l, 0, j)),
        out_shape=jax.ShapeDtypeStruct((DEPTH, N_SEG_PAD, N_MOD), F32),
        compiler_params=_params("arbitrary", "arbitrary"),
        name="adaln",
    )(cvec, ada_w, ada_b.reshape(DEPTH, 1, N_MOD))


def _norm_mod(x, g, shift, scale):
    y = x * lax.rsqrt(jnp.mean(x * x, axis=-1, keepdims=True) + EPS) * g
    return y * (1.0 + scale) + shift


def _stream_specs(x_parts, tr, width, pos):
    if len(x_parts) == 1:
        return [pl.BlockSpec((tr, width), lambda *idx: (idx[pos], 0))]
    n_pt = N_P // tr
    return [pl.BlockSpec((tr, width), lambda *idx: (jnp.minimum(idx[pos], n_pt - 1), 0)),
            pl.BlockSpec((tr, width), lambda *idx: (jnp.maximum(idx[pos] - n_pt, 0), 0))]


def _stream_tile(x_refs, row0):
    if len(x_refs) == 1:
        return x_refs[0][...]
    return jnp.where(row0 < N_P, x_refs[0][...], x_refs[1][...])


def _norm_mod_kernel(*refs, tr, n_parts):
    x_refs, (g_ref, sh_ref, sc_ref, o_ref) = refs[:n_parts], refs[n_parts:]
    row0 = pl.program_id(0) * tr
    x = _stream_tile(x_refs, row0)
    o_ref[...] = _norm_mod(x, g_ref[...], _mod_row(sh_ref, row0), _mod_row(sc_ref, row0)).astype(BF16)


def _norm_mod_call(x_parts, g, mod, layer, blk, tr=512):
    return pl.pallas_call(
        functools.partial(_norm_mod_kernel, tr=tr, n_parts=len(x_parts)),
        grid=(N_TOK // tr,),
        in_specs=_stream_specs(x_parts, tr, D_MODEL, 0) + [
            pl.BlockSpec((1, D_MODEL), lambda i: (0, 0)),
            _mod_spec(layer, blk, D_MODEL, None),
            _mod_spec(layer, blk + 1, D_MODEL, None)],
        out_specs=pl.BlockSpec((tr, D_MODEL), lambda i: (i, 0)),
        out_shape=jax.ShapeDtypeStruct((N_TOK, D_MODEL), BF16),
        compiler_params=_params("arbitrary"),
        name="norm_mod",
    )(*x_parts, g.reshape(1, D_MODEL), mod, mod)


def _cast_kernel(w_ref, o_ref):
    o_ref[...] = w_ref[...].astype(BF16)


def _cast_bf16(w, layer, tr=512):
    _, k, n = w.shape
    return pl.pallas_call(
        _cast_kernel,
        grid=(k // tr,),
        in_specs=[pl.BlockSpec((None, tr, n), lambda i: (layer, i, 0))],
        out_specs=pl.BlockSpec((tr, n), lambda i: (i, 0)),
        out_shape=jax.ShapeDtypeStruct((k, n), BF16),
        compiler_params=_params("arbitrary"),
        name="cast_bf16",
    )(w)


def _proj_kernel(h_ref, w_ref, o_ref, wb_ref):
    @pl.when(pl.program_id(1) == 0)
    def _():
        wb_ref[...] = w_ref[...].astype(BF16)

    o_ref[...] = jnp.dot(h_ref[...], wb_ref[...], preferred_element_type=F32).astype(o_ref.dtype)


def _proj(h, w, tm=1024, tn=1024):
    m, k = h.shape
    n = w.shape[1]
    return pl.pallas_call(
        _proj_kernel,
        grid=(n // tn, m // tm),
        in_specs=[pl.BlockSpec((tm, k), lambda j, i: (i, 0)),
                  pl.BlockSpec((k, tn), lambda j, i: (0, j))],
        out_specs=pl.BlockSpec((tm, tn), lambda j, i: (i, j)),
        out_shape=jax.ShapeDtypeStruct((m, n), BF16),
        scratch_shapes=[pltpu.VMEM((k, tn), BF16)],
        compiler_params=_params("arbitrary", "arbitrary"),
        name="proj",
    )(h, w)


def _shift_rows(p, prev_row, next_row, row0, seq_len):
    n = p.shape[0]
    ridx = lax.broadcasted_iota(jnp.int32, (n, 1), 0)
    pos = (row0 + ridx) & (seq_len - 1)
    p_prev = jnp.where(ridx == 0, prev_row, pltpu.roll(p, 1, 0))
    p_prev = jnp.where(pos == 0, 0.0, p_prev)
    p_next = jnp.where(ridx == n - 1, next_row, pltpu.roll(p, n - 1, 0))
    p_next = jnp.where(pos == seq_len - 1, 0.0, p_next)
    return p_prev, p_next


def _sconv_kernel(ab_ref, ac_ref, ax_ref, pc_ref, px_ref, nc_ref, nx_ref, w_ref, o_ref, *, tr):
    row0 = pl.program_id(0) * tr
    p = ac_ref[...].astype(F32) * ax_ref[...].astype(F32)
    prev_row = pc_ref[HALO - 1:HALO, :].astype(F32) * px_ref[HALO - 1:HALO, :].astype(F32)
    next_row = nc_ref[0:1, :].astype(F32) * nx_ref[0:1, :].astype(F32)
    p_prev, p_next = _shift_rows(p, prev_row, next_row, row0, _seq_len(row0))
    w = w_ref[...]
    conv = w[0:1, :] * p_prev + w[1:2, :] * p + w[2:3, :] * p_next
    o_ref[...] = (ab_ref[...].astype(F32) * conv).astype(BF16)


def _halo_specs(tr, width, col):
    per = tr // HALO
    last = N_TOK // HALO - 1
    prev = pl.BlockSpec((HALO, width), lambda i: (jnp.maximum(i * per - 1, 0), col))
    nxt = pl.BlockSpec((HALO, width), lambda i: (jnp.minimum((i + 1) * per, last), col))
    return prev, nxt


def _short_gated_conv(proj, conv_a, tr=512):
    pc, nc = _halo_specs(tr, W_A, 1)
    px, nx = _halo_specs(tr, W_A, 2)
    return pl.pallas_call(
        functools.partial(_sconv_kernel, tr=tr),
        grid=(N_TOK // tr,),
        in_specs=[pl.BlockSpec((tr, W_A), lambda i: (i, 0)),
                  pl.BlockSpec((tr, W_A), lambda i: (i, 1)),
                  pl.BlockSpec((tr, W_A), lambda i: (i, 2)),
                  pc, px, nc, nx,
                  pl.BlockSpec((SHORT_K, W_A), lambda i: (0, 0))],
        out_specs=pl.BlockSpec((tr, W_A), lambda i: (i, 0)),
        out_shape=jax.ShapeDtypeStruct((N_TOK, W_A), BF16),
        compiler_params=_params("arbitrary"),
        name="short_gated_conv",
    )(proj, proj, proj, proj, proj, proj, proj, conv_a)


ROWS = DEC_SEQ // GRID_W
NA_QROWS = 8
NA_KROWS = 16
NA_NQ = NA_QROWS * GRID_W
NA_NK = NA_KROWS * GRID_W
NA_STEPS = ROWS // NA_QROWS
QKV_COL0 = 3 * W_A // NA_HEAD_DIM
HEAD_SCALE = NA_HEAD_DIM ** -0.5
NT_DIMS = (((1,), (1,)), ((), ()))


def _head_rmsnorm(x, g):
    return x * lax.rsqrt(jnp.mean(x * x, axis=-1, keepdims=True) + EPS) * g


def _na_key_row0(j):
    return jnp.clip(j * NA_QROWS - NA_WIN_R // 2, 0, ROWS - NA_KROWS)


NA_SLOT_LO = NA_QROWS
NA_SLOTS = 2 * NA_KROWS


def _na_geometry(j):
    kr0 = min(max(j * NA_QROWS - NA_WIN_R // 2, 0), ROWS - NA_KROWS)
    per_a = []
    for a in range(NA_QROWS):
        r = j * NA_QROWS + a
        r0 = min(max(r - NA_WIN_R // 2, 0), ROWS - NA_WIN_R)
        per_a.append((kr0 - r + NA_WIN_R - 1 + NA_SLOT_LO, r0 - kr0))
    return kr0, per_a


def _na_bias_slots(rpb):
    rows = []
    for qc in range(GRID_W):
        cs = min(max(qc - NA_WIN_C // 2, 0), GRID_W - NA_WIN_C)
        lo = cs - qc + NA_WIN_C - 1
        rows.append(jnp.pad(rpb[:, :, lo:lo + NA_WIN_C].astype(F32),
                            ((0, 0), (0, 0), (cs, GRID_W - NA_WIN_C - cs)), constant_values=NEG_INF))
    col = jnp.stack(rows, axis=1)
    n_dr = rpb.shape[1]
    col = jnp.pad(col, ((0, 0), (0, 0), (NA_SLOT_LO, NA_SLOTS + 1 - NA_SLOT_LO - n_dr), (0, 0)),
                  constant_values=NEG_INF)
    even = col[:, :, :NA_SLOTS].reshape(NA_HEADS, GRID_W, NA_SLOTS * GRID_W)
    odd = col[:, :, 1:].reshape(NA_HEADS, GRID_W, NA_SLOTS * GRID_W)
    return even, odd


def _na_build_bias(j_static, even_ref, odd_ref, bias_ref):
    _, per_a = _na_geometry(j_static)
    key_row = lax.broadcasted_iota(jnp.int32, (1, NA_NK), 1) // GRID_W
    for a, (m, lo) in enumerate(per_a):
        src = even_ref if m % 2 == 0 else odd_ref
        start = (m - m % 2) * GRID_W
        slab = src[:, start:start + NA_NK]
        ok = (key_row >= lo) & (key_row < lo + NA_WIN_R)
        bias_ref[a * GRID_W:(a + 1) * GRID_W, :] = jnp.where(ok, slab, NEG_INF)


def _na_kernel(q_ref, k_ref, v_ref, ck_ref, cv_ref, even_ref, odd_ref, qg_ref, kg_ref, o_ref,
               kn_ref, ckb_ref, cvb_ref, bias_ref):
    j = pl.program_id(2)

    @pl.when(j == 0)
    def _():
        kn_ref[...] = _head_rmsnorm(k_ref[...].astype(F32), kg_ref[...]).astype(BF16)
        ckb_ref[...] = ck_ref[...].astype(BF16)
        cvb_ref[...] = cv_ref[...].astype(BF16)

    for j_static in (0, 1, NA_STEPS - 1):
        @pl.when(j == j_static)
        def _(j_static=j_static):
            _na_build_bias(j_static, even_ref, odd_ref, bias_ref)

    q = (_head_rmsnorm(q_ref[...].astype(F32), qg_ref[...]) * HEAD_SCALE).astype(BF16)
    tok0 = pl.multiple_of(_na_key_row0(j) * GRID_W, GRID_W)
    k_win = kn_ref[pl.ds(tok0, NA_NK), :]
    v_win = v_ref[pl.ds(tok0, NA_NK), :]
    s_loc = lax.dot_general(q, k_win, NT_DIMS, preferred_element_type=F32) + bias_ref[...]
    s_ctx = lax.dot_general(q, ckb_ref[...], NT_DIMS, preferred_element_type=F32)
    m = jnp.maximum(jnp.max(s_loc, axis=-1, keepdims=True), jnp.max(s_ctx, axis=-1, keepdims=True))
    p_loc = jnp.exp(s_loc - m)
    p_ctx = jnp.exp(s_ctx - m)
    denom = jnp.sum(p_loc, axis=-1, keepdims=True) + jnp.sum(p_ctx, axis=-1, keepdims=True)
    o = (jnp.dot(p_loc.astype(BF16), v_win, preferred_element_type=F32)
         + jnp.dot(p_ctx.astype(BF16), cvb_ref[...], preferred_element_type=F32))
    o_ref[...] = (o / denom).astype(BF16)


def _na_attention(proj, cache_k, cache_v, rpb, q_norm, k_norm):
    assert NA_STEPS >= 3 and all(_na_geometry(j)[1] == _na_geometry(1)[1] for j in range(1, NA_STEPS - 1))
    even, odd = _na_bias_slots(rpb)
    ck = cache_k.reshape(DEC_BATCH, PAST_LEN, W_B)
    cv = cache_v.reshape(DEC_BATCH, PAST_LEN, W_B)
    hd = NA_HEAD_DIM
    q_blk0 = N_P // NA_NQ
    kv_blk0 = N_P // DEC_SEQ
    slots = pl.BlockSpec((None, GRID_W, NA_SLOTS * GRID_W), lambda b, h, j: (h, 0, 0))
    return pl.pallas_call(
        _na_kernel,
        grid=(DEC_BATCH, NA_HEADS, NA_STEPS),
        in_specs=[
            pl.BlockSpec((NA_NQ, hd), lambda b, h, j: (q_blk0 + b * NA_STEPS + j, QKV_COL0 + h)),
            pl.BlockSpec((DEC_SEQ, hd), lambda b, h, j: (kv_blk0 + b, QKV_COL0 + NA_HEADS + h)),
            pl.BlockSpec((DEC_SEQ, hd), lambda b, h, j: (kv_blk0 + b, QKV_COL0 + 2 * NA_HEADS + h)),
            pl.BlockSpec((None, PAST_LEN, hd), lambda b, h, j: (b, 0, h)),
            pl.BlockSpec((None, PAST_LEN, hd), lambda b, h, j: (b, 0, h)),
            slots, slots,
            pl.BlockSpec((1, hd), lambda b, h, j: (0, 0)),
            pl.BlockSpec((1, hd), lambda b, h, j: (0, 0)),
        ],
        out_specs=pl.BlockSpec((NA_NQ, hd), lambda b, h, j: (b * NA_STEPS + j, h)),
        out_shape=jax.ShapeDtypeStruct((N_S, W_B), BF16),
        scratch_shapes=[pltpu.VMEM((DEC_SEQ, hd), BF16),
                        pltpu.VMEM((PAST_LEN, hd), BF16),
                        pltpu.VMEM((PAST_LEN, hd), BF16),
                        pltpu.VMEM((NA_NQ, NA_NK), F32)],
        compiler_params=_params("arbitrary", "arbitrary", "arbitrary"),
        name="na_attention",
    )(proj, proj, proj, ck, cv, even, odd, q_norm.reshape(1, hd), k_norm.reshape(1, hd))


def _ctx_kernel(q_ref, k_ref, v_ref, qg_ref, kg_ref, o_ref, kn_ref, vo_ref):
    q = (_head_rmsnorm(q_ref[...].astype(F32), qg_ref[...]) * HEAD_SCALE).astype(BF16)
    kn = _head_rmsnorm(k_ref[...].astype(F32), kg_ref[...])
    kn_ref[...] = kn
    vo_ref[...] = v_ref[...].astype(F32)
    s = lax.dot_general(q, kn.astype(BF16), NT_DIMS, preferred_element_type=F32)
    p = jnp.exp(s - jnp.max(s, axis=-1, keepdims=True))
    denom = jnp.sum(p, axis=-1, keepdims=True)
    o = jnp.dot(p.astype(BF16), v_ref[...], preferred_element_type=F32)
    o_ref[...] = (o / denom).astype(BF16)


def _ctx_attention(proj, q_norm, k_norm):
    hd = NA_HEAD_DIM
    ospec = pl.BlockSpec((SEQ, hd), lambda b, h: (b, h))
    return pl.pallas_call(
        _ctx_kernel,
        grid=(BATCH, NA_HEADS),
        in_specs=[
            pl.BlockSpec((SEQ, hd), lambda b, h: (b, QKV_COL0 + h)),
            pl.BlockSpec((SEQ, hd), lambda b, h: (b, QKV_COL0 + NA_HEADS + h)),
            pl.BlockSpec((SEQ, hd), lambda b, h: (b, QKV_COL0 + 2 * NA_HEADS + h)),
            pl.BlockSpec((1, hd), lambda b, h: (0, 0)),
            pl.BlockSpec((1, hd), lambda b, h: (0, 0)),
        ],
        out_specs=[ospec, ospec, ospec],
        out_shape=[jax.ShapeDtypeStruct((N_P, W_B), BF16),
                   jax.ShapeDtypeStruct((N_P, W_B), F32),
                   jax.ShapeDtypeStruct((N_P, W_B), F32)],
        compiler_params=_params("arbitrary", "arbitrary"),
        name="ctx_attention",
    )(proj, proj, proj, q_norm.reshape(1, hd), k_norm.reshape(1, hd))


def _out_proj_kernel(*refs, tm, ka, n_parts):
    ya_ref, yb_ref, w_ref = refs[:3]
    x_refs = refs[3:3 + n_parts]
    gate_ref, g_ref, sh_ref, sc_ref, xo_ref, ho_ref = refs[3 + n_parts:]
    row0 = pl.program_id(0) * tm
    o = (jnp.dot(ya_ref[...], w_ref[0:ka, :], preferred_element_type=F32)
         + jnp.dot(yb_ref[...], w_ref[ka:, :], preferred_element_type=F32))
    x_new = _stream_tile(x_refs, row0) + _mod_row(gate_ref, row0) * o
    xo_ref[...] = x_new
    ho_ref[...] = _norm_mod(x_new, g_ref[...], _mod_row(sh_ref, row0), _mod_row(sc_ref, row0)).astype(BF16)


def _out_proj(ya, yb, w_bf16, x_parts, mod, layer, g_ffn, tm=512):
    ka, kb = ya.shape[1], yb.shape[1]
    return pl.pallas_call(
        functools.partial(_out_proj_kernel, tm=tm, ka=ka, n_parts=len(x_parts)),
        grid=(N_TOK // tm,),
        in_specs=[pl.BlockSpec((tm, ka), lambda i: (i, 0)),
                  pl.BlockSpec((tm, kb), lambda i: (i, 0)),
                  pl.BlockSpec((ka + kb, D_MODEL), lambda i: (0, 0))]
        + _stream_specs(x_parts, tm, D_MODEL, 0) + [
            _mod_spec(layer, 2, D_MODEL, None),
            pl.BlockSpec((1, D_MODEL), lambda i: (0, 0)),
            _mod_spec(layer, 3, D_MODEL, None),
            _mod_spec(layer, 4, D_MODEL, None)],
        out_specs=[pl.BlockSpec((tm, D_MODEL), lambda i: (i, 0)),
                   pl.BlockSpec((tm, D_MODEL), lambda i: (i, 0))],
        out_shape=[jax.ShapeDtypeStruct((N_TOK, D_MODEL), F32),
                   jax.ShapeDtypeStruct((N_TOK, D_MODEL), BF16)],
        compiler_params=_params("arbitrary"),
        name="out_proj",
    )(ya, yb, w_bf16, *x_parts, mod, g_ffn.reshape(1, D_MODEL), mod, mod)


GELU_C = math.sqrt(2.0 / math.pi)


def _gelu_tanh(x):
    return 0.5 * x * (1.0 + jnp.tanh(GELU_C * (x + 0.044715 * (x * x * x))))


def _ffn_in_kernel(h_ref, hp_ref, hn_ref, wa_ref, wg_ref, cw_ref, o_ref, wab_ref, wgb_ref, *, tm, tn):
    i = pl.program_id(1)

    @pl.when(i == 0)
    def _():
        wab_ref[...] = wa_ref[...].astype(BF16)
        wgb_ref[...] = wg_ref[...].astype(BF16)

    row0 = i * tm
    seq_len = _seq_len(row0)
    h = h_ref[...]
    h_ext = jnp.concatenate([hp_ref[...], h, hn_ref[...]], axis=0)
    n = tm + 2 * HALO
    pos = (row0 + lax.broadcasted_iota(jnp.int32, (tm, 1), 0)) & (seq_len - 1)
    first = pos == 0
    last = pos == seq_len - 1
    a_ext = jnp.dot(h_ext, wab_ref[...], preferred_element_type=F32)
    g = jnp.dot(h, wgb_ref[...], preferred_element_type=F32)
    a_mid = a_ext[HALO:HALO + tm, :]
    a_prev = jnp.where(first, 0.0, pltpu.roll(a_ext, 1, 0)[HALO:HALO + tm, :])
    a_next = jnp.where(last, 0.0, pltpu.roll(a_ext, n - 1, 0)[HALO:HALO + tm, :])
    cw = cw_ref[...]
    conv = cw[0:1, :] * a_prev + cw[1:2, :] * a_mid + cw[2:3, :] * a_next
    o_ref[...] = (_gelu_tanh(conv) * g).astype(BF16)


def _ffn_in(h, w_in, conv, layer, tm=1024, tn=512):
    per = tm // HALO
    last = N_TOK // HALO - 1
    ng = D_FF // tn
    return pl.pallas_call(
        functools.partial(_ffn_in_kernel, tm=tm, tn=tn),
        grid=(ng, N_TOK // tm),
        in_specs=[pl.BlockSpec((tm, D_MODEL), lambda j, i: (i, 0)),
                  pl.BlockSpec((HALO, D_MODEL), lambda j, i: (jnp.maximum(i * per - 1, 0), 0)),
                  pl.BlockSpec((HALO, D_MODEL), lambda j, i: (jnp.minimum((i + 1) * per, last), 0)),
                  pl.BlockSpec((None, D_MODEL, tn), lambda j, i: (layer, 0, j)),
                  pl.BlockSpec((None, D_MODEL, tn), lambda j, i: (layer, 0, ng + j)),
                  pl.BlockSpec((None, SHORT_K, tn), lambda j, i: (layer, 0, j))],
        out_specs=pl.BlockSpec((tm, tn), lambda j, i: (i, j)),
        out_shape=jax.ShapeDtypeStruct((N_TOK, D_FF), BF16),
        scratch_shapes=[pltpu.VMEM((D_MODEL, tn), BF16), pltpu.VMEM((D_MODEL, tn), BF16)],
        compiler_params=_params("arbitrary", "arbitrary"),
        name="ffn_in",
    )(h, h, h, w_in, w_in, conv)


def _ffn_out_kernel(u_ref, w_ref, x_ref, gate_ref, o_ref, *, tm, row_start):
    row0 = row_start + pl.program_id(1) * tm
    o = jnp.dot(u_ref[...], w_ref[...], preferred_element_type=F32)
    o_ref[...] = x_ref[...] + _mod_row(gate_ref, row0) * o


def _ffn_out(u, w_bf16, x, mod, layer, row_start=0, n_rows=N_TOK, tm=512, tn=1024):
    t0 = row_start // tm
    return pl.pallas_call(
        functools.partial(_ffn_out_kernel, tm=tm, row_start=row_start),
        grid=(D_MODEL // tn, n_rows // tm),
        in_specs=[pl.BlockSpec((tm, D_FF), lambda j, i: (t0 + i, 0)),
                  pl.BlockSpec((D_FF, tn), lambda j, i: (0, j)),
                  pl.BlockSpec((tm, tn), lambda j, i: (t0 + i, j)),
                  _mod_spec(layer, 5, tn, 0)],
        out_specs=pl.BlockSpec((tm, tn), lambda j, i: (i, j)),
        out_shape=jax.ShapeDtypeStruct((n_rows, D_MODEL), F32),
        compiler_params=_params("arbitrary", "arbitrary"),
        name="ffn_out",
    )(u, w_bf16, x, mod)


CONF_TR = 256
CONF_PAD = CONF_K // 2


def _conformer_kernel(ca_ref, cg_ref, pa_ref, pg_ref, na_ref, ng_ref, w_ref, b_ref, lg_ref, lb_ref, o_ref):
    tr = CONF_TR
    row0 = pl.program_id(0) * tr
    seq_len = _seq_len(row0)
    pos0 = row0 & (seq_len - 1)

    def glu(a_ref, g_ref):
        return a_ref[...].astype(F32) * jax.nn.sigmoid(g_ref[...].astype(F32))

    prev = jnp.where(pos0 == 0, 0.0, glu(pa_ref, pg_ref))
    nxt = jnp.where(pos0 + tr == seq_len, 0.0, glu(na_ref, ng_ref))
    u_ext = jnp.concatenate([prev, glu(ca_ref, cg_ref), nxt], axis=0)
    w = w_ref[...]
    acc = jnp.zeros((tr, W_C), F32) + b_ref[...]
    n = tr + 2 * HALO
    base = HALO - CONF_PAD
    for r in range(8):
        u_r = pltpu.roll(u_ext, n - (base + r), 0)
        for a in range(len(range(r, CONF_K, 8))):
            k = 8 * a + r
            acc = acc + w[k:k + 1, :] * u_r[8 * a:8 * a + tr, :]
    mu = jnp.mean(acc, axis=-1, keepdims=True)
    d = acc - mu
    var = jnp.mean(d * d, axis=-1, keepdims=True)
    y = d * lax.rsqrt(var + EPS) * lg_ref[...] + lb_ref[...]
    o_ref[...] = (y * jax.nn.sigmoid(y)).astype(BF16)


def _conformer(proj, dw, dw_b, ln_g, ln_b):
    pa, na = _halo_specs(CONF_TR, W_C, 0)
    pg, ng = _halo_specs(CONF_TR, W_C, 1)
    row = pl.BlockSpec((1, W_C), lambda i: (0, 0))
    return pl.pallas_call(
        _conformer_kernel,
        grid=(N_TOK // CONF_TR,),
        in_specs=[pl.BlockSpec((CONF_TR, W_C), lambda i: (i, 0)),
                  pl.BlockSpec((CONF_TR, W_C), lambda i: (i, 1)),
                  pa, pg, na, ng,
                  pl.BlockSpec((CONF_K, W_C), lambda i: (0, 0)), row, row, row],
        out_specs=pl.BlockSpec((CONF_TR, W_C), lambda i: (i, 0)),
        out_shape=jax.ShapeDtypeStruct((N_TOK, W_C), BF16),
        compiler_params=_params("arbitrary"),
        name="conformer",
    )(proj, proj, proj, proj, proj, proj, dw, dw_b.reshape(1, W_C), ln_g.reshape(1, W_C), ln_b.reshape(1, W_C))


HY_COL0 = 2 * W_C // W_D


def _hy_short_kernel(*refs, tr):
    (x0_ref, x1_ref, vv_ref, p0_ref, p1_ref, pv_ref, n0_ref, n1_ref, nv_ref,
     w0_ref, w1_ref, wv_ref, b0_ref, b1_ref, bv_ref, x0o_ref, wo_ref) = refs
    row0 = pl.program_id(0) * tr
    seq_len = _seq_len(row0)

    def conv(c_ref, p_ref, n_ref, w_ref, b_ref):
        p = c_ref[...].astype(F32)
        p_prev, p_next = _shift_rows(p, p_ref[HALO - 1:HALO, :].astype(F32), n_ref[0:1, :].astype(F32),
                                     row0, seq_len)
        w = w_ref[...]
        return w[0:1, :] * p_prev + w[1:2, :] * p + w[2:3, :] * p_next + b_ref[...]

    x0o_ref[...] = conv(x0_ref, p0_ref, n0_ref, w0_ref, b0_ref).astype(BF16)
    x1 = conv(x1_ref, p1_ref, n1_ref, w1_ref, b1_ref)
    vv = conv(vv_ref, pv_ref, nv_ref, wv_ref, bv_ref)
    wo_ref[...] = (vv * x1).astype(BF16)


def _hy_short(proj, hy_short, hy_short_b, tr=512):
    main = [pl.BlockSpec((tr, W_D), functools.partial(lambda i, c: (i, c), c=HY_COL0 + c)) for c in range(3)]
    halos = [_halo_specs(tr, W_D, HY_COL0 + c) for c in range(3)]
    wspec = [pl.BlockSpec((SHORT_K, W_D), functools.partial(lambda i, c: (0, c), c=c)) for c in range(3)]
    bspec = [pl.BlockSpec((1, W_D), functools.partial(lambda i, c: (0, c), c=c)) for c in range(3)]
    out = jax.ShapeDtypeStruct((N_TOK, W_D), BF16)
    ospec = pl.BlockSpec((tr, W_D), lambda i: (i, 0))
    b2 = hy_short_b.reshape(1, 3 * W_D)
    return pl.pallas_call(
        functools.partial(_hy_short_kernel, tr=tr),
        grid=(N_TOK // tr,),
        in_specs=main + [h[0] for h in halos] + [h[1] for h in halos] + wspec + bspec,
        out_specs=[ospec, ospec],
        out_shape=[out, out],
        compiler_params=_params("arbitrary"),
        name="hy_short",
    )(*([proj] * 9), hy_short, hy_short, hy_short, b2, b2, b2)


HIGHEST = lax.Precision.HIGHEST
HY_EMB_PAD = 128


def _dft_matrices(L):
    blk = 64
    k = jnp.arange(L, dtype=jnp.int32)[:, None]

    def tables(n):
        m = (k * n[None, :]) % (2 * L)
        ang = m.astype(F32) * (math.pi / L)
        return jnp.cos(ang), jnp.sin(ang)

    c_hi, s_hi = tables(jnp.arange(L // blk, dtype=jnp.int32) * blk)
    c_lo, s_lo = tables(jnp.arange(blk, dtype=jnp.int32))
    c = c_hi[:, :, None] * c_lo[:, None, :] - s_hi[:, :, None] * s_lo[:, None, :]
    s = s_hi[:, :, None] * c_lo[:, None, :] + c_hi[:, :, None] * s_lo[:, None, :]
    return c.reshape(L, L).astype(BF16), s.reshape(L, L).astype(BF16)


def _hy_features(L):
    t = jnp.linspace(0.0, 1.0, L, dtype=F32)[:, None]
    bands = (HY_EMB - 1) // 2
    ang = 2 * math.pi * jnp.arange(L, dtype=F32)[:, None] / L
    freqs = jnp.linspace(1e-4, bands - 1, bands, dtype=F32)[None, :]
    z = jnp.concatenate([t, jnp.cos(freqs * ang), -jnp.sin(freqs * ang)], axis=-1)
    return jnp.pad(z, ((0, 0), (0, HY_EMB_PAD - HY_EMB))), t


def _hy_filter_kernel(z_ref, t_ref, w1_ref, b1_ref, f1_ref, w2_ref, b2_ref, f2_ref, w3_ref, dl_ref,
                      fs_ref, fd_ref, hn_ref, *, tr):
    i = pl.program_id(0)
    hid = jnp.sin(f1_ref[...] * (jnp.dot(z_ref[...], w1_ref[...], precision=HIGHEST,
                                         preferred_element_type=F32) + b1_ref[...]))
    hid = jnp.sin(f2_ref[...] * (jnp.dot(hid, w2_ref[...], precision=HIGHEST,
                                         preferred_element_type=F32) + b2_ref[...]))
    hf = jnp.dot(hid, w3_ref[...], precision=HIGHEST, preferred_element_type=F32)
    decay = jnp.exp(-t_ref[...] * jnp.abs(dl_ref[...]))
    ridx = i * tr + lax.broadcasted_iota(jnp.int32, (tr, 1), 0)
    fwd = hf[:, :W_D] * decay
    bwd = jnp.where(ridx == 0, 0.0, hf[:, W_D:] * decay)
    fsum = fwd + bwd
    fs_ref[...] = fsum.astype(BF16)
    fd_ref[...] = (bwd - fwd).astype(BF16)
    sgn = (1 - 2 * (ridx & 1)).astype(F32)
    part = jnp.sum(fsum * sgn, axis=0, keepdims=True)

    @pl.when(i == 0)
    def _():
        hn_ref[...] = jnp.zeros_like(hn_ref)

    hn_ref[0:1, :] += part


def _hy_filter(L, w1, b1, f1, w2, b2, f2, w3):
    tr = min(L, 512)
    z, t = _hy_features(L)
    max_decay = math.log(HY_TARGET) / HY_FAST_PCT
    min_decay = math.log(HY_TARGET) / HY_SLOW_PCT
    deltas = jnp.linspace(min_decay, max_decay, W_D, dtype=F32)[None, :]
    w1p = jnp.pad(w1, ((0, HY_EMB_PAD - HY_EMB), (0, 0)))
    full = lambda shape: pl.BlockSpec(shape, lambda i: (0, 0))
    return pl.pallas_call(
        functools.partial(_hy_filter_kernel, tr=tr),
        grid=(L // tr,),
        in_specs=[pl.BlockSpec((tr, HY_EMB_PAD), lambda i: (i, 0)),
                  pl.BlockSpec((tr, 1), lambda i: (i, 0)),
                  full((HY_EMB_PAD, HY_HIDDEN)), full((1, HY_HIDDEN)), full((1, HY_HIDDEN)),
                  full((HY_HIDDEN, HY_HIDDEN)), full((1, HY_HIDDEN)), full((1, HY_HIDDEN)),
                  full((HY_HIDDEN, 2 * W_D)), full((1, W_D))],
        out_specs=[pl.BlockSpec((tr, W_D), lambda i: (i, 0)),
                   pl.BlockSpec((tr, W_D), lambda i: (i, 0)),
                   pl.BlockSpec((8, W_D), lambda i: (0, 0))],
        out_shape=[jax.ShapeDtypeStruct((L, W_D), BF16),
                   jax.ShapeDtypeStruct((L, W_D), BF16),
                   jax.ShapeDtypeStruct((8, W_D), F32)],
        compiler_params=_params("arbitrary"),
        name="hy_filter",
    )(z, t, w1p, b1.reshape(1, -1), f1.reshape(1, -1), w2, b2.reshape(1, -1), f2.reshape(1, -1), w3, deltas)


def _hy_spec_kernel(c_ref, s_ref, fs_ref, fd_ref, hc_ref, hs_ref):
    hc_ref[...] = jnp.dot(c_ref[...], fs_ref[...], preferred_element_type=F32)
    hs_ref[...] = jnp.dot(s_ref[...], fd_ref[...], preferred_element_type=F32)


def _hy_spectrum(cmat, smat, fsum, fdif, tk):
    L = cmat.shape[0]
    tile = pl.BlockSpec((tk, L), lambda i: (i, 0))
    full = pl.BlockSpec((L, W_D), lambda i: (0, 0))
    out = pl.BlockSpec((tk, W_D), lambda i: (i, 0))
    return pl.pallas_call(
        _hy_spec_kernel,
        grid=(L // tk,),
        in_specs=[tile, tile, full, full],
        out_specs=[out, out],
        out_shape=[jax.ShapeDtypeStruct((L, W_D), F32)] * 2,
        compiler_params=_params("arbitrary"),
        name="hy_spectrum",
    )(cmat, smat, fsum, fdif)


def _hy_fwd_kernel(c_ref, s_ref, w_ref, hc_ref, hs_ref, hn_ref, yc_ref, ys_ref, yn_ref, *, L, tk):
    kt = pl.program_id(1)
    w = w_ref[...]
    xc = jnp.dot(c_ref[...], w, preferred_element_type=F32)
    xs = jnp.dot(s_ref[...], w, preferred_element_type=F32)
    hc = hc_ref[...]
    hs = hs_ref[...]
    kidx = kt * tk + lax.broadcasted_iota(jnp.int32, (tk, 1), 0)
    om = jnp.where(kidx == 0, 0.5 / L, 1.0 / L)
    yc_ref[...] = (om * (xc * hc + xs * hs)).astype(BF16)
    ys_ref[...] = ((1.0 / L) * (xs * hc - xc * hs)).astype(BF16)

    @pl.when(kt == 0)
    def _():
        n = lax.broadcasted_iota(jnp.int32, (L, 1), 0)
        sgn = (1 - 2 * (n & 1)).astype(F32)
        xn = jnp.sum(w.astype(F32) * sgn, axis=0, keepdims=True)
        yn_ref[...] = jnp.broadcast_to(xn * hn_ref[0:1, :] * (0.5 / L), yn_ref.shape)


def _hy_forward(cmat, smat, w, hc, hs, hn, L, nb, blk0, tk):
    tile = pl.BlockSpec((tk, L), lambda b, k: (k, 0))
    htile = pl.BlockSpec((tk, W_D), lambda b, k: (k, 0))
    out = pl.BlockSpec((tk, W_D), lambda b, k: (b * (L // tk) + k, 0))
    return pl.pallas_call(
        functools.partial(_hy_fwd_kernel, L=L, tk=tk),
        grid=(nb, L // tk),
        in_specs=[tile, tile,
                  pl.BlockSpec((L, W_D), lambda b, k: (blk0 + b, 0)),
                  htile, htile,
                  pl.BlockSpec((8, W_D), lambda b, k: (0, 0))],
        out_specs=[out, out, pl.BlockSpec((None, 8, W_D), lambda b, k: (b, 0, 0))],
        out_shape=[jax.ShapeDtypeStruct((nb * L, W_D), BF16),
                   jax.ShapeDtypeStruct((nb * L, W_D), BF16),
                   jax.ShapeDtypeStruct((nb, 8, W_D), F32)],
        compiler_params=_params("arbitrary", "arbitrary"),
        name="hy_forward",
    )(cmat, smat, w, hc, hs, hn)


def _hy_inv_kernel(c_ref, s_ref, yc_ref, ys_ref, yn_ref, x0_ref, w_ref, bias_ref, z_ref, *, tt):
    ti = pl.program_id(1)
    y = (jnp.dot(c_ref[...], yc_ref[...], preferred_element_type=F32)
         + jnp.dot(s_ref[...], ys_ref[...], preferred_element_type=F32))
    t = ti * tt + lax.broadcasted_iota(jnp.int32, (tt, 1), 0)
    sgn = (1 - 2 * (t & 1)).astype(F32)
    y = y + sgn * yn_ref[0:1, :] + w_ref[...].astype(F32) * bias_ref[...]
    z_ref[...] = (x0_ref[...].astype(F32) * y).astype(BF16)


def _hy_inverse(cmat, smat, yc, ys, yn, x0, w, bias, L, nb, blk0, tt):
    per = L // tt
    tile = pl.BlockSpec((tt, L), lambda b, t: (t, 0))
    seq = pl.BlockSpec((L, W_D), lambda b, t: (b, 0))
    rows_in = pl.BlockSpec((tt, W_D), lambda b, t: ((blk0 + b) * per + t, 0))
    return pl.pallas_call(
        functools.partial(_hy_inv_kernel, tt=tt),
        grid=(nb, per),
        in_specs=[tile, tile, seq, seq,
                  pl.BlockSpec((None, 8, W_D), lambda b, t: (b, 0, 0)),
                  rows_in, rows_in,
                  pl.BlockSpec((1, W_D), lambda b, t: (0, 0))],
        out_specs=pl.BlockSpec((tt, W_D), lambda b, t: (b * per + t, 0)),
        out_shape=jax.ShapeDtypeStruct((nb * L, W_D), BF16),
        compiler_params=_params("arbitrary", "arbitrary"),
        name="hy_inverse",
    )(cmat, smat, yc, ys, yn, x0, w, bias.reshape(1, W_D))


def _hyena_long_conv(x0, w, L, nb, blk0, filt_params, bias):
    tk = min(L, 256)
    cmat, smat = _dft_matrices(L)
    fsum, fdif, hn = _hy_filter(L, *filt_params)
    hc, hs = _hy_spectrum(cmat, smat, fsum, fdif, tk)
    yc, ys, yn = _hy_forward(cmat, smat, w, hc, hs, hn, L, nb, blk0, tk)
    return _hy_inverse(cmat, smat, yc, ys, yn, x0, w, bias, L, nb, blk0, tk)


def kernel(x_prompt, x_sample, cache_k, cache_v, c, c_ctx, ada_w, ada_b, norm_mix, norm_ffn,
           e_w_in, e_conv_a, e_q_norm, e_k_norm, e_rpb, e_w_out,
           o_w_in, o_conf_dw, o_conf_dw_b, o_conf_ln_g, o_conf_ln_b, o_hy_short, o_hy_short_b,
           o_hy_w1, o_hy_b1, o_hy_f1, o_hy_w2, o_hy_b2, o_hy_f2, o_hy_w3, o_hy_bias, o_w_out,
           ffn_in, ffn_conv, ffn_out):
    x_parts = (x_prompt.reshape(N_P, D_MODEL), x_sample.reshape(N_S, D_MODEL))
    cvec = jnp.concatenate([c_ctx[None, :], c, jnp.zeros((N_SEG_PAD - 1 - DEC_BATCH, D_MODEL), F32)], axis=0)
    mod = _adaln(cvec, ada_w, ada_b)
    ks_new, vs_new = [], []
    for layer in range(DEPTH):
        j = layer // 2
        last = layer == DEPTH - 1
        h = _norm_mod_call(x_parts, norm_mix[layer], mod, layer, 0)
        if layer % 2 == 0:
            proj = _proj(h, e_w_in[j])
            ya = _short_gated_conv(proj, e_conv_a[j])
            yb_p, k_p, v_p = _ctx_attention(proj, e_q_norm[j], e_k_norm[j])
            yb_s = _na_attention(proj, cache_k[:, j], cache_v[:, j], e_rpb[j], e_q_norm[j], e_k_norm[j])
            yb = jnp.concatenate([yb_p, yb_s], axis=0)
            ks_new.append(k_p.reshape(BATCH, SEQ, NA_HEADS, NA_HEAD_DIM))
            vs_new.append(v_p.reshape(BATCH, SEQ, NA_HEADS, NA_HEAD_DIM))
            w_out = _cast_bf16(e_w_out, j)
        else:
            proj = _proj(h, o_w_in[j])
            ya = _conformer(proj, o_conf_dw[j], o_conf_dw_b[j], o_conf_ln_g[j], o_conf_ln_b[j])
            x0, w = _hy_short(proj, o_hy_short[j], o_hy_short_b[j])
            fp = (o_hy_w1[j], o_hy_b1[j], o_hy_f1[j], o_hy_w2[j], o_hy_b2[j], o_hy_f2[j], o_hy_w3[j])
            z_p = _hyena_long_conv(x0, w, SEQ, BATCH, 0, fp, o_hy_bias[j])
            z_s = _hyena_long_conv(x0, w, DEC_SEQ, DEC_BATCH, N_P // DEC_SEQ, fp, o_hy_bias[j])
            yb = jnp.concatenate([z_p, z_s], axis=0)
            w_out = _cast_bf16(o_w_out, j)
        x, h_ffn = _out_proj(ya, yb, w_out, x_parts, mod, layer, norm_ffn[layer])
        u = _ffn_in(h_ffn, ffn_in, ffn_conv, layer)
        w_ffn_out = _cast_bf16(ffn_out, layer)
        if last:
            x_parts = (_ffn_out(u, w_ffn_out, x, mod, layer, 0, N_P),
                       _ffn_out(u, w_ffn_out, x, mod, layer, N_P, N_S))
        else:
            x_parts = (_ffn_out(u, w_ffn_out, x, mod, layer),)
    xp = x_parts[0].reshape(BATCH, SEQ, D_MODEL)
    xs = x_parts[1].reshape(DEC_BATCH, DEC_SEQ, D_MODEL)
    return (xp, xs, jnp.stack(ks_new, axis=1), jnp.stack(vs_new, axis=1))
```

```python
import functools
import math

import jax
import jax.numpy as jnp
from jax import lax
from jax.experimental import pallas as pl
from jax.experimental.pallas import tpu as pltpu

D_MODEL = 2048
BATCH = 16
SEQ = 256
DEPTH = 2
DEC_BATCH = 2
DEC_SEQ = 4096
PAST_LEN = 512
GRID_W = 64
W_A = 1024
NA_HEADS = 8
NA_HEAD_DIM = 128
W_B = NA_HEADS * NA_HEAD_DIM
W_C = 1024
W_D = 1024
NA_WIN_R = 8
NA_WIN_C = 16
SHORT_K = 3
CONF_K = 31
D_FF = 5632
HY_EMB = 33
HY_HIDDEN = 64
HY_FAST_PCT = 0.3
HY_SLOW_PCT = 1.5
HY_TARGET = 1e-2
EPS = 1e-6
NEG_INF = -1e30

N_P = BATCH * SEQ
N_S = DEC_BATCH * DEC_SEQ
N_TOK = N_P + N_S
SEG = DEC_SEQ
N_SEG_PAD = 8
N_MOD = 6 * D_MODEL
HALO = 16
VMEM_LIMIT_BYTES = 56 * 1024 * 1024
BF16 = jnp.bfloat16
F32 = jnp.float32


def _params(*sem):
    return pltpu.CompilerParams(dimension_semantics=sem, vmem_limit_bytes=VMEM_LIMIT_BYTES)


def _seq_len(row0):
    return jnp.where(row0 < N_P, SEQ, DEC_SEQ)


def _mod_row(mod_ref, row0):
    return mod_ref[pl.ds(row0 // SEG, 1), :]


def _mod_spec(layer, blk, tn, index_pos):
    per = D_MODEL // tn

    def imap(*idx):
        j = idx[index_pos] if index_pos is not None else 0
        return (layer, 0, blk * per + j)

    return pl.BlockSpec((None, N_SEG_PAD, tn), imap)


def _adaln_kernel(c_ref, w_ref, b_ref, o_ref):
    c = c_ref[...]
    s = (c * jax.nn.sigmoid(c)).astype(BF16)
    o_ref[...] = jnp.dot(s, w_ref[...].astype(BF16), preferred_element_type=F32) + b_ref[...]


def _adaln(cvec, ada_w, ada_b, tn=1024):
    return pl.pallas_call(
        _adaln_kernel,
        grid=(DEPTH, N_MOD // tn),
        in_specs=[pl.BlockSpec((N_SEG_PAD, D_MODEL), lambda l, j: (0, 0)),
                  pl.BlockSpec((None, D_MODEL, tn), lambda l, j: (l, 0, j)),
                  pl.BlockSpec((None, 1, tn), lambda l, j: (l, 0, j))],
        out_specs=pl.BlockSpec((None, N_SEG_PAD, tn), lambda l, j: (l, 0, j)),
        out_shape=jax.ShapeDtypeStruct((DEPTH, N_SEG_PAD, N_MOD), F32),
        compiler_params=_params("arbitrary", "arbitrary"),
        name="adaln",
    )(cvec, ada_w, ada_b.reshape(DEPTH, 1, N_MOD))


def _norm_mod(x, g, shift, scale):
    y = x * lax.rsqrt(jnp.mean(x * x, axis=-1, keepdims=True) + EPS) * g
    return y * (1.0 + scale) + shift


def _stream_specs(x_parts, tr, width, pos):
    if len(x_parts) == 1:
        return [pl.BlockSpec((tr, width), lambda *idx: (idx[pos], 0))]
    n_pt = N_P // tr
    return [pl.BlockSpec((tr, width), lambda *idx: (jnp.minimum(idx[pos], n_pt - 1), 0)),
            pl.BlockSpec((tr, width), lambda *idx: (jnp.maximum(idx[pos] - n_pt, 0), 0))]


def _stream_tile(x_refs, row0):
    if len(x_refs) == 1:
        return x_refs[0][...]
    return jnp.where(row0 < N_P, x_refs[0][...], x_refs[1][...])


def _norm_mod_kernel(*refs, tr, n_parts):
    x_refs, (g_ref, sh_ref, sc_ref, o_ref) = refs[:n_parts], refs[n_parts:]
    row0 = pl.program_id(0) * tr
    x = _stream_tile(x_refs, row0)
    o_ref[...] = _norm_mod(x, g_ref[...], _mod_row(sh_ref, row0), _mod_row(sc_ref, row0)).astype(BF16)


def _norm_mod_call(x_parts, g, mod, layer, blk, tr=512):
    return pl.pallas_call(
        functools.partial(_norm_mod_kernel, tr=tr, n_parts=len(x_parts)),
        grid=(N_TOK // tr,),
        in_specs=_stream_specs(x_parts, tr, D_MODEL, 0) + [
            pl.BlockSpec((1, D_MODEL), lambda i: (0, 0)),
            _mod_spec(layer, blk, D_MODEL, None),
            _mod_spec(layer, blk + 1, D_MODEL, None)],
        out_specs=pl.BlockSpec((tr, D_MODEL), lambda i: (i, 0)),
        out_shape=jax.ShapeDtypeStruct((N_TOK, D_MODEL), BF16),
        compiler_params=_params("arbitrary"),
        name="norm_mod",
    )(*x_parts, g.reshape(1, D_MODEL), mod, mod)


def _cast_kernel(w_ref, o_ref):
    o_ref[...] = w_ref[...].astype(BF16)


def _cast_bf16(w, layer, tr=512):
    _, k, n = w.shape
    return pl.pallas_call(
        _cast_kernel,
        grid=(k // tr,),
        in_specs=[pl.BlockSpec((None, tr, n), lambda i: (layer, i, 0))],
        out_specs=pl.BlockSpec((tr, n), lambda i: (i, 0)),
        out_shape=jax.ShapeDtypeStruct((k, n), BF16),
        compiler_params=_params("arbitrary"),
        name="cast_bf16",
    )(w)


def _proj_kernel(h_ref, w_ref, o_ref, wb_ref):
    @pl.when(pl.program_id(1) == 0)
    def _():
        wb_ref[...] = w_ref[...].astype(BF16)

    o_ref[...] = jnp.dot(h_ref[...], wb_ref[...], preferred_element_type=F32).astype(o_ref.dtype)


def _proj(h, w, tm=1024, tn=1024):
    m, k = h.shape
    n = w.shape[1]
    return pl.pallas_call(
        _proj_kernel,
        grid=(n // tn, m // tm),
        in_specs=[pl.BlockSpec((tm, k), lambda j, i: (i, 0)),
                  pl.BlockSpec((k, tn), lambda j, i: (0, j))],
        out_specs=pl.BlockSpec((tm, tn), lambda j, i: (i, j)),
        out_shape=jax.ShapeDtypeStruct((m, n), BF16),
        scratch_shapes=[pltpu.VMEM((k, tn), BF16)],
        compiler_params=_params("arbitrary", "arbitrary"),
        name="proj",
    )(h, w)


def _shift_rows(p, prev_row, next_row, row0, seq_len):
    n = p.shape[0]
    ridx = lax.broadcasted_iota(jnp.int32, (n, 1), 0)
    pos = (row0 + ridx) & (seq_len - 1)
    p_prev = jnp.where(ridx == 0, prev_row, pltpu.roll(p, 1, 0))
    p_prev = jnp.where(pos == 0, 0.0, p_prev)
    p_next = jnp.where(ridx == n - 1, next_row, pltpu.roll(p, n - 1, 0))
    p_next = jnp.where(pos == seq_len - 1, 0.0, p_next)
    return p_prev, p_next


def _sconv_kernel(ab_ref, ac_ref, ax_ref, pc_ref, px_ref, nc_ref, nx_ref, w_ref, o_ref, *, tr):
    row0 = pl.program_id(0) * tr
    p = ac_ref[...].astype(F32) * ax_ref[...].astype(F32)
    prev_row = pc_ref[HALO - 1:HALO, :].astype(F32) * px_ref[HALO - 1:HALO, :].astype(F32)
    next_row = nc_ref[0:1, :].astype(F32) * nx_ref[0:1, :].astype(F32)
    p_prev, p_next = _shift_rows(p, prev_row, next_row, row0, _seq_len(row0))
    w = w_ref[...]
    conv = w[0:1, :] * p_prev + w[1:2, :] * p + w[2:3, :] * p_next
    o_ref[...] = (ab_ref[...].astype(F32) * conv).astype(BF16)


def _halo_specs(tr, width, col):
    per = tr // HALO
    last = N_TOK // HALO - 1
    prev = pl.BlockSpec((HALO, width), lambda i: (jnp.maximum(i * per - 1, 0), col))
    nxt = pl.BlockSpec((HALO, width), lambda i: (jnp.minimum((i + 1) * per, last), col))
    return prev, nxt


def _short_gated_conv(proj, conv_a, tr=512):
    pc, nc = _halo_specs(tr, W_A, 1)
    px, nx = _halo_specs(tr, W_A, 2)
    return pl.pallas_call(
        functools.partial(_sconv_kernel, tr=tr),
        grid=(N_TOK // tr,),
        in_specs=[pl.BlockSpec((tr, W_A), lambda i: (i, 0)),
                  pl.BlockSpec((tr, W_A), lambda i: (i, 1)),
                  pl.BlockSpec((tr, W_A), lambda i: (i, 2)),
                  pc, px, nc, nx,
                  pl.BlockSpec((SHORT_K, W_A), lambda i: (0, 0))],
        out_specs=pl.BlockSpec((tr, W_A), lambda i: (i, 0)),
        out_shape=jax.ShapeDtypeStruct((N_TOK, W_A), BF16),
        compiler_params=_params("arbitrary"),
        name="short_gated_conv",
    )(proj, proj, proj, proj, proj, proj, proj, conv_a)


ROWS = DEC_SEQ // GRID_W
NA_QROWS = 8
NA_KROWS = 16
NA_NQ = NA_QROWS * GRID_W
NA_NK = NA_KROWS * GRID_W
NA_STEPS = ROWS // NA_QROWS
QKV_COL0 = 3 * W_A // NA_HEAD_DIM
HEAD_SCALE = NA_HEAD_DIM ** -0.5
NT_DIMS = (((1,), (1,)), ((), ()))


def _head_rmsnorm(x, g):
    return x * lax.rsqrt(jnp.mean(x * x, axis=-1, keepdims=True) + EPS) * g


def _na_key_row0(j):
    return jnp.clip(j * NA_QROWS - NA_WIN_R // 2, 0, ROWS - NA_KROWS)


NA_SLOT_LO = NA_QROWS
NA_SLOTS = 2 * NA_KROWS


def _na_geometry(j):
    kr0 = min(max(j * NA_QROWS - NA_WIN_R // 2, 0), ROWS - NA_KROWS)
    per_a = []
    for a in range(NA_QROWS):
        r = j * NA_QROWS + a
        r0 = min(max(r - NA_WIN_R // 2, 0), ROWS - NA_WIN_R)
        per_a.append((kr0 - r + NA_WIN_R - 1 + NA_SLOT_LO, r0 - kr0))
    return kr0, per_a


def _na_bias_slots(rpb):
    n_dr = rpb.shape[1]
    period = 2 * GRID_W - 1
    gap = jnp.full((NA_HEADS, n_dr, period - (2 * NA_WIN_C - 1)), NEG_INF, F32)
    ring = jnp.concatenate([rpb[:, :, NA_WIN_C - 1:].astype(F32), gap, rpb[:, :, :NA_WIN_C - 1].astype(F32)],
                           axis=-1)
    toep = jnp.tile(ring, (1, 1, GRID_W))[:, :, :GRID_W * (period - 1)]
    toep = toep.reshape(NA_HEADS, n_dr, GRID_W, period - 1)[:, :, :, :GRID_W]
    qc = jnp.arange(GRID_W)[:, None]
    kc = jnp.arange(GRID_W)[None, :]
    cs = jnp.clip(qc - NA_WIN_C // 2, 0, GRID_W - NA_WIN_C)
    col = jnp.where((kc >= cs) & (kc < cs + NA_WIN_C), toep, NEG_INF).transpose(0, 2, 1, 3)
    col = jnp.pad(col, ((0, 0), (0, 0), (NA_SLOT_LO, NA_SLOTS + 1 - NA_SLOT_LO - n_dr), (0, 0)),
                  constant_values=NEG_INF)
    even = col[:, :, :NA_SLOTS].reshape(NA_HEADS, GRID_W, NA_SLOTS * GRID_W)
    odd = col[:, :, 1:].reshape(NA_HEADS, GRID_W, NA_SLOTS * GRID_W)
    return even, odd


def _na_build_bias(j_static, even_ref, odd_ref, bias_ref):
    _, per_a = _na_geometry(j_static)
    key_row = lax.broadcasted_iota(jnp.int32, (1, NA_NK), 1) // GRID_W
    for a, (m, lo) in enumerate(per_a):
        src = even_ref if m % 2 == 0 else odd_ref
        start = (m - m % 2) * GRID_W
        slab = src[:, start:start + NA_NK]
        ok = (key_row >= lo) & (key_row < lo + NA_WIN_R)
        bias_ref[a * GRID_W:(a + 1) * GRID_W, :] = jnp.where(ok, slab, NEG_INF)


def _na_kernel(q_ref, k_ref, v_ref, ck_ref, cv_ref, even_ref, odd_ref, qg_ref, kg_ref, o_ref,
               kn_ref, ckb_ref, cvb_ref, bias_ref):
    j = pl.program_id(2)

    @pl.when(j == 0)
    def _():
        kn_ref[...] = _head_rmsnorm(k_ref[...].astype(F32), kg_ref[...]).astype(BF16)
        ckb_ref[...] = ck_ref[...].astype(BF16)
        cvb_ref[...] = cv_ref[...].astype(BF16)

    for j_static in (0, 1, NA_STEPS - 1):
        @pl.when(j == j_static)
        def _(j_static=j_static):
            _na_build_bias(j_static, even_ref, odd_ref, bias_ref)

    q = (_head_rmsnorm(q_ref[...].astype(F32), qg_ref[...]) * HEAD_SCALE).astype(BF16)
    tok0 = pl.multiple_of(_na_key_row0(j) * GRID_W, GRID_W)
    k_win = kn_ref[pl.ds(tok0, NA_NK), :]
    v_win = v_ref[pl.ds(tok0, NA_NK), :]
    s_loc = lax.dot_general(q, k_win, NT_DIMS, preferred_element_type=F32) + bias_ref[...]
    s_ctx = lax.dot_general(q, ckb_ref[...], NT_DIMS, preferred_element_type=F32)
    m = jnp.maximum(jnp.max(s_loc, axis=-1, keepdims=True), jnp.max(s_ctx, axis=-1, keepdims=True))
    p_loc = jnp.exp(s_loc - m)
    p_ctx = jnp.exp(s_ctx - m)
    denom = jnp.sum(p_loc, axis=-1, keepdims=True) + jnp.sum(p_ctx, axis=-1, keepdims=True)
    o = (jnp.dot(p_loc.astype(BF16), v_win, preferred_element_type=F32)
         + jnp.dot(p_ctx.astype(BF16), cvb_ref[...], preferred_element_type=F32))
    o_ref[...] = (o / denom).astype(BF16)


def _na_attention(proj, cache_k, cache_v, rpb, q_norm, k_norm):
    assert NA_STEPS >= 3 and all(_na_geometry(j)[1] == _na_geometry(1)[1] for j in range(1, NA_STEPS - 1))
    even, odd = _na_bias_slots(rpb)
    ck = cache_k.reshape(DEC_BATCH, PAST_LEN, W_B)
    cv = cache_v.reshape(DEC_BATCH, PAST_LEN, W_B)
    hd = NA_HEAD_DIM
    q_blk0 = N_P // NA_NQ
    kv_blk0 = N_P // DEC_SEQ
    slots = pl.BlockSpec((None, GRID_W, NA_SLOTS * GRID_W), lambda b, h, j: (h, 0, 0))
    return pl.pallas_call(
        _na_kernel,
        grid=(DEC_BATCH, NA_HEADS, NA_STEPS),
        in_specs=[
            pl.BlockSpec((NA_NQ, hd), lambda b, h, j: (q_blk0 + b * NA_STEPS + j, QKV_COL0 + h)),
            pl.BlockSpec((DEC_SEQ, hd), lambda b, h, j: (kv_blk0 + b, QKV_COL0 + NA_HEADS + h)),
            pl.BlockSpec((DEC_SEQ, hd), lambda b, h, j: (kv_blk0 + b, QKV_COL0 + 2 * NA_HEADS + h)),
            pl.BlockSpec((None, PAST_LEN, hd), lambda b, h, j: (b, 0, h)),
            pl.BlockSpec((None, PAST_LEN, hd), lambda b, h, j: (b, 0, h)),
            slots, slots,
            pl.BlockSpec((1, hd), lambda b, h, j: (0, 0)),
            pl.BlockSpec((1, hd), lambda b, h, j: (0, 0)),
        ],
        out_specs=pl.BlockSpec((NA_NQ, hd), lambda b, h, j: (b * NA_STEPS + j, h)),
        out_shape=jax.ShapeDtypeStruct((N_S, W_B), BF16),
        scratch_shapes=[pltpu.VMEM((DEC_SEQ, hd), BF16),
                        pltpu.VMEM((PAST_LEN, hd), BF16),
                        pltpu.VMEM((PAST_LEN, hd), BF16),
                        pltpu.VMEM((NA_NQ, NA_NK), F32)],
        compiler_params=_params("arbitrary", "arbitrary", "arbitrary"),
        name="na_attention",
    )(proj, proj, proj, ck, cv, even, odd, q_norm.reshape(1, hd), k_norm.reshape(1, hd))


def _ctx_kernel(q_ref, k_ref, v_ref, qg_ref, kg_ref, o_ref, kn_ref, vo_ref):
    vo_ref[...] = v_ref[...].astype(F32)
    for h in range(NA_HEADS):
        cols = slice(h * NA_HEAD_DIM, (h + 1) * NA_HEAD_DIM)
        q = (_head_rmsnorm(q_ref[:, cols].astype(F32), qg_ref[...]) * HEAD_SCALE).astype(BF16)
        kn = _head_rmsnorm(k_ref[:, cols].astype(F32), kg_ref[...])
        kn_ref[:, cols] = kn
        s = lax.dot_general(q, kn.astype(BF16), NT_DIMS, preferred_element_type=F32)
        p = jnp.exp(s - jnp.max(s, axis=-1, keepdims=True))
        denom = jnp.sum(p, axis=-1, keepdims=True)
        o = jnp.dot(p.astype(BF16), v_ref[:, cols], preferred_element_type=F32)
        o_ref[:, cols] = (o / denom).astype(BF16)


def _ctx_attention(proj, q_norm, k_norm):
    hd = NA_HEAD_DIM
    col0 = 3 * W_A // W_B
    ospec = pl.BlockSpec((SEQ, W_B), lambda b: (b, 0))
    return pl.pallas_call(
        _ctx_kernel,
        grid=(BATCH,),
        in_specs=[
            pl.BlockSpec((SEQ, W_B), lambda b: (b, col0)),
            pl.BlockSpec((SEQ, W_B), lambda b: (b, col0 + 1)),
            pl.BlockSpec((SEQ, W_B), lambda b: (b, col0 + 2)),
            pl.BlockSpec((1, hd), lambda b: (0, 0)),
            pl.BlockSpec((1, hd), lambda b: (0, 0)),
        ],
        out_specs=[ospec, ospec, ospec],
        out_shape=[jax.ShapeDtypeStruct((N_P, W_B), BF16),
                   jax.ShapeDtypeStruct((N_P, W_B), F32),
                   jax.ShapeDtypeStruct((N_P, W_B), F32)],
        compiler_params=_params("arbitrary"),
        name="ctx_attention",
    )(proj, proj, proj, q_norm.reshape(1, hd), k_norm.reshape(1, hd))


def _out_proj_kernel(*refs, tm, ka, n_yb, n_x):
    ya_ref = refs[0]
    yb_refs = refs[1:1 + n_yb]
    w_ref = refs[1 + n_yb]
    x_refs = refs[2 + n_yb:2 + n_yb + n_x]
    gate_ref, g_ref, sh_ref, sc_ref, xo_ref, ho_ref = refs[2 + n_yb + n_x:]
    row0 = pl.program_id(0) * tm
    o = (jnp.dot(ya_ref[...], w_ref[0:ka, :], preferred_element_type=F32)
         + jnp.dot(_stream_tile(yb_refs, row0), w_ref[ka:, :], preferred_element_type=F32))
    x_new = _stream_tile(x_refs, row0) + _mod_row(gate_ref, row0) * o
    xo_ref[...] = x_new
    ho_ref[...] = _norm_mod(x_new, g_ref[...], _mod_row(sh_ref, row0), _mod_row(sc_ref, row0)).astype(BF16)


def _out_proj(ya, yb_parts, w_bf16, x_parts, mod, layer, g_ffn, tm=512):
    ka, kb = ya.shape[1], yb_parts[0].shape[1]
    return pl.pallas_call(
        functools.partial(_out_proj_kernel, tm=tm, ka=ka, n_yb=len(yb_parts), n_x=len(x_parts)),
        grid=(N_TOK // tm,),
        in_specs=[pl.BlockSpec((tm, ka), lambda i: (i, 0))]
        + _stream_specs(yb_parts, tm, kb, 0)
        + [pl.BlockSpec((ka + kb, D_MODEL), lambda i: (0, 0))]
        + _stream_specs(x_parts, tm, D_MODEL, 0) + [
            _mod_spec(layer, 2, D_MODEL, None),
            pl.BlockSpec((1, D_MODEL), lambda i: (0, 0)),
            _mod_spec(layer, 3, D_MODEL, None),
            _mod_spec(layer, 4, D_MODEL, None)],
        out_specs=[pl.BlockSpec((tm, D_MODEL), lambda i: (i, 0)),
                   pl.BlockSpec((tm, D_MODEL), lambda i: (i, 0))],
        out_shape=[jax.ShapeDtypeStruct((N_TOK, D_MODEL), F32),
                   jax.ShapeDtypeStruct((N_TOK, D_MODEL), BF16)],
        compiler_params=_params("arbitrary"),
        name="out_proj",
    )(ya, *yb_parts, w_bf16, *x_parts, mod, g_ffn.reshape(1, D_MODEL), mod, mod)


GELU_C = math.sqrt(2.0 / math.pi)


def _gelu_tanh(x):
    return 0.5 * x * (1.0 + jnp.tanh(GELU_C * (x + 0.044715 * (x * x * x))))


def _ffn_in_kernel(h_ref, hp_ref, hn_ref, wa_ref, wg_ref, cw_ref, o_ref, wab_ref, wgb_ref, *, tm, tn):
    i = pl.program_id(1)

    @pl.when(i == 0)
    def _():
        wab_ref[...] = wa_ref[...].astype(BF16)
        wgb_ref[...] = wg_ref[...].astype(BF16)

    row0 = i * tm
    seq_len = _seq_len(row0)
    h = h_ref[...]
    h_ext = jnp.concatenate([hp_ref[...], h, hn_ref[...]], axis=0)
    n = tm + 2 * HALO
    pos = (row0 + lax.broadcasted_iota(jnp.int32, (tm, 1), 0)) & (seq_len - 1)
    first = pos == 0
    last = pos == seq_len - 1
    a_ext = jnp.dot(h_ext, wab_ref[...], preferred_element_type=F32)
    g = jnp.dot(h, wgb_ref[...], preferred_element_type=F32)
    a_mid = a_ext[HALO:HALO + tm, :]
    a_prev = jnp.where(first, 0.0, pltpu.roll(a_ext, 1, 0)[HALO:HALO + tm, :])
    a_next = jnp.where(last, 0.0, pltpu.roll(a_ext, n - 1, 0)[HALO:HALO + tm, :])
    cw = cw_ref[...]
    conv = cw[0:1, :] * a_prev + cw[1:2, :] * a_mid + cw[2:3, :] * a_next
    o_ref[...] = (_gelu_tanh(conv) * g).astype(BF16)


def _ffn_in(h, w_in, conv, layer, tm=1024, tn=512):
    per = tm // HALO
    last = N_TOK // HALO - 1
    ng = D_FF // tn
    return pl.pallas_call(
        functools.partial(_ffn_in_kernel, tm=tm, tn=tn),
        grid=(ng, N_TOK // tm),
        in_specs=[pl.BlockSpec((tm, D_MODEL), lambda j, i: (i, 0)),
                  pl.BlockSpec((HALO, D_MODEL), lambda j, i: (jnp.maximum(i * per - 1, 0), 0)),
                  pl.BlockSpec((HALO, D_MODEL), lambda j, i: (jnp.minimum((i + 1) * per, last), 0)),
                  pl.BlockSpec((None, D_MODEL, tn), lambda j, i: (layer, 0, j)),
                  pl.BlockSpec((None, D_MODEL, tn), lambda j, i: (layer, 0, ng + j)),
                  pl.BlockSpec((None, SHORT_K, tn), lambda j, i: (layer, 0, j))],
        out_specs=pl.BlockSpec((tm, tn), lambda j, i: (i, j)),
        out_shape=jax.ShapeDtypeStruct((N_TOK, D_FF), BF16),
        scratch_shapes=[pltpu.VMEM((D_MODEL, tn), BF16), pltpu.VMEM((D_MODEL, tn), BF16)],
        compiler_params=_params("arbitrary", "arbitrary"),
        name="ffn_in",
    )(h, h, h, w_in, w_in, conv)


def _ffn_out_kernel(u_ref, w_ref, x_ref, gate_ref, o_ref, wb_ref, *, tm, row_start):
    i = pl.program_id(1)

    @pl.when(i == 0)
    def _():
        wb_ref[...] = w_ref[...].astype(BF16)

    row0 = row_start + i * tm
    o = jnp.dot(u_ref[...], wb_ref[...], preferred_element_type=F32)
    o_ref[...] = x_ref[...] + _mod_row(gate_ref, row0) * o


def _ffn_out(u, w_out, x, mod, layer, row_start=0, n_rows=N_TOK, tm=512, tn=512):
    t0 = row_start // tm
    return pl.pallas_call(
        functools.partial(_ffn_out_kernel, tm=tm, row_start=row_start),
        grid=(D_MODEL // tn, n_rows // tm),
        in_specs=[pl.BlockSpec((tm, D_FF), lambda j, i: (t0 + i, 0)),
                  pl.BlockSpec((None, D_FF, tn), lambda j, i: (layer, 0, j)),
                  pl.BlockSpec((tm, tn), lambda j, i: (t0 + i, j)),
                  _mod_spec(layer, 5, tn, 0)],
        out_specs=pl.BlockSpec((tm, tn), lambda j, i: (i, j)),
        out_shape=jax.ShapeDtypeStruct((n_rows, D_MODEL), F32),
        scratch_shapes=[pltpu.VMEM((D_FF, tn), BF16)],
        compiler_params=_params("arbitrary", "arbitrary"),
        name="ffn_out",
    )(u, w_out, x, mod)


CONF_TR = 256
CONF_PAD = CONF_K // 2


def _conformer_kernel(ca_ref, cg_ref, pa_ref, pg_ref, na_ref, ng_ref, w_ref, b_ref, lg_ref, lb_ref, o_ref):
    tr = CONF_TR
    row0 = pl.program_id(0) * tr
    seq_len = _seq_len(row0)
    pos0 = row0 & (seq_len - 1)

    def glu(a_ref, g_ref):
        return a_ref[...].astype(F32) * jax.nn.sigmoid(g_ref[...].astype(F32))

    prev = jnp.where(pos0 == 0, 0.0, glu(pa_ref, pg_ref))
    nxt = jnp.where(pos0 + tr == seq_len, 0.0, glu(na_ref, ng_ref))
    u_ext = jnp.concatenate([prev, glu(ca_ref, cg_ref), nxt], axis=0)
    w = w_ref[...]
    acc = jnp.zeros((tr, W_C), F32) + b_ref[...]
    n = tr + 2 * HALO
    base = HALO - CONF_PAD
    for r in range(8):
        u_r = pltpu.roll(u_ext, n - (base + r), 0)
        for a in range(len(range(r, CONF_K, 8))):
            k = 8 * a + r
            acc = acc + w[k:k + 1, :] * u_r[8 * a:8 * a + tr, :]
    mu = jnp.mean(acc, axis=-1, keepdims=True)
    d = acc - mu
    var = jnp.mean(d * d, axis=-1, keepdims=True)
    y = d * lax.rsqrt(var + EPS) * lg_ref[...] + lb_ref[...]
    o_ref[...] = (y * jax.nn.sigmoid(y)).astype(BF16)


def _conformer(proj, dw, dw_b, ln_g, ln_b):
    pa, na = _halo_specs(CONF_TR, W_C, 0)
    pg, ng = _halo_specs(CONF_TR, W_C, 1)
    row = pl.BlockSpec((1, W_C), lambda i: (0, 0))
    return pl.pallas_call(
        _conformer_kernel,
        grid=(N_TOK // CONF_TR,),
        in_specs=[pl.BlockSpec((CONF_TR, W_C), lambda i: (i, 0)),
                  pl.BlockSpec((CONF_TR, W_C), lambda i: (i, 1)),
                  pa, pg, na, ng,
                  pl.BlockSpec((CONF_K, W_C), lambda i: (0, 0)), row, row, row],
        out_specs=pl.BlockSpec((CONF_TR, W_C), lambda i: (i, 0)),
        out_shape=jax.ShapeDtypeStruct((N_TOK, W_C), BF16),
        compiler_params=_params("arbitrary"),
        name="conformer",
    )(proj, proj, proj, proj, proj, proj, dw, dw_b.reshape(1, W_C), ln_g.reshape(1, W_C), ln_b.reshape(1, W_C))


HY_COL0 = 2 * W_C // W_D


def _hy_short_kernel(*refs, tr):
    (x0_ref, x1_ref, vv_ref, p0_ref, p1_ref, pv_ref, n0_ref, n1_ref, nv_ref,
     w0_ref, w1_ref, wv_ref, b0_ref, b1_ref, bv_ref, x0o_ref, wo_ref) = refs
    row0 = pl.program_id(0) * tr
    seq_len = _seq_len(row0)

    def conv(c_ref, p_ref, n_ref, w_ref, b_ref):
        p = c_ref[...].astype(F32)
        p_prev, p_next = _shift_rows(p, p_ref[HALO - 1:HALO, :].astype(F32), n_ref[0:1, :].astype(F32),
                                     row0, seq_len)
        w = w_ref[...]
        return w[0:1, :] * p_prev + w[1:2, :] * p + w[2:3, :] * p_next + b_ref[...]

    x0o_ref[...] = conv(x0_ref, p0_ref, n0_ref, w0_ref, b0_ref).astype(BF16)
    x1 = conv(x1_ref, p1_ref, n1_ref, w1_ref, b1_ref)
    vv = conv(vv_ref, pv_ref, nv_ref, wv_ref, bv_ref)
    wo_ref[...] = (vv * x1).astype(BF16)


def _hy_short(proj, hy_short, hy_short_b, tr=512):
    main = [pl.BlockSpec((tr, W_D), functools.partial(lambda i, c: (i, c), c=HY_COL0 + c)) for c in range(3)]
    halos = [_halo_specs(tr, W_D, HY_COL0 + c) for c in range(3)]
    wspec = [pl.BlockSpec((SHORT_K, W_D), functools.partial(lambda i, c: (0, c), c=c)) for c in range(3)]
    bspec = [pl.BlockSpec((1, W_D), functools.partial(lambda i, c: (0, c), c=c)) for c in range(3)]
    out = jax.ShapeDtypeStruct((N_TOK, W_D), BF16)
    ospec = pl.BlockSpec((tr, W_D), lambda i: (i, 0))
    b2 = hy_short_b.reshape(1, 3 * W_D)
    return pl.pallas_call(
        functools.partial(_hy_short_kernel, tr=tr),
        grid=(N_TOK // tr,),
        in_specs=main + [h[0] for h in halos] + [h[1] for h in halos] + wspec + bspec,
        out_specs=[ospec, ospec],
        out_shape=[out, out],
        compiler_params=_params("arbitrary"),
        name="hy_short",
    )(*([proj] * 9), hy_short, hy_short, hy_short, b2, b2, b2)


HIGHEST = lax.Precision.HIGHEST
HY_EMB_PAD = 128


def _dft_gen_kernel(ct_ref, st_ref, cr_ref, sr_ref, c_ref, s_ref):
    t = pl.program_id(0)
    cr = cr_ref[pl.ds(t, 1), :]
    sr = sr_ref[pl.ds(t, 1), :]
    ct = ct_ref[...]
    st = st_ref[...]
    c_ref[...] = (cr * ct - sr * st).astype(BF16)
    s_ref[...] = (sr * ct + cr * st).astype(BF16)


def _dft_matrices(L):
    tr = min(L, 256)
    nt = L // tr
    nt_pad = -(-nt // 8) * 8
    n = jnp.arange(L, dtype=jnp.int32)[None, :]

    def tables(k):
        ang = ((k[:, None] * n) % (2 * L)).astype(F32) * (math.pi / L)
        return jnp.cos(ang), jnp.sin(ang)

    ct, st = tables(jnp.arange(tr, dtype=jnp.int32))
    cr, sr = tables(jnp.arange(nt_pad, dtype=jnp.int32) * tr)
    tile = pl.BlockSpec((tr, L), lambda t: (0, 0))
    rows = pl.BlockSpec((nt_pad, L), lambda t: (0, 0))
    out = pl.BlockSpec((tr, L), lambda t: (t, 0))
    return pl.pallas_call(
        _dft_gen_kernel,
        grid=(nt,),
        in_specs=[tile, tile, rows, rows],
        out_specs=[out, out],
        out_shape=[jax.ShapeDtypeStruct((L, L), BF16)] * 2,
        compiler_params=_params("arbitrary"),
        name="dft_gen",
    )(ct, st, cr, sr)


def _hy_features(L):
    t = jnp.linspace(0.0, 1.0, L, dtype=F32)[:, None]
    bands = (HY_EMB - 1) // 2
    ang = 2 * math.pi * jnp.arange(L, dtype=F32)[:, None] / L
    freqs = jnp.linspace(1e-4, bands - 1, bands, dtype=F32)[None, :]
    z = jnp.concatenate([t, jnp.cos(freqs * ang), -jnp.sin(freqs * ang)], axis=-1)
    return jnp.pad(z, ((0, 0), (0, HY_EMB_PAD - HY_EMB))), t


def _hy_filter_kernel(z_ref, t_ref, w1_ref, b1_ref, f1_ref, w2_ref, b2_ref, f2_ref, w3_ref, dl_ref,
                      fs_ref, fd_ref, hn_ref, *, tr):
    i = pl.program_id(0)
    hid = jnp.sin(f1_ref[...] * (jnp.dot(z_ref[...], w1_ref[...], precision=HIGHEST,
                                         preferred_element_type=F32) + b1_ref[...]))
    hid = jnp.sin(f2_ref[...] * (jnp.dot(hid, w2_ref[...], precision=HIGHEST,
                                         preferred_element_type=F32) + b2_ref[...]))
    hf = jnp.dot(hid, w3_ref[...], precision=HIGHEST, preferred_element_type=F32)
    decay = jnp.exp(-t_ref[...] * jnp.abs(dl_ref[...]))
    ridx = i * tr + lax.broadcasted_iota(jnp.int32, (tr, 1), 0)
    fwd = hf[:, :W_D] * decay
    bwd = jnp.where(ridx == 0, 0.0, hf[:, W_D:] * decay)
    fsum = fwd + bwd
    fs_ref[...] = fsum.astype(BF16)
    fd_ref[...] = (bwd - fwd).astype(BF16)
    sgn = (1 - 2 * (ridx & 1)).astype(F32)
    part = jnp.sum(fsum * sgn, axis=0, keepdims=True)

    @pl.when(i == 0)
    def _():
        hn_ref[...] = jnp.zeros_like(hn_ref)

    hn_ref[0:1, :] += part


def _hy_filter(L, w1, b1, f1, w2, b2, f2, w3):
    tr = min(L, 512)
    z, t = _hy_features(L)
    max_decay = math.log(HY_TARGET) / HY_FAST_PCT
    min_decay = math.log(HY_TARGET) / HY_SLOW_PCT
    deltas = jnp.linspace(min_decay, max_decay, W_D, dtype=F32)[None, :]
    w1p = jnp.pad(w1, ((0, HY_EMB_PAD - HY_EMB), (0, 0)))
    full = lambda shape: pl.BlockSpec(shape, lambda i: (0, 0))
    return pl.pallas_call(
        functools.partial(_hy_filter_kernel, tr=tr),
        grid=(L // tr,),
        in_specs=[pl.BlockSpec((tr, HY_EMB_PAD), lambda i: (i, 0)),
                  pl.BlockSpec((tr, 1), lambda i: (i, 0)),
                  full((HY_EMB_PAD, HY_HIDDEN)), full((1, HY_HIDDEN)), full((1, HY_HIDDEN)),
                  full((HY_HIDDEN, HY_HIDDEN)), full((1, HY_HIDDEN)), full((1, HY_HIDDEN)),
                  full((HY_HIDDEN, 2 * W_D)), full((1, W_D))],
        out_specs=[pl.BlockSpec((tr, W_D), lambda i: (i, 0)),
                   pl.BlockSpec((tr, W_D), lambda i: (i, 0)),
                   pl.BlockSpec((8, W_D), lambda i: (0, 0))],
        out_shape=[jax.ShapeDtypeStruct((L, W_D), BF16),
                   jax.ShapeDtypeStruct((L, W_D), BF16),
                   jax.ShapeDtypeStruct((8, W_D), F32)],
        compiler_params=_params("arbitrary"),
        name="hy_filter",
    )(z, t, w1p, b1.reshape(1, -1), f1.reshape(1, -1), w2, b2.reshape(1, -1), f2.reshape(1, -1), w3, deltas)


def _hy_spec_kernel(c_ref, s_ref, fs_ref, fd_ref, hc_ref, hs_ref):
    hc_ref[...] = jnp.dot(c_ref[...], fs_ref[...], preferred_element_type=F32)
    hs_ref[...] = jnp.dot(s_ref[...], fd_ref[...], preferred_element_type=F32)


def _hy_spectrum(cmat, smat, fsum, fdif, tk):
    L = cmat.shape[0]
    tile = pl.BlockSpec((tk, L), lambda i: (i, 0))
    full = pl.BlockSpec((L, W_D), lambda i: (0, 0))
    out = pl.BlockSpec((tk, W_D), lambda i: (i, 0))
    return pl.pallas_call(
        _hy_spec_kernel,
        grid=(L // tk,),
        in_specs=[tile, tile, full, full],
        out_specs=[out, out],
        out_shape=[jax.ShapeDtypeStruct((L, W_D), F32)] * 2,
        compiler_params=_params("arbitrary"),
        name="hy_spectrum",
    )(cmat, smat, fsum, fdif)


def _hy_fwd_kernel(c_ref, s_ref, w_ref, hc_ref, hs_ref, hn_ref, yc_ref, ys_ref, yn_ref, *, L, tk):
    kt = pl.program_id(1)
    w = w_ref[...]
    xc = jnp.dot(c_ref[...], w, preferred_element_type=F32)
    xs = jnp.dot(s_ref[...], w, preferred_element_type=F32)
    hc = hc_ref[...]
    hs = hs_ref[...]
    kidx = kt * tk + lax.broadcasted_iota(jnp.int32, (tk, 1), 0)
    om = jnp.where(kidx == 0, 0.5 / L, 1.0 / L)
    yc_ref[...] = (om * (xc * hc + xs * hs)).astype(BF16)
    ys_ref[...] = ((1.0 / L) * (xs * hc - xc * hs)).astype(BF16)

    @pl.when(kt == 0)
    def _():
        n = lax.broadcasted_iota(jnp.int32, (L, 1), 0)
        sgn = (1 - 2 * (n & 1)).astype(F32)
        xn = jnp.sum(w.astype(F32) * sgn, axis=0, keepdims=True)
        yn_ref[...] = jnp.broadcast_to(xn * hn_ref[0:1, :] * (0.5 / L), yn_ref.shape)


def _hy_forward(cmat, smat, w, hc, hs, hn, L, nb, blk0, tk):
    tile = pl.BlockSpec((tk, L), lambda b, k: (k, 0))
    htile = pl.BlockSpec((tk, W_D), lambda b, k: (k, 0))
    out = pl.BlockSpec((tk, W_D), lambda b, k: (b * (L // tk) + k, 0))
    return pl.pallas_call(
        functools.partial(_hy_fwd_kernel, L=L, tk=tk),
        grid=(nb, L // tk),
        in_specs=[tile, tile,
                  pl.BlockSpec((L, W_D), lambda b, k: (blk0 + b, 0)),
                  htile, htile,
                  pl.BlockSpec((8, W_D), lambda b, k: (0, 0))],
        out_specs=[out, out, pl.BlockSpec((None, 8, W_D), lambda b, k: (b, 0, 0))],
        out_shape=[jax.ShapeDtypeStruct((nb * L, W_D), BF16),
                   jax.ShapeDtypeStruct((nb * L, W_D), BF16),
                   jax.ShapeDtypeStruct((nb, 8, W_D), F32)],
        compiler_params=_params("arbitrary", "arbitrary"),
        name="hy_forward",
    )(cmat, smat, w, hc, hs, hn)


def _hy_inv_kernel(c_ref, s_ref, yc_ref, ys_ref, yn_ref, x0_ref, w_ref, bias_ref, z_ref, *, tt):
    ti = pl.program_id(1)
    y = (jnp.dot(c_ref[...], yc_ref[...], preferred_element_type=F32)
         + jnp.dot(s_ref[...], ys_ref[...], preferred_element_type=F32))
    t = ti * tt + lax.broadcasted_iota(jnp.int32, (tt, 1), 0)
    sgn = (1 - 2 * (t & 1)).astype(F32)
    y = y + sgn * yn_ref[0:1, :] + w_ref[...].astype(F32) * bias_ref[...]
    z_ref[...] = (x0_ref[...].astype(F32) * y).astype(BF16)


def _hy_inverse(cmat, smat, yc, ys, yn, x0, w, bias, L, nb, blk0, tt):
    per = L // tt
    tile = pl.BlockSpec((tt, L), lambda b, t: (t, 0))
    seq = pl.BlockSpec((L, W_D), lambda b, t: (b, 0))
    rows_in = pl.BlockSpec((tt, W_D), lambda b, t: ((blk0 + b) * per + t, 0))
    return pl.pallas_call(
        functools.partial(_hy_inv_kernel, tt=tt),
        grid=(nb, per),
        in_specs=[tile, tile, seq, seq,
                  pl.BlockSpec((None, 8, W_D), lambda b, t: (b, 0, 0)),
                  rows_in, rows_in,
                  pl.BlockSpec((1, W_D), lambda b, t: (0, 0))],
        out_specs=pl.BlockSpec((tt, W_D), lambda b, t: (b * per + t, 0)),
        out_shape=jax.ShapeDtypeStruct((nb * L, W_D), BF16),
        compiler_params=_params("arbitrary", "arbitrary"),
        name="hy_inverse",
    )(cmat, smat, yc, ys, yn, x0, w, bias.reshape(1, W_D))


def _hyena_long_conv(x0, w, L, nb, blk0, filt_params, bias):
    tk = min(L, 256)
    cmat, smat = _dft_matrices(L)
    fsum, fdif, hn = _hy_filter(L, *filt_params)
    hc, hs = _hy_spectrum(cmat, smat, fsum, fdif, tk)
    yc, ys, yn = _hy_forward(cmat, smat, w, hc, hs, hn, L, nb, blk0, tk)
    return _hy_inverse(cmat, smat, yc, ys, yn, x0, w, bias, L, nb, blk0, tk)


def kernel(x_prompt, x_sample, cache_k, cache_v, c, c_ctx, ada_w, ada_b, norm_mix, norm_ffn,
           e_w_in, e_conv_a, e_q_norm, e_k_norm, e_rpb, e_w_out,
           o_w_in, o_conf_dw, o_conf_dw_b, o_conf_ln_g, o_conf_ln_b, o_hy_short, o_hy_short_b,
           o_hy_w1, o_hy_b1, o_hy_f1, o_hy_w2, o_hy_b2, o_hy_f2, o_hy_w3, o_hy_bias, o_w_out,
           ffn_in, ffn_conv, ffn_out):
    x_parts = (x_prompt.reshape(N_P, D_MODEL), x_sample.reshape(N_S, D_MODEL))
    cvec = jnp.concatenate([c_ctx[None, :], c, jnp.zeros((N_SEG_PAD - 1 - DEC_BATCH, D_MODEL), F32)], axis=0)
    mod = _adaln(cvec, ada_w, ada_b)
    ks_new, vs_new = [], []
    for layer in range(DEPTH):
        j = layer // 2
        last = layer == DEPTH - 1
        h = _norm_mod_call(x_parts, norm_mix[layer], mod, layer, 0)
        if layer % 2 == 0:
            proj = _proj(h, e_w_in[j])
            ya = _short_gated_conv(proj, e_conv_a[j])
            yb_p, k_p, v_p = _ctx_attention(proj, e_q_norm[j], e_k_norm[j])
            yb_s = _na_attention(proj, cache_k[:, j], cache_v[:, j], e_rpb[j], e_q_norm[j], e_k_norm[j])
            yb = (yb_p, yb_s)
            ks_new.append(k_p.reshape(BATCH, SEQ, NA_HEADS, NA_HEAD_DIM))
            vs_new.append(v_p.reshape(BATCH, SEQ, NA_HEADS, NA_HEAD_DIM))
            w_out = _cast_bf16(e_w_out, j)
        else:
            proj = _proj(h, o_w_in[j])
            ya = _conformer(proj, o_conf_dw[j], o_conf_dw_b[j], o_conf_ln_g[j], o_conf_ln_b[j])
            x0, w = _hy_short(proj, o_hy_short[j], o_hy_short_b[j])
            fp = (o_hy_w1[j], o_hy_b1[j], o_hy_f1[j], o_hy_w2[j], o_hy_b2[j], o_hy_f2[j], o_hy_w3[j])
            z_p = _hyena_long_conv(x0, w, SEQ, BATCH, 0, fp, o_hy_bias[j])
            z_s = _hyena_long_conv(x0, w, DEC_SEQ, DEC_BATCH, N_P // DEC_SEQ, fp, o_hy_bias[j])
            yb = (z_p, z_s)
            w_out = _cast_bf16(o_w_out, j)
        x, h_ffn = _out_proj(ya, yb, w_out, x_parts, mod, layer, norm_ffn[layer])
        u = _ffn_in(h_ffn, ffn_in, ffn_conv, layer)
        if last:
            x_parts = (_ffn_out(u, ffn_out, x, mod, layer, 0, N_P),
                       _ffn_out(u, ffn_out, x, mod, layer, N_P, N_S))
        else:
            x_parts = (_ffn_out(u, ffn_out, x, mod, layer),)
    xp = x_parts[0].reshape(BATCH, SEQ, D_MODEL)
    xs = x_parts[1].reshape(DEC_BATCH, DEC_SEQ, D_MODEL)
    return (xp, xs, jnp.stack(ks_new, axis=1), jnp.stack(vs_new, axis=1))
```

```python
import functools
import math

import jax
import jax.numpy as jnp
from jax import lax
from jax.experimental import pallas as pl
from jax.experimental.pallas import tpu as pltpu

D_MODEL = 2048
BATCH = 16
SEQ = 256
DEPTH = 2
DEC_BATCH = 2
DEC_SEQ = 4096
PAST_LEN = 512
GRID_W = 64
W_A = 1024
NA_HEADS = 8
NA_HEAD_DIM = 128
W_B = NA_HEADS * NA_HEAD_DIM
W_C = 1024
W_D = 1024
NA_WIN_R = 8
NA_WIN_C = 16
SHORT_K = 3
CONF_K = 31
D_FF = 5632
HY_EMB = 33
HY_HIDDEN = 64
HY_FAST_PCT = 0.3
HY_SLOW_PCT = 1.5
HY_TARGET = 1e-2
EPS = 1e-6
NEG_INF = -1e30

N_P = BATCH * SEQ
N_S = DEC_BATCH * DEC_SEQ
N_TOK = N_P + N_S
SEG = DEC_SEQ
N_SEG_PAD = 8
N_MOD = 6 * D_MODEL
HALO = 16
VMEM_LIMIT_BYTES = 56 * 1024 * 1024
BF16 = jnp.bfloat16
F32 = jnp.float32


def _params(*sem):
    return pltpu.CompilerParams(dimension_semantics=sem, vmem_limit_bytes=VMEM_LIMIT_BYTES)


def _seq_len(row0):
    return jnp.where(row0 < N_P, SEQ, DEC_SEQ)


def _mod_row(mod_ref, row0):
    return mod_ref[pl.ds(row0 // SEG, 1), :]


def _mod_spec(layer, blk, tn, index_pos):
    per = D_MODEL // tn

    def imap(*idx):
        j = idx[index_pos] if index_pos is not None else 0
        return (layer, 0, blk * per + j)

    return pl.BlockSpec((None, N_SEG_PAD, tn), imap)


def _adaln_kernel(c_ref, w_ref, b_ref, o_ref):
    c = c_ref[...]
    s = (c * jax.nn.sigmoid(c)).astype(BF16)
    o_ref[...] = jnp.dot(s, w_ref[...].astype(BF16), preferred_element_type=F32) + b_ref[...]


def _adaln(cvec, ada_w, ada_b, tn=1024):
    return pl.pallas_call(
        _adaln_kernel,
        grid=(DEPTH, N_MOD // tn),
        in_specs=[pl.BlockSpec((N_SEG_PAD, D_MODEL), lambda l, j: (0, 0)),
                  pl.BlockSpec((None, D_MODEL, tn), lambda l, j: (l, 0, j)),
                  pl.BlockSpec((None, 1, tn), lambda l, j: (l, 0, j))],
        out_specs=pl.BlockSpec((None, N_SEG_PAD, tn), lambda l, j: (l, 0, j)),
        out_shape=jax.ShapeDtypeStruct((DEPTH, N_SEG_PAD, N_MOD), F32),
        compiler_params=_params("arbitrary", "arbitrary"),
        name="adaln",
    )(cvec, ada_w, ada_b.reshape(DEPTH, 1, N_MOD))


def _norm_mod(x, g, shift, scale):
    y = x * lax.rsqrt(jnp.mean(x * x, axis=-1, keepdims=True) + EPS) * g
    return y * (1.0 + scale) + shift


def _stream_specs(x_parts, tr, width, pos):
    if len(x_parts) == 1:
        return [pl.BlockSpec((tr, width), lambda *idx: (idx[pos], 0))]
    n_pt = N_P // tr
    return [pl.BlockSpec((tr, width), lambda *idx: (jnp.minimum(idx[pos], n_pt - 1), 0)),
            pl.BlockSpec((tr, width), lambda *idx: (jnp.maximum(idx[pos] - n_pt, 0), 0))]


def _stream_tile(x_refs, row0):
    if len(x_refs) == 1:
        return x_refs[0][...]
    return jnp.where(row0 < N_P, x_refs[0][...], x_refs[1][...])


def _norm_mod_kernel(*refs, tr, n_parts):
    x_refs, (g_ref, sh_ref, sc_ref, o_ref) = refs[:n_parts], refs[n_parts:]
    row0 = pl.program_id(0) * tr
    x = _stream_tile(x_refs, row0)
    o_ref[...] = _norm_mod(x, g_ref[...], _mod_row(sh_ref, row0), _mod_row(sc_ref, row0)).astype(BF16)


def _norm_mod_call(x_parts, g, mod, layer, blk, tr=512):
    return pl.pallas_call(
        functools.partial(_norm_mod_kernel, tr=tr, n_parts=len(x_parts)),
        grid=(N_TOK // tr,),
        in_specs=_stream_specs(x_parts, tr, D_MODEL, 0) + [
            pl.BlockSpec((1, D_MODEL), lambda i: (0, 0)),
            _mod_spec(layer, blk, D_MODEL, None),
            _mod_spec(layer, blk + 1, D_MODEL, None)],
        out_specs=pl.BlockSpec((tr, D_MODEL), lambda i: (i, 0)),
        out_shape=jax.ShapeDtypeStruct((N_TOK, D_MODEL), BF16),
        compiler_params=_params("arbitrary"),
        name="norm_mod",
    )(*x_parts, g.reshape(1, D_MODEL), mod, mod)


def _cast_kernel(w_ref, o_ref):
    o_ref[...] = w_ref[...].astype(BF16)


def _cast_bf16(w, layer, tr=512):
    _, k, n = w.shape
    return pl.pallas_call(
        _cast_kernel,
        grid=(k // tr,),
        in_specs=[pl.BlockSpec((None, tr, n), lambda i: (layer, i, 0))],
        out_specs=pl.BlockSpec((tr, n), lambda i: (i, 0)),
        out_shape=jax.ShapeDtypeStruct((k, n), BF16),
        compiler_params=_params("arbitrary"),
        name="cast_bf16",
    )(w)


def _proj_kernel(h_ref, w_ref, o_ref, wb_ref):
    @pl.when(pl.program_id(1) == 0)
    def _():
        wb_ref[...] = w_ref[...].astype(BF16)

    o_ref[...] = jnp.dot(h_ref[...], wb_ref[...], preferred_element_type=F32).astype(o_ref.dtype)


def _proj(h, w, tm=1024, tn=1024):
    m, k = h.shape
    n = w.shape[1]
    return pl.pallas_call(
        _proj_kernel,
        grid=(n // tn, m // tm),
        in_specs=[pl.BlockSpec((tm, k), lambda j, i: (i, 0)),
                  pl.BlockSpec((k, tn), lambda j, i: (0, j))],
        out_specs=pl.BlockSpec((tm, tn), lambda j, i: (i, j)),
        out_shape=jax.ShapeDtypeStruct((m, n), BF16),
        scratch_shapes=[pltpu.VMEM((k, tn), BF16)],
        compiler_params=_params("arbitrary", "arbitrary"),
        name="proj",
    )(h, w)


def _shift_rows(p, prev_row, next_row, row0, seq_len):
    n = p.shape[0]
    ridx = lax.broadcasted_iota(jnp.int32, (n, 1), 0)
    pos = (row0 + ridx) & (seq_len - 1)
    p_prev = jnp.where(ridx == 0, prev_row, pltpu.roll(p, 1, 0))
    p_prev = jnp.where(pos == 0, 0.0, p_prev)
    p_next = jnp.where(ridx == n - 1, next_row, pltpu.roll(p, n - 1, 0))
    p_next = jnp.where(pos == seq_len - 1, 0.0, p_next)
    return p_prev, p_next


def _sconv_kernel(ab_ref, ac_ref, ax_ref, pc_ref, px_ref, nc_ref, nx_ref, w_ref, o_ref, *, tr):
    row0 = pl.program_id(0) * tr
    p = ac_ref[...].astype(F32) * ax_ref[...].astype(F32)
    prev_row = pc_ref[HALO - 1:HALO, :].astype(F32) * px_ref[HALO - 1:HALO, :].astype(F32)
    next_row = nc_ref[0:1, :].astype(F32) * nx_ref[0:1, :].astype(F32)
    p_prev, p_next = _shift_rows(p, prev_row, next_row, row0, _seq_len(row0))
    w = w_ref[...]
    conv = w[0:1, :] * p_prev + w[1:2, :] * p + w[2:3, :] * p_next
    o_ref[...] = (ab_ref[...].astype(F32) * conv).astype(BF16)


def _halo_specs(tr, width, col):
    per = tr // HALO
    last = N_TOK // HALO - 1
    prev = pl.BlockSpec((HALO, width), lambda i: (jnp.maximum(i * per - 1, 0), col))
    nxt = pl.BlockSpec((HALO, width), lambda i: (jnp.minimum((i + 1) * per, last), col))
    return prev, nxt


def _short_gated_conv(proj, conv_a, tr=512):
    pc, nc = _halo_specs(tr, W_A, 1)
    px, nx = _halo_specs(tr, W_A, 2)
    return pl.pallas_call(
        functools.partial(_sconv_kernel, tr=tr),
        grid=(N_TOK // tr,),
        in_specs=[pl.BlockSpec((tr, W_A), lambda i: (i, 0)),
                  pl.BlockSpec((tr, W_A), lambda i: (i, 1)),
                  pl.BlockSpec((tr, W_A), lambda i: (i, 2)),
                  pc, px, nc, nx,
                  pl.BlockSpec((SHORT_K, W_A), lambda i: (0, 0))],
        out_specs=pl.BlockSpec((tr, W_A), lambda i: (i, 0)),
        out_shape=jax.ShapeDtypeStruct((N_TOK, W_A), BF16),
        compiler_params=_params("arbitrary"),
        name="short_gated_conv",
    )(proj, proj, proj, proj, proj, proj, proj, conv_a)


ROWS = DEC_SEQ // GRID_W
NA_QROWS = 8
NA_KROWS = 16
NA_NQ = NA_QROWS * GRID_W
NA_NK = NA_KROWS * GRID_W
NA_STEPS = ROWS // NA_QROWS
QKV_COL0 = 3 * W_A // NA_HEAD_DIM
HEAD_SCALE = NA_HEAD_DIM ** -0.5
LOG2E = math.log2(math.e)
NA_CHUNK = 16
NT_DIMS = (((1,), (1,)), ((), ()))


def _head_rmsnorm(x, g):
    return x * lax.rsqrt(jnp.mean(x * x, axis=-1, keepdims=True) + EPS) * g


def _na_key_row0(j):
    return jnp.clip(j * NA_QROWS - NA_WIN_R // 2, 0, ROWS - NA_KROWS)


NA_SLOT_LO = NA_QROWS
NA_SLOTS = 2 * NA_KROWS


def _na_geometry(j):
    kr0 = min(max(j * NA_QROWS - NA_WIN_R // 2, 0), ROWS - NA_KROWS)
    per_a = []
    for a in range(NA_QROWS):
        r = j * NA_QROWS + a
        r0 = min(max(r - NA_WIN_R // 2, 0), ROWS - NA_WIN_R)
        per_a.append((kr0 - r + NA_WIN_R - 1 + NA_SLOT_LO, r0 - kr0))
    return kr0, per_a


def _na_bias_slots(rpb):
    n_dr = rpb.shape[1]
    period = 2 * GRID_W - 1
    gap = jnp.full((NA_HEADS, n_dr, period - (2 * NA_WIN_C - 1)), NEG_INF, F32)
    ring = jnp.concatenate([rpb[:, :, NA_WIN_C - 1:].astype(F32), gap, rpb[:, :, :NA_WIN_C - 1].astype(F32)],
                           axis=-1)
    toep = jnp.tile(ring, (1, 1, GRID_W))[:, :, :GRID_W * (period - 1)]
    toep = toep.reshape(NA_HEADS, n_dr, GRID_W, period - 1)[:, :, :, :GRID_W]
    qc = jnp.arange(GRID_W)[:, None]
    kc = jnp.arange(GRID_W)[None, :]
    cs = jnp.clip(qc - NA_WIN_C // 2, 0, GRID_W - NA_WIN_C)
    col = jnp.where((kc >= cs) & (kc < cs + NA_WIN_C), toep, NEG_INF).transpose(0, 2, 1, 3)
    col = jnp.pad(col, ((0, 0), (0, 0), (NA_SLOT_LO, NA_SLOTS + 1 - NA_SLOT_LO - n_dr), (0, 0)),
                  constant_values=NEG_INF)
    even = col[:, :, :NA_SLOTS].reshape(NA_HEADS, GRID_W, NA_SLOTS * GRID_W)
    odd = col[:, :, 1:].reshape(NA_HEADS, GRID_W, NA_SLOTS * GRID_W)
    return even, odd


def _na_build_bias(j_static, even_ref, odd_ref, bias_ref):
    _, per_a = _na_geometry(j_static)
    key_row = lax.broadcasted_iota(jnp.int32, (1, NA_NK), 1) // GRID_W
    for a, (m, lo) in enumerate(per_a):
        src = even_ref if m % 2 == 0 else odd_ref
        start = (m - m % 2) * GRID_W
        slab = src[:, start:start + NA_NK] * LOG2E
        ok = (key_row >= lo) & (key_row < lo + NA_WIN_R)
        bias_ref[a * GRID_W:(a + 1) * GRID_W, :] = jnp.where(ok, slab, NEG_INF)


def _with_ones_column(v):
    lane = lax.broadcasted_iota(jnp.int32, v.shape, 1)
    return jnp.concatenate([v.astype(BF16), jnp.where(lane == 0, 1.0, 0.0).astype(BF16)], axis=1)


def _na_kernel(q_ref, k_ref, v_ref, ck_ref, cv_ref, even_ref, odd_ref, qg_ref, kg_ref, o_ref,
               kn_ref, vx_ref, ckb_ref, cvx_ref, bias_ref, s_ref, p_ref):
    j = pl.program_id(2)
    hd = NA_HEAD_DIM

    @pl.when(j == 0)
    def _():
        kn_ref[...] = _head_rmsnorm(k_ref[...].astype(F32), kg_ref[...]).astype(BF16)
        vx_ref[...] = _with_ones_column(v_ref[...])
        ckb_ref[...] = ck_ref[...].astype(BF16)
        cvx_ref[...] = _with_ones_column(cv_ref[...])

    for j_static in (0, 1, NA_STEPS - 1):
        @pl.when(j == j_static)
        def _(j_static=j_static):
            _na_build_bias(j_static, even_ref, odd_ref, bias_ref)

    q = (_head_rmsnorm(q_ref[...].astype(F32), qg_ref[...]) * (HEAD_SCALE * LOG2E)).astype(BF16)
    tok0 = pl.multiple_of(_na_key_row0(j) * GRID_W, GRID_W)
    s_ref[:, :NA_NK] = lax.dot_general(q, kn_ref[pl.ds(tok0, NA_NK), :], NT_DIMS, preferred_element_type=F32)
    s_ref[:, NA_NK:] = lax.dot_general(q, ckb_ref[...], NT_DIMS, preferred_element_type=F32)
    for c in range(NA_NQ // NA_CHUNK):
        rows = slice(c * NA_CHUNK, (c + 1) * NA_CHUNK)
        s_loc = s_ref[rows, :NA_NK] + bias_ref[rows, :]
        s_ctx = s_ref[rows, NA_NK:]
        m = functools.reduce(jnp.maximum, [blk[:, t:t + hd] for blk in (s_loc, s_ctx)
                                           for t in range(0, blk.shape[1], hd)])
        m = jnp.max(m, axis=-1, keepdims=True)
        p_ref[rows, :NA_NK] = jnp.exp2(s_loc - m).astype(BF16)
        p_ref[rows, NA_NK:] = jnp.exp2(s_ctx - m).astype(BF16)
    o = (jnp.dot(p_ref[:, :NA_NK], vx_ref[pl.ds(tok0, NA_NK), :], preferred_element_type=F32)
         + jnp.dot(p_ref[:, NA_NK:], cvx_ref[...], preferred_element_type=F32))
    o_ref[...] = (o[:, :hd] / o[:, hd:hd + 1]).astype(BF16)


def _na_attention(proj, cache_k, cache_v, rpb, q_norm, k_norm):
    assert NA_STEPS >= 3 and all(_na_geometry(j)[1] == _na_geometry(1)[1] for j in range(1, NA_STEPS - 1))
    even, odd = _na_bias_slots(rpb)
    ck = cache_k.reshape(DEC_BATCH, PAST_LEN, W_B)
    cv = cache_v.reshape(DEC_BATCH, PAST_LEN, W_B)
    hd = NA_HEAD_DIM
    q_blk0 = N_P // NA_NQ
    kv_blk0 = N_P // DEC_SEQ
    slots = pl.BlockSpec((None, GRID_W, NA_SLOTS * GRID_W), lambda b, h, j: (h, 0, 0))
    return pl.pallas_call(
        _na_kernel,
        grid=(DEC_BATCH, NA_HEADS, NA_STEPS),
        in_specs=[
            pl.BlockSpec((NA_NQ, hd), lambda b, h, j: (q_blk0 + b * NA_STEPS + j, QKV_COL0 + h)),
            pl.BlockSpec((DEC_SEQ, hd), lambda b, h, j: (kv_blk0 + b, QKV_COL0 + NA_HEADS + h)),
            pl.BlockSpec((DEC_SEQ, hd), lambda b, h, j: (kv_blk0 + b, QKV_COL0 + 2 * NA_HEADS + h)),
            pl.BlockSpec((None, PAST_LEN, hd), lambda b, h, j: (b, 0, h)),
            pl.BlockSpec((None, PAST_LEN, hd), lambda b, h, j: (b, 0, h)),
            slots, slots,
            pl.BlockSpec((1, hd), lambda b, h, j: (0, 0)),
            pl.BlockSpec((1, hd), lambda b, h, j: (0, 0)),
        ],
        out_specs=pl.BlockSpec((NA_NQ, hd), lambda b, h, j: (b * NA_STEPS + j, h)),
        out_shape=jax.ShapeDtypeStruct((N_S, W_B), BF16),
        scratch_shapes=[pltpu.VMEM((DEC_SEQ, hd), BF16),
                        pltpu.VMEM((DEC_SEQ, 2 * hd), BF16),
                        pltpu.VMEM((PAST_LEN, hd), BF16),
                        pltpu.VMEM((PAST_LEN, 2 * hd), BF16),
                        pltpu.VMEM((NA_NQ, NA_NK), F32),
                        pltpu.VMEM((NA_NQ, NA_NK + PAST_LEN), F32),
                        pltpu.VMEM((NA_NQ, NA_NK + PAST_LEN), BF16)],
        compiler_params=_params("arbitrary", "arbitrary", "arbitrary"),
        name="na_attention",
    )(proj, proj, proj, ck, cv, even, odd, q_norm.reshape(1, hd), k_norm.reshape(1, hd))


def _ctx_kernel(q_ref, k_ref, v_ref, qg_ref, kg_ref, o_ref, kn_ref, vo_ref):
    vo_ref[...] = v_ref[...].astype(F32)
    for h in range(NA_HEADS):
        cols = slice(h * NA_HEAD_DIM, (h + 1) * NA_HEAD_DIM)
        q = (_head_rmsnorm(q_ref[:, cols].astype(F32), qg_ref[...]) * HEAD_SCALE).astype(BF16)
        kn = _head_rmsnorm(k_ref[:, cols].astype(F32), kg_ref[...])
        kn_ref[:, cols] = kn
        s = lax.dot_general(q, kn.astype(BF16), NT_DIMS, preferred_element_type=F32)
        p = jnp.exp(s - jnp.max(s, axis=-1, keepdims=True))
        denom = jnp.sum(p, axis=-1, keepdims=True)
        o = jnp.dot(p.astype(BF16), v_ref[:, cols], preferred_element_type=F32)
        o_ref[:, cols] = (o / denom).astype(BF16)


def _ctx_attention(proj, q_norm, k_norm):
    hd = NA_HEAD_DIM
    col0 = 3 * W_A // W_B
    ospec = pl.BlockSpec((SEQ, W_B), lambda b: (b, 0))
    return pl.pallas_call(
        _ctx_kernel,
        grid=(BATCH,),
        in_specs=[
            pl.BlockSpec((SEQ, W_B), lambda b: (b, col0)),
            pl.BlockSpec((SEQ, W_B), lambda b: (b, col0 + 1)),
            pl.BlockSpec((SEQ, W_B), lambda b: (b, col0 + 2)),
            pl.BlockSpec((1, hd), lambda b: (0, 0)),
            pl.BlockSpec((1, hd), lambda b: (0, 0)),
        ],
        out_specs=[ospec, ospec, ospec],
        out_shape=[jax.ShapeDtypeStruct((N_P, W_B), BF16),
                   jax.ShapeDtypeStruct((N_P, W_B), F32),
                   jax.ShapeDtypeStruct((N_P, W_B), F32)],
        compiler_params=_params("arbitrary"),
        name="ctx_attention",
    )(proj, proj, proj, q_norm.reshape(1, hd), k_norm.reshape(1, hd))


def _out_proj_kernel(*refs, tm, ka, n_yb, n_x):
    ya_ref = refs[0]
    yb_refs = refs[1:1 + n_yb]
    w_ref = refs[1 + n_yb]
    x_refs = refs[2 + n_yb:2 + n_yb + n_x]
    gate_ref, g_ref, sh_ref, sc_ref, xo_ref, ho_ref = refs[2 + n_yb + n_x:]
    row0 = pl.program_id(0) * tm
    o = (jnp.dot(ya_ref[...], w_ref[0:ka, :], preferred_element_type=F32)
         + jnp.dot(_stream_tile(yb_refs, row0), w_ref[ka:, :], preferred_element_type=F32))
    x_new = _stream_tile(x_refs, row0) + _mod_row(gate_ref, row0) * o
    xo_ref[...] = x_new
    ho_ref[...] = _norm_mod(x_new, g_ref[...], _mod_row(sh_ref, row0), _mod_row(sc_ref, row0)).astype(BF16)


def _out_proj(ya, yb_parts, w_bf16, x_parts, mod, layer, g_ffn, tm=512):
    ka, kb = ya.shape[1], yb_parts[0].shape[1]
    return pl.pallas_call(
        functools.partial(_out_proj_kernel, tm=tm, ka=ka, n_yb=len(yb_parts), n_x=len(x_parts)),
        grid=(N_TOK // tm,),
        in_specs=[pl.BlockSpec((tm, ka), lambda i: (i, 0))]
        + _stream_specs(yb_parts, tm, kb, 0)
        + [pl.BlockSpec((ka + kb, D_MODEL), lambda i: (0, 0))]
        + _stream_specs(x_parts, tm, D_MODEL, 0) + [
            _mod_spec(layer, 2, D_MODEL, None),
            pl.BlockSpec((1, D_MODEL), lambda i: (0, 0)),
            _mod_spec(layer, 3, D_MODEL, None),
            _mod_spec(layer, 4, D_MODEL, None)],
        out_specs=[pl.BlockSpec((tm, D_MODEL), lambda i: (i, 0)),
                   pl.BlockSpec((tm, D_MODEL), lambda i: (i, 0))],
        out_shape=[jax.ShapeDtypeStruct((N_TOK, D_MODEL), F32),
                   jax.ShapeDtypeStruct((N_TOK, D_MODEL), BF16)],
        compiler_params=_params("arbitrary"),
        name="out_proj",
    )(ya, *yb_parts, w_bf16, *x_parts, mod, g_ffn.reshape(1, D_MODEL), mod, mod)


GELU_C = math.sqrt(2.0 / math.pi)


def _gelu_tanh(x):
    return 0.5 * x * (1.0 + jnp.tanh(GELU_C * (x + 0.044715 * (x * x * x))))


def _ffn_in_kernel(h_ref, hp_ref, hn_ref, wa_ref, wg_ref, cw_ref, o_ref, wab_ref, wgb_ref, *, tm, tn):
    i = pl.program_id(1)

    @pl.when(i == 0)
    def _():
        wab_ref[...] = wa_ref[...].astype(BF16)
        wgb_ref[...] = wg_ref[...].astype(BF16)

    row0 = i * tm
    seq_len = _seq_len(row0)
    h = h_ref[...]
    h_ext = jnp.concatenate([hp_ref[...], h, hn_ref[...]], axis=0)
    n = tm + 2 * HALO
    pos = (row0 + lax.broadcasted_iota(jnp.int32, (tm, 1), 0)) & (seq_len - 1)
    first = pos == 0
    last = pos == seq_len - 1
    a_ext = jnp.dot(h_ext, wab_ref[...], preferred_element_type=F32)
    g = jnp.dot(h, wgb_ref[...], preferred_element_type=F32)
    a_mid = a_ext[HALO:HALO + tm, :]
    a_prev = jnp.where(first, 0.0, pltpu.roll(a_ext, 1, 0)[HALO:HALO + tm, :])
    a_next = jnp.where(last, 0.0, pltpu.roll(a_ext, n - 1, 0)[HALO:HALO + tm, :])
    cw = cw_ref[...]
    conv = cw[0:1, :] * a_prev + cw[1:2, :] * a_mid + cw[2:3, :] * a_next
    o_ref[...] = (_gelu_tanh(conv) * g).astype(BF16)


def _ffn_in(h, w_in, conv, layer, tm=1024, tn=512):
    per = tm // HALO
    last = N_TOK // HALO - 1
    ng = D_FF // tn
    return pl.pallas_call(
        functools.partial(_ffn_in_kernel, tm=tm, tn=tn),
        grid=(ng, N_TOK // tm),
        in_specs=[pl.BlockSpec((tm, D_MODEL), lambda j, i: (i, 0)),
                  pl.BlockSpec((HALO, D_MODEL), lambda j, i: (jnp.maximum(i * per - 1, 0), 0)),
                  pl.BlockSpec((HALO, D_MODEL), lambda j, i: (jnp.minimum((i + 1) * per, last), 0)),
                  pl.BlockSpec((None, D_MODEL, tn), lambda j, i: (layer, 0, j)),
                  pl.BlockSpec((None, D_MODEL, tn), lambda j, i: (layer, 0, ng + j)),
                  pl.BlockSpec((None, SHORT_K, tn), lambda j, i: (layer, 0, j))],
        out_specs=pl.BlockSpec((tm, tn), lambda j, i: (i, j)),
        out_shape=jax.ShapeDtypeStruct((N_TOK, D_FF), BF16),
        scratch_shapes=[pltpu.VMEM((D_MODEL, tn), BF16), pltpu.VMEM((D_MODEL, tn), BF16)],
        compiler_params=_params("arbitrary", "arbitrary"),
        name="ffn_in",
    )(h, h, h, w_in, w_in, conv)


def _ffn_out_kernel(u_ref, w_ref, x_ref, gate_ref, o_ref, *, tm, row_start):
    row0 = row_start + pl.program_id(1) * tm
    o = jnp.dot(u_ref[...], w_ref[...], preferred_element_type=F32)
    o_ref[...] = x_ref[...] + _mod_row(gate_ref, row0) * o


def _ffn_out(u, w_bf16, x, mod, layer, row_start=0, n_rows=N_TOK, tm=512, tn=1024):
    t0 = row_start // tm
    return pl.pallas_call(
        functools.partial(_ffn_out_kernel, tm=tm, row_start=row_start),
        grid=(D_MODEL // tn, n_rows // tm),
        in_specs=[pl.BlockSpec((tm, D_FF), lambda j, i: (t0 + i, 0)),
                  pl.BlockSpec((D_FF, tn), lambda j, i: (0, j)),
                  pl.BlockSpec((tm, tn), lambda j, i: (t0 + i, j)),
                  _mod_spec(layer, 5, tn, 0)],
        out_specs=pl.BlockSpec((tm, tn), lambda j, i: (i, j)),
        out_shape=jax.ShapeDtypeStruct((n_rows, D_MODEL), F32),
        compiler_params=_params("arbitrary", "arbitrary"),
        name="ffn_out",
    )(u, w_bf16, x, mod)


CONF_TR = 256
CONF_PAD = CONF_K // 2


def _conformer_kernel(ca_ref, cg_ref, pa_ref, pg_ref, na_ref, ng_ref, w_ref, b_ref, lg_ref, lb_ref, o_ref):
    tr = CONF_TR
    row0 = pl.program_id(0) * tr
    seq_len = _seq_len(row0)
    pos0 = row0 & (seq_len - 1)

    def glu(a_ref, g_ref):
        return a_ref[...].astype(F32) * jax.nn.sigmoid(g_ref[...].astype(F32))

    prev = jnp.where(pos0 == 0, 0.0, glu(pa_ref, pg_ref))
    nxt = jnp.where(pos0 + tr == seq_len, 0.0, glu(na_ref, ng_ref))
    u_ext = jnp.concatenate([prev, glu(ca_ref, cg_ref), nxt], axis=0)
    w = w_ref[...]
    acc = jnp.zeros((tr, W_C), F32) + b_ref[...]
    n = tr + 2 * HALO
    base = HALO - CONF_PAD
    for r in range(8):
        u_r = pltpu.roll(u_ext, n - (base + r), 0)
        for a in range(len(range(r, CONF_K, 8))):
            k = 8 * a + r
            acc = acc + w[k:k + 1, :] * u_r[8 * a:8 * a + tr, :]
    mu = jnp.mean(acc, axis=-1, keepdims=True)
    d = acc - mu
    var = jnp.mean(d * d, axis=-1, keepdims=True)
    y = d * lax.rsqrt(var + EPS) * lg_ref[...] + lb_ref[...]
    o_ref[...] = (y * jax.nn.sigmoid(y)).astype(BF16)


def _conformer(proj, dw, dw_b, ln_g, ln_b):
    pa, na = _halo_specs(CONF_TR, W_C, 0)
    pg, ng = _halo_specs(CONF_TR, W_C, 1)
    row = pl.BlockSpec((1, W_C), lambda i: (0, 0))
    return pl.pallas_call(
        _conformer_kernel,
        grid=(N_TOK // CONF_TR,),
        in_specs=[pl.BlockSpec((CONF_TR, W_C), lambda i: (i, 0)),
                  pl.BlockSpec((CONF_TR, W_C), lambda i: (i, 1)),
                  pa, pg, na, ng,
                  pl.BlockSpec((CONF_K, W_C), lambda i: (0, 0)), row, row, row],
        out_specs=pl.BlockSpec((CONF_TR, W_C), lambda i: (i, 0)),
        out_shape=jax.ShapeDtypeStruct((N_TOK, W_C), BF16),
        compiler_params=_params("arbitrary"),
        name="conformer",
    )(proj, proj, proj, proj, proj, proj, dw, dw_b.reshape(1, W_C), ln_g.reshape(1, W_C), ln_b.reshape(1, W_C))


HY_COL0 = 2 * W_C // W_D


def _hy_short_kernel(*refs, tr):
    (x0_ref, x1_ref, vv_ref, p0_ref, p1_ref, pv_ref, n0_ref, n1_ref, nv_ref,
     w0_ref, w1_ref, wv_ref, b0_ref, b1_ref, bv_ref, x0o_ref, wo_ref) = refs
    row0 = pl.program_id(0) * tr
    seq_len = _seq_len(row0)

    def conv(c_ref, p_ref, n_ref, w_ref, b_ref):
        p = c_ref[...].astype(F32)
        p_prev, p_next = _shift_rows(p, p_ref[HALO - 1:HALO, :].astype(F32), n_ref[0:1, :].astype(F32),
                                     row0, seq_len)
        w = w_ref[...]
        return w[0:1, :] * p_prev + w[1:2, :] * p + w[2:3, :] * p_next + b_ref[...]

    x0o_ref[...] = conv(x0_ref, p0_ref, n0_ref, w0_ref, b0_ref).astype(BF16)
    x1 = conv(x1_ref, p1_ref, n1_ref, w1_ref, b1_ref)
    vv = conv(vv_ref, pv_ref, nv_ref, wv_ref, bv_ref)
    wo_ref[...] = (vv * x1).astype(BF16)


def _hy_short(proj, hy_short, hy_short_b, tr=512):
    main = [pl.BlockSpec((tr, W_D), functools.partial(lambda i, c: (i, c), c=HY_COL0 + c)) for c in range(3)]
    halos = [_halo_specs(tr, W_D, HY_COL0 + c) for c in range(3)]
    wspec = [pl.BlockSpec((SHORT_K, W_D), functools.partial(lambda i, c: (0, c), c=c)) for c in range(3)]
    bspec = [pl.BlockSpec((1, W_D), functools.partial(lambda i, c: (0, c), c=c)) for c in range(3)]
    out = jax.ShapeDtypeStruct((N_TOK, W_D), BF16)
    ospec = pl.BlockSpec((tr, W_D), lambda i: (i, 0))
    b2 = hy_short_b.reshape(1, 3 * W_D)
    return pl.pallas_call(
        functools.partial(_hy_short_kernel, tr=tr),
        grid=(N_TOK // tr,),
        in_specs=main + [h[0] for h in halos] + [h[1] for h in halos] + wspec + bspec,
        out_specs=[ospec, ospec],
        out_shape=[out, out],
        compiler_params=_params("arbitrary"),
        name="hy_short",
    )(*([proj] * 9), hy_short, hy_short, hy_short, b2, b2, b2)


HIGHEST = lax.Precision.HIGHEST
HY_EMB_PAD = 128


def _dft_gen_kernel(ct_ref, st_ref, cr_ref, sr_ref, c_ref, s_ref):
    t = pl.program_id(0)
    cr = cr_ref[pl.ds(t, 1), :]
    sr = sr_ref[pl.ds(t, 1), :]
    ct = ct_ref[...]
    st = st_ref[...]
    c_ref[...] = (cr * ct - sr * st).astype(BF16)
    s_ref[...] = (sr * ct + cr * st).astype(BF16)


def _dft_matrices(L):
    tr = min(L, 256)
    nt = L // tr
    nt_pad = -(-nt // 8) * 8
    n = jnp.arange(L, dtype=jnp.int32)[None, :]

    def tables(k):
        ang = ((k[:, None] * n) % (2 * L)).astype(F32) * (math.pi / L)
        return jnp.cos(ang), jnp.sin(ang)

    ct, st = tables(jnp.arange(tr, dtype=jnp.int32))
    cr, sr = tables(jnp.arange(nt_pad, dtype=jnp.int32) * tr)
    tile = pl.BlockSpec((tr, L), lambda t: (0, 0))
    rows = pl.BlockSpec((nt_pad, L), lambda t: (0, 0))
    out = pl.BlockSpec((tr, L), lambda t: (t, 0))
    return pl.pallas_call(
        _dft_gen_kernel,
        grid=(nt,),
        in_specs=[tile, tile, rows, rows],
        out_specs=[out, out],
        out_shape=[jax.ShapeDtypeStruct((L, L), BF16)] * 2,
        compiler_params=_params("arbitrary"),
        name="dft_gen",
    )(ct, st, cr, sr)


def _hy_features(L):
    t = jnp.linspace(0.0, 1.0, L, dtype=F32)[:, None]
    bands = (HY_EMB - 1) // 2
    ang = 2 * math.pi * jnp.arange(L, dtype=F32)[:, None] / L
    freqs = jnp.linspace(1e-4, bands - 1, bands, dtype=F32)[None, :]
    z = jnp.concatenate([t, jnp.cos(freqs * ang), -jnp.sin(freqs * ang)], axis=-1)
    return jnp.pad(z, ((0, 0), (0, HY_EMB_PAD - HY_EMB))), t


def _hy_filter_kernel(z_ref, t_ref, w1_ref, b1_ref, f1_ref, w2_ref, b2_ref, f2_ref, w3_ref, dl_ref,
                      fs_ref, fd_ref, hn_ref, *, tr):
    i = pl.program_id(0)
    hid = jnp.sin(f1_ref[...] * (jnp.dot(z_ref[...], w1_ref[...], precision=HIGHEST,
                                         preferred_element_type=F32) + b1_ref[...]))
    hid = jnp.sin(f2_ref[...] * (jnp.dot(hid, w2_ref[...], precision=HIGHEST,
                                         preferred_element_type=F32) + b2_ref[...]))
    hf = jnp.dot(hid.astype(BF16), w3_ref[...].astype(BF16), preferred_element_type=F32)
    decay = jnp.exp(-t_ref[...] * jnp.abs(dl_ref[...]))
    ridx = i * tr + lax.broadcasted_iota(jnp.int32, (tr, 1), 0)
    fwd = hf[:, :W_D] * decay
    bwd = jnp.where(ridx == 0, 0.0, hf[:, W_D:] * decay)
    fsum = fwd + bwd
    fs_ref[...] = fsum.astype(BF16)
    fd_ref[...] = (bwd - fwd).astype(BF16)
    sgn = (1 - 2 * (ridx & 1)).astype(F32)
    part = jnp.sum(fsum * sgn, axis=0, keepdims=True)

    @pl.when(i == 0)
    def _():
        hn_ref[...] = jnp.zeros_like(hn_ref)

    hn_ref[0:1, :] += part


def _hy_filter(L, w1, b1, f1, w2, b2, f2, w3):
    tr = min(L, 512)
    z, t = _hy_features(L)
    max_decay = math.log(HY_TARGET) / HY_FAST_PCT
    min_decay = math.log(HY_TARGET) / HY_SLOW_PCT
    deltas = jnp.linspace(min_decay, max_decay, W_D, dtype=F32)[None, :]
    w1p = jnp.pad(w1, ((0, HY_EMB_PAD - HY_EMB), (0, 0)))
    full = lambda shape: pl.BlockSpec(shape, lambda i: (0, 0))
    return pl.pallas_call(
        functools.partial(_hy_filter_kernel, tr=tr),
        grid=(L // tr,),
        in_specs=[pl.BlockSpec((tr, HY_EMB_PAD), lambda i: (i, 0)),
                  pl.BlockSpec((tr, 1), lambda i: (i, 0)),
                  full((HY_EMB_PAD, HY_HIDDEN)), full((1, HY_HIDDEN)), full((1, HY_HIDDEN)),
                  full((HY_HIDDEN, HY_HIDDEN)), full((1, HY_HIDDEN)), full((1, HY_HIDDEN)),
                  full((HY_HIDDEN, 2 * W_D)), full((1, W_D))],
        out_specs=[pl.BlockSpec((tr, W_D), lambda i: (i, 0)),
                   pl.BlockSpec((tr, W_D), lambda i: (i, 0)),
                   pl.BlockSpec((8, W_D), lambda i: (0, 0))],
        out_shape=[jax.ShapeDtypeStruct((L, W_D), BF16),
                   jax.ShapeDtypeStruct((L, W_D), BF16),
                   jax.ShapeDtypeStruct((8, W_D), F32)],
        compiler_params=_params("arbitrary"),
        name="hy_filter",
    )(z, t, w1p, b1.reshape(1, -1), f1.reshape(1, -1), w2, b2.reshape(1, -1), f2.reshape(1, -1), w3, deltas)


def _hy_spec_kernel(c_ref, s_ref, fs_ref, fd_ref, hc_ref, hs_ref):
    hc_ref[...] = jnp.dot(c_ref[...], fs_ref[...], preferred_element_type=F32)
    hs_ref[...] = jnp.dot(s_ref[...], fd_ref[...], preferred_element_type=F32)


def _hy_spectrum(cmat, smat, fsum, fdif, tk):
    L = cmat.shape[0]
    tile = pl.BlockSpec((tk, L), lambda i: (i, 0))
    full = pl.BlockSpec((L, W_D), lambda i: (0, 0))
    out = pl.BlockSpec((tk, W_D), lambda i: (i, 0))
    return pl.pallas_call(
        _hy_spec_kernel,
        grid=(L // tk,),
        in_specs=[tile, tile, full, full],
        out_specs=[out, out],
        out_shape=[jax.ShapeDtypeStruct((L, W_D), F32)] * 2,
        compiler_params=_params("arbitrary"),
        name="hy_spectrum",
    )(cmat, smat, fsum, fdif)


def _hy_fwd_kernel(c_ref, s_ref, w_ref, hc_ref, hs_ref, hn_ref, yc_ref, ys_ref, yn_ref, *, L, tk):
    kt = pl.program_id(1)
    w = w_ref[...]
    xc = jnp.dot(c_ref[...], w, preferred_element_type=F32)
    xs = jnp.dot(s_ref[...], w, preferred_element_type=F32)
    hc = hc_ref[...]
    hs = hs_ref[...]
    kidx = kt * tk + lax.broadcasted_iota(jnp.int32, (tk, 1), 0)
    om = jnp.where(kidx == 0, 0.5 / L, 1.0 / L)
    yc_ref[...] = (om * (xc * hc + xs * hs)).astype(BF16)
    ys_ref[...] = ((1.0 / L) * (xs * hc - xc * hs)).astype(BF16)

    @pl.when(kt == 0)
    def _():
        n = lax.broadcasted_iota(jnp.int32, (L, 1), 0)
        sgn = (1 - 2 * (n & 1)).astype(F32)
        xn = jnp.sum(w.astype(F32) * sgn, axis=0, keepdims=True)
        yn_ref[...] = jnp.broadcast_to(xn * hn_ref[0:1, :] * (0.5 / L), yn_ref.shape)


def _hy_forward(cmat, smat, w, hc, hs, hn, L, nb, blk0, tk):
    tile = pl.BlockSpec((tk, L), lambda b, k: (k, 0))
    htile = pl.BlockSpec((tk, W_D), lambda b, k: (k, 0))
    out = pl.BlockSpec((tk, W_D), lambda b, k: (b * (L // tk) + k, 0))
    return pl.pallas_call(
        functools.partial(_hy_fwd_kernel, L=L, tk=tk),
        grid=(nb, L // tk),
        in_specs=[tile, tile,
                  pl.BlockSpec((L, W_D), lambda b, k: (blk0 + b, 0)),
                  htile, htile,
                  pl.BlockSpec((8, W_D), lambda b, k: (0, 0))],
        out_specs=[out, out, pl.BlockSpec((None, 8, W_D), lambda b, k: (b, 0, 0))],
        out_shape=[jax.ShapeDtypeStruct((nb * L, W_D), BF16),
                   jax.ShapeDtypeStruct((nb * L, W_D), BF16),
                   jax.ShapeDtypeStruct((nb, 8, W_D), F32)],
        compiler_params=_params("arbitrary", "arbitrary"),
        name="hy_forward",
    )(cmat, smat, w, hc, hs, hn)


def _hy_inv_kernel(c_ref, s_ref, yc_ref, ys_ref, yn_ref, x0_ref, w_ref, bias_ref, z_ref, *, tt):
    ti = pl.program_id(1)
    y = (jnp.dot(c_ref[...], yc_ref[...], preferred_element_type=F32)
         + jnp.dot(s_ref[...], ys_ref[...], preferred_element_type=F32))
    t = ti * tt + lax.broadcasted_iota(jnp.int32, (tt, 1), 0)
    sgn = (1 - 2 * (t & 1)).astype(F32)
    y = y + sgn * yn_ref[0:1, :] + w_ref[...].astype(F32) * bias_ref[...]
    z_ref[...] = (x0_ref[...].astype(F32) * y).astype(BF16)


def _hy_inverse(cmat, smat, yc, ys, yn, x0, w, bias, L, nb, blk0, tt):
    per = L // tt
    tile = pl.BlockSpec((tt, L), lambda b, t: (t, 0))
    seq = pl.BlockSpec((L, W_D), lambda b, t: (b, 0))
    rows_in = pl.BlockSpec((tt, W_D), lambda b, t: ((blk0 + b) * per + t, 0))
    return pl.pallas_call(
        functools.partial(_hy_inv_kernel, tt=tt),
        grid=(nb, per),
        in_specs=[tile, tile, seq, seq,
                  pl.BlockSpec((None, 8, W_D), lambda b, t: (b, 0, 0)),
                  rows_in, rows_in,
                  pl.BlockSpec((1, W_D), lambda b, t: (0, 0))],
        out_specs=pl.BlockSpec((tt, W_D), lambda b, t: (b * per + t, 0)),
        out_shape=jax.ShapeDtypeStruct((nb * L, W_D), BF16),
        compiler_params=_params("arbitrary", "arbitrary"),
        name="hy_inverse",
    )(cmat, smat, yc, ys, yn, x0, w, bias.reshape(1, W_D))


def _hyena_long_conv(x0, w, L, nb, blk0, filt_params, bias):
    tk = min(L, 256)
    cmat, smat = _dft_matrices(L)
    fsum, fdif, hn = _hy_filter(L, *filt_params)
    hc, hs = _hy_spectrum(cmat, smat, fsum, fdif, tk)
    yc, ys, yn = _hy_forward(cmat, smat, w, hc, hs, hn, L, nb, blk0, tk)
    return _hy_inverse(cmat, smat, yc, ys, yn, x0, w, bias, L, nb, blk0, tk)


def kernel(x_prompt, x_sample, cache_k, cache_v, c, c_ctx, ada_w, ada_b, norm_mix, norm_ffn,
           e_w_in, e_conv_a, e_q_norm, e_k_norm, e_rpb, e_w_out,
           o_w_in, o_conf_dw, o_conf_dw_b, o_conf_ln_g, o_conf_ln_b, o_hy_short, o_hy_short_b,
           o_hy_w1, o_hy_b1, o_hy_f1, o_hy_w2, o_hy_b2, o_hy_f2, o_hy_w3, o_hy_bias, o_w_out,
           ffn_in, ffn_conv, ffn_out):
    x_parts = (x_prompt.reshape(N_P, D_MODEL), x_sample.reshape(N_S, D_MODEL))
    cvec = jnp.concatenate([c_ctx[None, :], c, jnp.zeros((N_SEG_PAD - 1 - DEC_BATCH, D_MODEL), F32)], axis=0)
    mod = _adaln(cvec, ada_w, ada_b)
    ks_new, vs_new = [], []
    for layer in range(DEPTH):
        j = layer // 2
        last = layer == DEPTH - 1
        h = _norm_mod_call(x_parts, norm_mix[layer], mod, layer, 0)
        if layer % 2 == 0:
            proj = _proj(h, e_w_in[j])
            ya = _short_gated_conv(proj, e_conv_a[j])
            yb_p, k_p, v_p = _ctx_attention(proj, e_q_norm[j], e_k_norm[j])
            yb_s = _na_attention(proj, cache_k[:, j], cache_v[:, j], e_rpb[j], e_q_norm[j], e_k_norm[j])
            yb = (yb_p, yb_s)
            ks_new.append(k_p.reshape(BATCH, SEQ, NA_HEADS, NA_HEAD_DIM))
            vs_new.append(v_p.reshape(BATCH, SEQ, NA_HEADS, NA_HEAD_DIM))
            w_out = _cast_bf16(e_w_out, j)
        else:
            proj = _proj(h, o_w_in[j])
            ya = _conformer(proj, o_conf_dw[j], o_conf_dw_b[j], o_conf_ln_g[j], o_conf_ln_b[j])
            x0, w = _hy_short(proj, o_hy_short[j], o_hy_short_b[j])
            fp = (o_hy_w1[j], o_hy_b1[j], o_hy_f1[j], o_hy_w2[j], o_hy_b2[j], o_hy_f2[j], o_hy_w3[j])
            z_p = _hyena_long_conv(x0, w, SEQ, BATCH, 0, fp, o_hy_bias[j])
            z_s = _hyena_long_conv(x0, w, DEC_SEQ, DEC_BATCH, N_P // DEC_SEQ, fp, o_hy_bias[j])
            yb = (z_p, z_s)
            w_out = _cast_bf16(o_w_out, j)
        x, h_ffn = _out_proj(ya, yb, w_out, x_parts, mod, layer, norm_ffn[layer])
        u = _ffn_in(h_ffn, ffn_in, ffn_conv, layer)
        w_ffn_out = _cast_bf16(ffn_out, layer)
        if last:
            x_parts = (_ffn_out(u, w_ffn_out, x, mod, layer, 0, N_P),
                       _ffn_out(u, w_ffn_out, x, mod, layer, N_P, N_S))
        else:
            x_parts = (_ffn_out(u, w_ffn_out, x, mod, layer),)
    xp = x_parts[0].reshape(BATCH, SEQ, D_MODEL)
    xs = x_parts[1].reshape(DEC_BATCH, DEC_SEQ, D_MODEL)
    return (xp, xs, jnp.stack(ks_new, axis=1), jnp.stack(vs_new, axis=1))
```

```python
import functools
import math

import jax
import jax.numpy as jnp
from jax import lax
from jax.experimental import pallas as pl
from jax.experimental.pallas import tpu as pltpu

D_MODEL = 2048
BATCH = 16
SEQ = 256
DEPTH = 2
DEC_BATCH = 2
DEC_SEQ = 4096
PAST_LEN = 512
GRID_W = 64
W_A = 1024
NA_HEADS = 8
NA_HEAD_DIM = 128
W_B = NA_HEADS * NA_HEAD_DIM
W_C = 1024
W_D = 1024
NA_WIN_R = 8
NA_WIN_C = 16
SHORT_K = 3
CONF_K = 31
D_FF = 5632
HY_EMB = 33
HY_HIDDEN = 64
HY_FAST_PCT = 0.3
HY_SLOW_PCT = 1.5
HY_TARGET = 1e-2
EPS = 1e-6
NEG_INF = -1e30

N_P = BATCH * SEQ
N_S = DEC_BATCH * DEC_SEQ
N_TOK = N_P + N_S
SEG = DEC_SEQ
N_SEG_PAD = 8
N_MOD = 6 * D_MODEL
HALO = 16
VMEM_LIMIT_BYTES = 60 * 1024 * 1024
BF16 = jnp.bfloat16
F32 = jnp.float32


def _params(*sem):
    return pltpu.CompilerParams(dimension_semantics=sem, vmem_limit_bytes=VMEM_LIMIT_BYTES)


def _seq_len(row0):
    return jnp.where(row0 < N_P, SEQ, DEC_SEQ)


def _mod_row(mod_ref, row0):
    return mod_ref[pl.ds(row0 // SEG, 1), :]


def _mod_spec(layer, blk, tn, index_pos):
    per = D_MODEL // tn

    def imap(*idx):
        j = idx[index_pos] if index_pos is not None else 0
        return (layer, 0, blk * per + j)

    return pl.BlockSpec((None, N_SEG_PAD, tn), imap)


def _adaln_kernel(c_ref, w_ref, b_ref, o_ref):
    c = c_ref[...]
    s = (c * jax.nn.sigmoid(c)).astype(BF16)
    o_ref[...] = jnp.dot(s, w_ref[...].astype(BF16), preferred_element_type=F32) + b_ref[...]


def _adaln(cvec, ada_w, ada_b, tn=1024):
    return pl.pallas_call(
        _adaln_kernel,
        grid=(DEPTH, N_MOD // tn),
        in_specs=[pl.BlockSpec((N_SEG_PAD, D_MODEL), lambda l, j: (0, 0)),
                  pl.BlockSpec((None, D_MODEL, tn), lambda l, j: (l, 0, j)),
                  pl.BlockSpec((None, 1, tn), lambda l, j: (l, 0, j))],
        out_specs=pl.BlockSpec((None, N_SEG_PAD, tn), lambda l, j: (l, 0, j)),
        out_shape=jax.ShapeDtypeStruct((DEPTH, N_SEG_PAD, N_MOD), F32),
        compiler_params=_params("arbitrary", "arbitrary"),
        name="adaln",
    )(cvec, ada_w, ada_b.reshape(DEPTH, 1, N_MOD))


def _norm_mod(x, g, shift, scale):
    y = x * lax.rsqrt(jnp.mean(x * x, axis=-1, keepdims=True) + EPS) * g
    return y * (1.0 + scale) + shift


def _stream_specs(x_parts, tr, width, pos):
    if len(x_parts) == 1:
        return [pl.BlockSpec((tr, width), lambda *idx: (idx[pos], 0))]
    n_pt = N_P // tr
    return [pl.BlockSpec((tr, width), lambda *idx: (jnp.minimum(idx[pos], n_pt - 1), 0)),
            pl.BlockSpec((tr, width), lambda *idx: (jnp.maximum(idx[pos] - n_pt, 0), 0))]


def _stream_tile(x_refs, row0):
    if len(x_refs) == 1:
        return x_refs[0][...]
    return jnp.where(row0 < N_P, x_refs[0][...], x_refs[1][...])


def _norm_mod_kernel(*refs, tr, n_parts):
    x_refs, (g_ref, sh_ref, sc_ref, o_ref) = refs[:n_parts], refs[n_parts:]
    row0 = pl.program_id(0) * tr
    x = _stream_tile(x_refs, row0)
    o_ref[...] = _norm_mod(x, g_ref[...], _mod_row(sh_ref, row0), _mod_row(sc_ref, row0)).astype(BF16)


def _norm_mod_call(x_parts, g, mod, layer, blk, tr=512):
    return pl.pallas_call(
        functools.partial(_norm_mod_kernel, tr=tr, n_parts=len(x_parts)),
        grid=(N_TOK // tr,),
        in_specs=_stream_specs(x_parts, tr, D_MODEL, 0) + [
            pl.BlockSpec((1, D_MODEL), lambda i: (0, 0)),
            _mod_spec(layer, blk, D_MODEL, None),
            _mod_spec(layer, blk + 1, D_MODEL, None)],
        out_specs=pl.BlockSpec((tr, D_MODEL), lambda i: (i, 0)),
        out_shape=jax.ShapeDtypeStruct((N_TOK, D_MODEL), BF16),
        compiler_params=_params("arbitrary"),
        name="norm_mod",
    )(*x_parts, g.reshape(1, D_MODEL), mod, mod)


CAST_ROWS = 64


def _proj_kernel(h_ref, w_ref, wo_ref, o_ref, wob_ref, wb_ref):
    @pl.when(pl.program_id(1) == 0)
    def _():
        wb_ref[...] = w_ref[...].astype(BF16)

    wob_ref[...] = wo_ref[...].astype(BF16)
    o_ref[...] = jnp.dot(h_ref[...], wb_ref[...], preferred_element_type=F32).astype(o_ref.dtype)


def _proj(h, w, w_out, tm=1024, tn=1024):
    m, k = h.shape
    n = w.shape[1]
    k2, n2 = w_out.shape
    n_i = m // tm
    n_cast = k2 // CAST_ROWS
    assert n_cast * CAST_ROWS == k2 and n_cast <= (n // tn) * n_i

    def cast_block(j, i):
        return (jnp.minimum(j * n_i + i, n_cast - 1), 0)

    return pl.pallas_call(
        _proj_kernel,
        grid=(n // tn, n_i),
        in_specs=[pl.BlockSpec((tm, k), lambda j, i: (i, 0)),
                  pl.BlockSpec((k, tn), lambda j, i: (0, j)),
                  pl.BlockSpec((CAST_ROWS, n2), cast_block)],
        out_specs=[pl.BlockSpec((tm, tn), lambda j, i: (i, j)),
                   pl.BlockSpec((CAST_ROWS, n2), cast_block)],
        out_shape=[jax.ShapeDtypeStruct((m, n), BF16),
                   jax.ShapeDtypeStruct((k2, n2), BF16)],
        scratch_shapes=[pltpu.VMEM((k, tn), BF16)],
        compiler_params=_params("arbitrary", "arbitrary"),
        name="proj",
    )(h, w, w_out)


def _shift_rows(p, prev_row, next_row, row0, seq_len):
    n = p.shape[0]
    ridx = lax.broadcasted_iota(jnp.int32, (n, 1), 0)
    pos = (row0 + ridx) & (seq_len - 1)
    p_prev = jnp.where(ridx == 0, prev_row, pltpu.roll(p, 1, 0))
    p_prev = jnp.where(pos == 0, 0.0, p_prev)
    p_next = jnp.where(ridx == n - 1, next_row, pltpu.roll(p, n - 1, 0))
    p_next = jnp.where(pos == seq_len - 1, 0.0, p_next)
    return p_prev, p_next


def _sconv_kernel(ab_ref, ac_ref, ax_ref, pc_ref, px_ref, nc_ref, nx_ref, w_ref, o_ref, *, tr):
    row0 = pl.program_id(0) * tr
    p = ac_ref[...].astype(F32) * ax_ref[...].astype(F32)
    prev_row = pc_ref[HALO - 1:HALO, :].astype(F32) * px_ref[HALO - 1:HALO, :].astype(F32)
    next_row = nc_ref[0:1, :].astype(F32) * nx_ref[0:1, :].astype(F32)
    p_prev, p_next = _shift_rows(p, prev_row, next_row, row0, _seq_len(row0))
    w = w_ref[...]
    conv = w[0:1, :] * p_prev + w[1:2, :] * p + w[2:3, :] * p_next
    o_ref[...] = (ab_ref[...].astype(F32) * conv).astype(BF16)


def _halo_specs(tr, width, col):
    per = tr // HALO
    last = N_TOK // HALO - 1
    prev = pl.BlockSpec((HALO, width), lambda i: (jnp.maximum(i * per - 1, 0), col))
    nxt = pl.BlockSpec((HALO, width), lambda i: (jnp.minimum((i + 1) * per, last), col))
    return prev, nxt


def _short_gated_conv(proj, conv_a, tr=512):
    pc, nc = _halo_specs(tr, W_A, 1)
    px, nx = _halo_specs(tr, W_A, 2)
    return pl.pallas_call(
        functools.partial(_sconv_kernel, tr=tr),
        grid=(N_TOK // tr,),
        in_specs=[pl.BlockSpec((tr, W_A), lambda i: (i, 0)),
                  pl.BlockSpec((tr, W_A), lambda i: (i, 1)),
                  pl.BlockSpec((tr, W_A), lambda i: (i, 2)),
                  pc, px, nc, nx,
                  pl.BlockSpec((SHORT_K, W_A), lambda i: (0, 0))],
        out_specs=pl.BlockSpec((tr, W_A), lambda i: (i, 0)),
        out_shape=jax.ShapeDtypeStruct((N_TOK, W_A), BF16),
        compiler_params=_params("arbitrary"),
        name="short_gated_conv",
    )(proj, proj, proj, proj, proj, proj, proj, conv_a)


ROWS = DEC_SEQ // GRID_W
NA_QROWS = 8
NA_KROWS = 16
NA_NQ = NA_QROWS * GRID_W
NA_NK = NA_KROWS * GRID_W
NA_STEPS = ROWS // NA_QROWS
QKV_COL0 = 3 * W_A // NA_HEAD_DIM
HEAD_SCALE = NA_HEAD_DIM ** -0.5
LOG2E = math.log2(math.e)
NA_CHUNK = 16
NT_DIMS = (((1,), (1,)), ((), ()))


def _head_rmsnorm(x, g):
    return x * lax.rsqrt(jnp.mean(x * x, axis=-1, keepdims=True) + EPS) * g


def _na_key_row0(j):
    return jnp.clip(j * NA_QROWS - NA_WIN_R // 2, 0, ROWS - NA_KROWS)


NA_SLOT_LO = NA_QROWS
NA_SLOTS = 2 * NA_KROWS


def _na_geometry(j):
    kr0 = min(max(j * NA_QROWS - NA_WIN_R // 2, 0), ROWS - NA_KROWS)
    per_a = []
    for a in range(NA_QROWS):
        r = j * NA_QROWS + a
        r0 = min(max(r - NA_WIN_R // 2, 0), ROWS - NA_WIN_R)
        per_a.append((kr0 - r + NA_WIN_R - 1 + NA_SLOT_LO, r0 - kr0))
    return kr0, per_a


def _na_bias_slots(rpb):
    n_dr = rpb.shape[1]
    period = 2 * GRID_W - 1
    gap = jnp.full((NA_HEADS, n_dr, period - (2 * NA_WIN_C - 1)), NEG_INF, F32)
    ring = jnp.concatenate([rpb[:, :, NA_WIN_C - 1:].astype(F32), gap, rpb[:, :, :NA_WIN_C - 1].astype(F32)],
                           axis=-1)
    toep = jnp.tile(ring, (1, 1, GRID_W))[:, :, :GRID_W * (period - 1)]
    toep = toep.reshape(NA_HEADS, n_dr, GRID_W, period - 1)[:, :, :, :GRID_W]
    qc = jnp.arange(GRID_W)[:, None]
    kc = jnp.arange(GRID_W)[None, :]
    cs = jnp.clip(qc - NA_WIN_C // 2, 0, GRID_W - NA_WIN_C)
    col = jnp.where((kc >= cs) & (kc < cs + NA_WIN_C), toep, NEG_INF).transpose(0, 2, 1, 3)
    col = jnp.pad(col, ((0, 0), (0, 0), (NA_SLOT_LO, NA_SLOTS + 1 - NA_SLOT_LO - n_dr), (0, 0)),
                  constant_values=NEG_INF)
    even = col[:, :, :NA_SLOTS].reshape(NA_HEADS, GRID_W, NA_SLOTS * GRID_W)
    odd = col[:, :, 1:].reshape(NA_HEADS, GRID_W, NA_SLOTS * GRID_W)
    return even, odd


def _na_build_bias(j_static, even_ref, odd_ref, bias_ref):
    _, per_a = _na_geometry(j_static)
    key_row = lax.broadcasted_iota(jnp.int32, (1, NA_NK), 1) // GRID_W
    for a, (m, lo) in enumerate(per_a):
        src = even_ref if m % 2 == 0 else odd_ref
        start = (m - m % 2) * GRID_W
        slab = src[:, start:start + NA_NK] * LOG2E
        ok = (key_row >= lo) & (key_row < lo + NA_WIN_R)
        bias_ref[a * GRID_W:(a + 1) * GRID_W, :] = jnp.where(ok, slab, NEG_INF)


def _with_ones_column(v):
    lane = lax.broadcasted_iota(jnp.int32, v.shape, 1)
    return jnp.concatenate([v.astype(BF16), jnp.where(lane == 0, 1.0, 0.0).astype(BF16)], axis=1)


def _na_kernel(q_ref, k_ref, v_ref, ck_ref, cv_ref, even_ref, odd_ref, qg_ref, kg_ref, o_ref,
               kn_ref, vx_ref, ckb_ref, cvx_ref, bias_ref, s_ref, p_ref):
    j = pl.program_id(2)
    hd = NA_HEAD_DIM

    @pl.when(j == 0)
    def _():
        kn_ref[...] = _head_rmsnorm(k_ref[...].astype(F32), kg_ref[...]).astype(BF16)
        vx_ref[...] = _with_ones_column(v_ref[...])
        ckb_ref[...] = ck_ref[...].astype(BF16)
        cvx_ref[...] = _with_ones_column(cv_ref[...])

    for j_static in (0, 1, NA_STEPS - 1):
        @pl.when(j == j_static)
        def _(j_static=j_static):
            _na_build_bias(j_static, even_ref, odd_ref, bias_ref)

    q = (_head_rmsnorm(q_ref[...].astype(F32), qg_ref[...]) * (HEAD_SCALE * LOG2E)).astype(BF16)
    tok0 = pl.multiple_of(_na_key_row0(j) * GRID_W, GRID_W)
    s_ref[:, :NA_NK] = lax.dot_general(q, kn_ref[pl.ds(tok0, NA_NK), :], NT_DIMS, preferred_element_type=F32)
    s_ref[:, NA_NK:] = lax.dot_general(q, ckb_ref[...], NT_DIMS, preferred_element_type=F32)
    for c in range(NA_NQ // NA_CHUNK):
        rows = slice(c * NA_CHUNK, (c + 1) * NA_CHUNK)
        s_loc = s_ref[rows, :NA_NK] + bias_ref[rows, :]
        s_ctx = s_ref[rows, NA_NK:]
        m = functools.reduce(jnp.maximum, [blk[:, t:t + hd] for blk in (s_loc, s_ctx)
                                           for t in range(0, blk.shape[1], hd)])
        m = jnp.max(m, axis=-1, keepdims=True)
        p_ref[rows, :NA_NK] = jnp.exp2(s_loc - m).astype(BF16)
        p_ref[rows, NA_NK:] = jnp.exp2(s_ctx - m).astype(BF16)
    o = (jnp.dot(p_ref[:, :NA_NK], vx_ref[pl.ds(tok0, NA_NK), :], preferred_element_type=F32)
         + jnp.dot(p_ref[:, NA_NK:], cvx_ref[...], preferred_element_type=F32))
    o_ref[...] = (o[:, :hd] / o[:, hd:hd + 1]).astype(BF16)


def _na_attention(proj, cache_k, cache_v, rpb, q_norm, k_norm):
    assert NA_STEPS >= 3 and all(_na_geometry(j)[1] == _na_geometry(1)[1] for j in range(1, NA_STEPS - 1))
    even, odd = _na_bias_slots(rpb)
    ck = cache_k.reshape(DEC_BATCH, PAST_LEN, W_B)
    cv = cache_v.reshape(DEC_BATCH, PAST_LEN, W_B)
    hd = NA_HEAD_DIM
    q_blk0 = N_P // NA_NQ
    kv_blk0 = N_P // DEC_SEQ
    slots = pl.BlockSpec((None, GRID_W, NA_SLOTS * GRID_W), lambda b, h, j: (h, 0, 0))
    return pl.pallas_call(
        _na_kernel,
        grid=(DEC_BATCH, NA_HEADS, NA_STEPS),
        in_specs=[
            pl.BlockSpec((NA_NQ, hd), lambda b, h, j: (q_blk0 + b * NA_STEPS + j, QKV_COL0 + h)),
            pl.BlockSpec((DEC_SEQ, hd), lambda b, h, j: (kv_blk0 + b, QKV_COL0 + NA_HEADS + h)),
            pl.BlockSpec((DEC_SEQ, hd), lambda b, h, j: (kv_blk0 + b, QKV_COL0 + 2 * NA_HEADS + h)),
            pl.BlockSpec((None, PAST_LEN, hd), lambda b, h, j: (b, 0, h)),
            pl.BlockSpec((None, PAST_LEN, hd), lambda b, h, j: (b, 0, h)),
            slots, slots,
            pl.BlockSpec((1, hd), lambda b, h, j: (0, 0)),
            pl.BlockSpec((1, hd), lambda b, h, j: (0, 0)),
        ],
        out_specs=pl.BlockSpec((NA_NQ, hd), lambda b, h, j: (b * NA_STEPS + j, h)),
        out_shape=jax.ShapeDtypeStruct((N_S, W_B), BF16),
        scratch_shapes=[pltpu.VMEM((DEC_SEQ, hd), BF16),
                        pltpu.VMEM((DEC_SEQ, 2 * hd), BF16),
                        pltpu.VMEM((PAST_LEN, hd), BF16),
                        pltpu.VMEM((PAST_LEN, 2 * hd), BF16),
                        pltpu.VMEM((NA_NQ, NA_NK), F32),
                        pltpu.VMEM((NA_NQ, NA_NK + PAST_LEN), F32),
                        pltpu.VMEM((NA_NQ, NA_NK + PAST_LEN), BF16)],
        compiler_params=_params("arbitrary", "arbitrary", "arbitrary"),
        name="na_attention",
    )(proj, proj, proj, ck, cv, even, odd, q_norm.reshape(1, hd), k_norm.reshape(1, hd))


def _ctx_kernel(q_ref, k_ref, v_ref, qg_ref, kg_ref, o_ref, kn_ref, vo_ref):
    vo_ref[...] = v_ref[...].astype(F32)
    for h in range(NA_HEADS):
        cols = slice(h * NA_HEAD_DIM, (h + 1) * NA_HEAD_DIM)
        q = (_head_rmsnorm(q_ref[:, cols].astype(F32), qg_ref[...]) * HEAD_SCALE).astype(BF16)
        kn = _head_rmsnorm(k_ref[:, cols].astype(F32), kg_ref[...])
        kn_ref[:, cols] = kn
        s = lax.dot_general(q, kn.astype(BF16), NT_DIMS, preferred_element_type=F32)
        p = jnp.exp(s - jnp.max(s, axis=-1, keepdims=True))
        denom = jnp.sum(p, axis=-1, keepdims=True)
        o = jnp.dot(p.astype(BF16), v_ref[:, cols], preferred_element_type=F32)
        o_ref[:, cols] = (o / denom).astype(BF16)


def _ctx_attention(proj, q_norm, k_norm):
    hd = NA_HEAD_DIM
    col0 = 3 * W_A // W_B
    ospec = pl.BlockSpec((SEQ, W_B), lambda b: (b, 0))
    return pl.pallas_call(
        _ctx_kernel,
        grid=(BATCH,),
        in_specs=[
            pl.BlockSpec((SEQ, W_B), lambda b: (b, col0)),
            pl.BlockSpec((SEQ, W_B), lambda b: (b, col0 + 1)),
            pl.BlockSpec((SEQ, W_B), lambda b: (b, col0 + 2)),
            pl.BlockSpec((1, hd), lambda b: (0, 0)),
            pl.BlockSpec((1, hd), lambda b: (0, 0)),
        ],
        out_specs=[ospec, ospec, ospec],
        out_shape=[jax.ShapeDtypeStruct((N_P, W_B), BF16),
                   jax.ShapeDtypeStruct((N_P, W_B), F32),
                   jax.ShapeDtypeStruct((N_P, W_B), F32)],
        compiler_params=_params("arbitrary"),
        name="ctx_attention",
    )(proj, proj, proj, q_norm.reshape(1, hd), k_norm.reshape(1, hd))


def _out_proj_kernel(*refs, tm, ka, n_yb, n_x):
    ya_ref = refs[0]
    yb_refs = refs[1:1 + n_yb]
    w_ref = refs[1 + n_yb]
    x_refs = refs[2 + n_yb:2 + n_yb + n_x]
    gate_ref, g_ref, sh_ref, sc_ref, xo_ref, ho_ref = refs[2 + n_yb + n_x:]
    row0 = pl.program_id(0) * tm
    o = (jnp.dot(ya_ref[...], w_ref[0:ka, :], preferred_element_type=F32)
         + jnp.dot(_stream_tile(yb_refs, row0), w_ref[ka:, :], preferred_element_type=F32))
    x_new = _stream_tile(x_refs, row0) + _mod_row(gate_ref, row0) * o
    xo_ref[...] = x_new
    ho_ref[...] = _norm_mod(x_new, g_ref[...], _mod_row(sh_ref, row0), _mod_row(sc_ref, row0)).astype(BF16)


def _out_proj(ya, yb_parts, w_bf16, x_parts, mod, layer, g_ffn, tm=512):
    ka, kb = ya.shape[1], yb_parts[0].shape[1]
    return pl.pallas_call(
        functools.partial(_out_proj_kernel, tm=tm, ka=ka, n_yb=len(yb_parts), n_x=len(x_parts)),
        grid=(N_TOK // tm,),
        in_specs=[pl.BlockSpec((tm, ka), lambda i: (i, 0))]
        + _stream_specs(yb_parts, tm, kb, 0)
        + [pl.BlockSpec((ka + kb, D_MODEL), lambda i: (0, 0))]
        + _stream_specs(x_parts, tm, D_MODEL, 0) + [
            _mod_spec(layer, 2, D_MODEL, None),
            pl.BlockSpec((1, D_MODEL), lambda i: (0, 0)),
            _mod_spec(layer, 3, D_MODEL, None),
            _mod_spec(layer, 4, D_MODEL, None)],
        out_specs=[pl.BlockSpec((tm, D_MODEL), lambda i: (i, 0)),
                   pl.BlockSpec((tm, D_MODEL), lambda i: (i, 0))],
        out_shape=[jax.ShapeDtypeStruct((N_TOK, D_MODEL), F32),
                   jax.ShapeDtypeStruct((N_TOK, D_MODEL), BF16)],
        compiler_params=_params("arbitrary"),
        name="out_proj",
    )(ya, *yb_parts, w_bf16, *x_parts, mod, g_ffn.reshape(1, D_MODEL), mod, mod)


GELU_C = math.sqrt(2.0 / math.pi)


def _gelu_tanh(x):
    return 0.5 * x * (1.0 + jnp.tanh(GELU_C * (x + 0.044715 * (x * x * x))))


def _ffn_in_kernel(h_ref, hp_ref, hn_ref, wa_ref, wg_ref, cw_ref, wo_ref, o_ref, wob_ref, wab_ref, wgb_ref,
                   *, tm, tn):
    i = pl.program_id(1)

    @pl.when(i == 0)
    def _():
        wab_ref[...] = wa_ref[...].astype(BF16)
        wgb_ref[...] = wg_ref[...].astype(BF16)

    wob_ref[...] = wo_ref[...].astype(BF16)

    row0 = i * tm
    seq_len = _seq_len(row0)
    h = h_ref[...]
    h_ext = jnp.concatenate([hp_ref[...], h, hn_ref[...]], axis=0)
    n = tm + 2 * HALO
    pos = (row0 + lax.broadcasted_iota(jnp.int32, (tm, 1), 0)) & (seq_len - 1)
    first = pos == 0
    last = pos == seq_len - 1
    a_ext = jnp.dot(h_ext, wab_ref[...], preferred_element_type=F32)
    g = jnp.dot(h, wgb_ref[...], preferred_element_type=F32)
    a_mid = a_ext[HALO:HALO + tm, :]
    a_prev = jnp.where(first, 0.0, pltpu.roll(a_ext, 1, 0)[HALO:HALO + tm, :])
    a_next = jnp.where(last, 0.0, pltpu.roll(a_ext, n - 1, 0)[HALO:HALO + tm, :])
    cw = cw_ref[...]
    conv = cw[0:1, :] * a_prev + cw[1:2, :] * a_mid + cw[2:3, :] * a_next
    o_ref[...] = (_gelu_tanh(conv) * g).astype(BF16)


def _ffn_in(h, w_in, conv, w_out, layer, tm=1024, tn=512):
    per = tm // HALO
    last = N_TOK // HALO - 1
    ng = D_FF // tn
    n_i = N_TOK // tm
    per_j = D_FF // ng // CAST_ROWS
    assert per_j <= n_i and per_j * CAST_ROWS * ng == D_FF

    def cast_block(j, i):
        return j * per_j + jnp.minimum(i, per_j - 1)

    return pl.pallas_call(
        functools.partial(_ffn_in_kernel, tm=tm, tn=tn),
        grid=(ng, n_i),
        in_specs=[pl.BlockSpec((tm, D_MODEL), lambda j, i: (i, 0)),
                  pl.BlockSpec((HALO, D_MODEL), lambda j, i: (jnp.maximum(i * per - 1, 0), 0)),
                  pl.BlockSpec((HALO, D_MODEL), lambda j, i: (jnp.minimum((i + 1) * per, last), 0)),
                  pl.BlockSpec((None, D_MODEL, tn), lambda j, i: (layer, 0, j)),
                  pl.BlockSpec((None, D_MODEL, tn), lambda j, i: (layer, 0, ng + j)),
                  pl.BlockSpec((None, SHORT_K, tn), lambda j, i: (layer, 0, j)),
                  pl.BlockSpec((None, CAST_ROWS, D_MODEL), lambda j, i: (layer, cast_block(j, i), 0))],
        out_specs=[pl.BlockSpec((tm, tn), lambda j, i: (i, j)),
                   pl.BlockSpec((CAST_ROWS, D_MODEL), lambda j, i: (cast_block(j, i), 0))],
        out_shape=[jax.ShapeDtypeStruct((N_TOK, D_FF), BF16),
                   jax.ShapeDtypeStruct((D_FF, D_MODEL), BF16)],
        scratch_shapes=[pltpu.VMEM((D_MODEL, tn), BF16), pltpu.VMEM((D_MODEL, tn), BF16)],
        compiler_params=_params("arbitrary", "arbitrary"),
        name="ffn_in",
    )(h, h, h, w_in, w_in, conv, w_out)


def _ffn_out_kernel(u_ref, w_ref, x_ref, gate_ref, o_ref, *, tm, row_start):
    row0 = row_start + pl.program_id(1) * tm
    o = jnp.dot(u_ref[...], w_ref[...], preferred_element_type=F32)
    o_ref[...] = x_ref[...] + _mod_row(gate_ref, row0) * o


def _ffn_out(u, w_bf16, x, mod, layer, row_start=0, n_rows=N_TOK, tm=512, tn=1024):
    t0 = row_start // tm
    return pl.pallas_call(
        functools.partial(_ffn_out_kernel, tm=tm, row_start=row_start),
        grid=(D_MODEL // tn, n_rows // tm),
        in_specs=[pl.BlockSpec((tm, D_FF), lambda j, i: (t0 + i, 0)),
                  pl.BlockSpec((D_FF, tn), lambda j, i: (0, j)),
                  pl.BlockSpec((tm, tn), lambda j, i: (t0 + i, j)),
                  _mod_spec(layer, 5, tn, 0)],
        out_specs=pl.BlockSpec((tm, tn), lambda j, i: (i, j)),
        out_shape=jax.ShapeDtypeStruct((n_rows, D_MODEL), F32),
        compiler_params=_params("arbitrary", "arbitrary"),
        name="ffn_out",
    )(u, w_bf16, x, mod)


CONF_TR = 256
CONF_PAD = CONF_K // 2


def _conformer_kernel(ca_ref, cg_ref, pa_ref, pg_ref, na_ref, ng_ref, w_ref, b_ref, lg_ref, lb_ref, o_ref):
    tr = CONF_TR
    row0 = pl.program_id(0) * tr
    seq_len = _seq_len(row0)
    pos0 = row0 & (seq_len - 1)

    def glu(a_ref, g_ref):
        return a_ref[...].astype(F32) * jax.nn.sigmoid(g_ref[...].astype(F32))

    prev = jnp.where(pos0 == 0, 0.0, glu(pa_ref, pg_ref))
    nxt = jnp.where(pos0 + tr == seq_len, 0.0, glu(na_ref, ng_ref))
    u_ext = jnp.concatenate([prev, glu(ca_ref, cg_ref), nxt], axis=0)
    w = w_ref[...]
    acc = jnp.zeros((tr, W_C), F32) + b_ref[...]
    n = tr + 2 * HALO
    base = HALO - CONF_PAD
    for r in range(8):
        u_r = pltpu.roll(u_ext, n - (base + r), 0)
        for a in range(len(range(r, CONF_K, 8))):
            k = 8 * a + r
            acc = acc + w[k:k + 1, :] * u_r[8 * a:8 * a + tr, :]
    mu = jnp.mean(acc, axis=-1, keepdims=True)
    d = acc - mu
    var = jnp.mean(d * d, axis=-1, keepdims=True)
    y = d * lax.rsqrt(var + EPS) * lg_ref[...] + lb_ref[...]
    o_ref[...] = (y * jax.nn.sigmoid(y)).astype(BF16)


def _conformer(proj, dw, dw_b, ln_g, ln_b):
    pa, na = _halo_specs(CONF_TR, W_C, 0)
    pg, ng = _halo_specs(CONF_TR, W_C, 1)
    row = pl.BlockSpec((1, W_C), lambda i: (0, 0))
    return pl.pallas_call(
        _conformer_kernel,
        grid=(N_TOK // CONF_TR,),
        in_specs=[pl.BlockSpec((CONF_TR, W_C), lambda i: (i, 0)),
                  pl.BlockSpec((CONF_TR, W_C), lambda i: (i, 1)),
                  pa, pg, na, ng,
                  pl.BlockSpec((CONF_K, W_C), lambda i: (0, 0)), row, row, row],
        out_specs=pl.BlockSpec((CONF_TR, W_C), lambda i: (i, 0)),
        out_shape=jax.ShapeDtypeStruct((N_TOK, W_C), BF16),
        compiler_params=_params("arbitrary"),
        name="conformer",
    )(proj, proj, proj, proj, proj, proj, dw, dw_b.reshape(1, W_C), ln_g.reshape(1, W_C), ln_b.reshape(1, W_C))


HY_COL0 = 2 * W_C // W_D


def _hy_short_kernel(*refs, tr):
    (x0_ref, x1_ref, vv_ref, p0_ref, p1_ref, pv_ref, n0_ref, n1_ref, nv_ref,
     w0_ref, w1_ref, wv_ref, b0_ref, b1_ref, bv_ref, x0o_ref, wo_ref) = refs
    row0 = pl.program_id(0) * tr
    seq_len = _seq_len(row0)

    def conv(c_ref, p_ref, n_ref, w_ref, b_ref):
        p = c_ref[...].astype(F32)
        p_prev, p_next = _shift_rows(p, p_ref[HALO - 1:HALO, :].astype(F32), n_ref[0:1, :].astype(F32),
                                     row0, seq_len)
        w = w_ref[...]
        return w[0:1, :] * p_prev + w[1:2, :] * p + w[2:3, :] * p_next + b_ref[...]

    x0o_ref[...] = conv(x0_ref, p0_ref, n0_ref, w0_ref, b0_ref).astype(BF16)
    x1 = conv(x1_ref, p1_ref, n1_ref, w1_ref, b1_ref)
    vv = conv(vv_ref, pv_ref, nv_ref, wv_ref, bv_ref)
    wo_ref[...] = (vv * x1).astype(BF16)


def _hy_short(proj, hy_short, hy_short_b, tr=512):
    main = [pl.BlockSpec((tr, W_D), functools.partial(lambda i, c: (i, c), c=HY_COL0 + c)) for c in range(3)]
    halos = [_halo_specs(tr, W_D, HY_COL0 + c) for c in range(3)]
    wspec = [pl.BlockSpec((SHORT_K, W_D), functools.partial(lambda i, c: (0, c), c=c)) for c in range(3)]
    bspec = [pl.BlockSpec((1, W_D), functools.partial(lambda i, c: (0, c), c=c)) for c in range(3)]
    out = jax.ShapeDtypeStruct((N_TOK, W_D), BF16)
    ospec = pl.BlockSpec((tr, W_D), lambda i: (i, 0))
    b2 = hy_short_b.reshape(1, 3 * W_D)
    return pl.pallas_call(
        functools.partial(_hy_short_kernel, tr=tr),
        grid=(N_TOK // tr,),
        in_specs=main + [h[0] for h in halos] + [h[1] for h in halos] + wspec + bspec,
        out_specs=[ospec, ospec],
        out_shape=[out, out],
        compiler_params=_params("arbitrary"),
        name="hy_short",
    )(*([proj] * 9), hy_short, hy_short, hy_short, b2, b2, b2)


HIGHEST = lax.Precision.HIGHEST
HY_EMB_PAD = 128


def _dft_gen_kernel(ct_ref, st_ref, cr_ref, sr_ref, c_ref, s_ref):
    t = pl.program_id(0)
    cr = cr_ref[pl.ds(t, 1), :]
    sr = sr_ref[pl.ds(t, 1), :]
    ct = ct_ref[...]
    st = st_ref[...]
    c_ref[...] = (cr * ct - sr * st).astype(BF16)
    s_ref[...] = (sr * ct + cr * st).astype(BF16)


def _dft_matrices(L):
    tr = min(L, 256)
    nt = L // tr
    nt_pad = -(-nt // 8) * 8
    n = jnp.arange(L, dtype=jnp.int32)[None, :]

    def tables(k):
        ang = ((k[:, None] * n) % (2 * L)).astype(F32) * (math.pi / L)
        return jnp.cos(ang), jnp.sin(ang)

    ct, st = tables(jnp.arange(tr, dtype=jnp.int32))
    cr, sr = tables(jnp.arange(nt_pad, dtype=jnp.int32) * tr)
    tile = pl.BlockSpec((tr, L), lambda t: (0, 0))
    rows = pl.BlockSpec((nt_pad, L), lambda t: (0, 0))
    out = pl.BlockSpec((tr, L), lambda t: (t, 0))
    return pl.pallas_call(
        _dft_gen_kernel,
        grid=(nt,),
        in_specs=[tile, tile, rows, rows],
        out_specs=[out, out],
        out_shape=[jax.ShapeDtypeStruct((L, L), BF16)] * 2,
        compiler_params=_params("arbitrary"),
        name="dft_gen",
    )(ct, st, cr, sr)


def _hy_features(L):
    t = jnp.linspace(0.0, 1.0, L, dtype=F32)[:, None]
    bands = (HY_EMB - 1) // 2
    ang = 2 * math.pi * jnp.arange(L, dtype=F32)[:, None] / L
    freqs = jnp.linspace(1e-4, bands - 1, bands, dtype=F32)[None, :]
    z = jnp.concatenate([t, jnp.cos(freqs * ang), -jnp.sin(freqs * ang)], axis=-1)
    return jnp.pad(z, ((0, 0), (0, HY_EMB_PAD - HY_EMB))), t


def _hy_filter_kernel(z_ref, t_ref, w1_ref, b1_ref, f1_ref, w2_ref, b2_ref, f2_ref, w3_ref, dl_ref,
                      fs_ref, fd_ref, hn_ref, *, tr):
    i = pl.program_id(0)
    hid = jnp.sin(f1_ref[...] * (jnp.dot(z_ref[...], w1_ref[...], precision=HIGHEST,
                                         preferred_element_type=F32) + b1_ref[...]))
    hid = jnp.sin(f2_ref[...] * (jnp.dot(hid, w2_ref[...], precision=HIGHEST,
                                         preferred_element_type=F32) + b2_ref[...]))
    hf = jnp.dot(hid.astype(BF16), w3_ref[...].astype(BF16), preferred_element_type=F32)
    decay = jnp.exp(-t_ref[...] * jnp.abs(dl_ref[...]))
    ridx = i * tr + lax.broadcasted_iota(jnp.int32, (tr, 1), 0)
    fwd = hf[:, :W_D] * decay
    bwd = jnp.where(ridx == 0, 0.0, hf[:, W_D:] * decay)
    fsum = fwd + bwd
    fs_ref[...] = fsum.astype(BF16)
    fd_ref[...] = (bwd - fwd).astype(BF16)
    sgn = (1 - 2 * (ridx & 1)).astype(F32)
    part = jnp.sum(fsum * sgn, axis=0, keepdims=True)

    @pl.when(i == 0)
    def _():
        hn_ref[...] = jnp.zeros_like(hn_ref)

    hn_ref[0:1, :] += part


def _hy_filter(L, w1, b1, f1, w2, b2, f2, w3):
    tr = min(L, 512)
    z, t = _hy_features(L)
    max_decay = math.log(HY_TARGET) / HY_FAST_PCT
    min_decay = math.log(HY_TARGET) / HY_SLOW_PCT
    deltas = jnp.linspace(min_decay, max_decay, W_D, dtype=F32)[None, :]
    w1p = jnp.pad(w1, ((0, HY_EMB_PAD - HY_EMB), (0, 0)))
    full = lambda shape: pl.BlockSpec(shape, lambda i: (0, 0))
    return pl.pallas_call(
        functools.partial(_hy_filter_kernel, tr=tr),
        grid=(L // tr,),
        in_specs=[pl.BlockSpec((tr, HY_EMB_PAD), lambda i: (i, 0)),
                  pl.BlockSpec((tr, 1), lambda i: (i, 0)),
                  full((HY_EMB_PAD, HY_HIDDEN)), full((1, HY_HIDDEN)), full((1, HY_HIDDEN)),
                  full((HY_HIDDEN, HY_HIDDEN)), full((1, HY_HIDDEN)), full((1, HY_HIDDEN)),
                  full((HY_HIDDEN, 2 * W_D)), full((1, W_D))],
        out_specs=[pl.BlockSpec((tr, W_D), lambda i: (i, 0)),
                   pl.BlockSpec((tr, W_D), lambda i: (i, 0)),
                   pl.BlockSpec((8, W_D), lambda i: (0, 0))],
        out_shape=[jax.ShapeDtypeStruct((L, W_D), BF16),
                   jax.ShapeDtypeStruct((L, W_D), BF16),
                   jax.ShapeDtypeStruct((8, W_D), F32)],
        compiler_params=_params("arbitrary"),
        name="hy_filter",
    )(z, t, w1p, b1.reshape(1, -1), f1.reshape(1, -1), w2, b2.reshape(1, -1), f2.reshape(1, -1), w3, deltas)


def _hy_spec_kernel(c_ref, s_ref, fs_ref, fd_ref, hc_ref, hs_ref):
    hc_ref[...] = jnp.dot(c_ref[...], fs_ref[...], preferred_element_type=F32)
    hs_ref[...] = jnp.dot(s_ref[...], fd_ref[...], preferred_element_type=F32)


def _hy_spectrum(cmat, smat, fsum, fdif, tk):
    L = cmat.shape[0]
    tile = pl.BlockSpec((tk, L), lambda i: (i, 0))
    full = pl.BlockSpec((L, W_D), lambda i: (0, 0), pipeline_mode=pl.Buffered(1))
    out = pl.BlockSpec((tk, W_D), lambda i: (i, 0))
    return pl.pallas_call(
        _hy_spec_kernel,
        grid=(L // tk,),
        in_specs=[tile, tile, full, full],
        out_specs=[out, out],
        out_shape=[jax.ShapeDtypeStruct((L, W_D), F32)] * 2,
        compiler_params=_params("arbitrary"),
        name="hy_spectrum",
    )(cmat, smat, fsum, fdif)


def _hy_fwd_kernel(c_ref, s_ref, w_ref, hc_ref, hs_ref, hn_ref, yc_ref, ys_ref, yn_ref, *, L, tk):
    kt = pl.program_id(1)
    w = w_ref[...]
    xc = jnp.dot(c_ref[...], w, preferred_element_type=F32)
    xs = jnp.dot(s_ref[...], w, preferred_element_type=F32)
    hc = hc_ref[...]
    hs = hs_ref[...]
    kidx = kt * tk + lax.broadcasted_iota(jnp.int32, (tk, 1), 0)
    om = jnp.where(kidx == 0, 0.5 / L, 1.0 / L)
    yc_ref[...] = (om * (xc * hc + xs * hs)).astype(BF16)
    ys_ref[...] = ((1.0 / L) * (xs * hc - xc * hs)).astype(BF16)

    @pl.when(kt == 0)
    def _():
        n = lax.broadcasted_iota(jnp.int32, (L, 1), 0)
        sgn = (1 - 2 * (n & 1)).astype(F32)
        xn = jnp.sum(w.astype(F32) * sgn, axis=0, keepdims=True)
        yn_ref[...] = jnp.broadcast_to(xn * hn_ref[0:1, :] * (0.5 / L), yn_ref.shape)


def _hy_forward(cmat, smat, w, hc, hs, hn, L, nb, blk0, tk):
    tile = pl.BlockSpec((tk, L), lambda b, k: (k, 0))
    htile = pl.BlockSpec((tk, W_D), lambda b, k: (k, 0))
    out = pl.BlockSpec((tk, W_D), lambda b, k: (b * (L // tk) + k, 0))
    return pl.pallas_call(
        functools.partial(_hy_fwd_kernel, L=L, tk=tk),
        grid=(nb, L // tk),
        in_specs=[tile, tile,
                  pl.BlockSpec((L, W_D), lambda b, k: (blk0 + b, 0), pipeline_mode=pl.Buffered(1)),
                  htile, htile,
                  pl.BlockSpec((8, W_D), lambda b, k: (0, 0))],
        out_specs=[out, out, pl.BlockSpec((None, 8, W_D), lambda b, k: (b, 0, 0))],
        out_shape=[jax.ShapeDtypeStruct((nb * L, W_D), BF16),
                   jax.ShapeDtypeStruct((nb * L, W_D), BF16),
                   jax.ShapeDtypeStruct((nb, 8, W_D), F32)],
        compiler_params=_params("arbitrary", "arbitrary"),
        name="hy_forward",
    )(cmat, smat, w, hc, hs, hn)


def _hy_inv_kernel(c_ref, s_ref, yc_ref, ys_ref, yn_ref, x0_ref, w_ref, bias_ref, z_ref, *, tt):
    ti = pl.program_id(1)
    y = (jnp.dot(c_ref[...], yc_ref[...], preferred_element_type=F32)
         + jnp.dot(s_ref[...], ys_ref[...], preferred_element_type=F32))
    t = ti * tt + lax.broadcasted_iota(jnp.int32, (tt, 1), 0)
    sgn = (1 - 2 * (t & 1)).astype(F32)
    y = y + sgn * yn_ref[0:1, :] + w_ref[...].astype(F32) * bias_ref[...]
    z_ref[...] = (x0_ref[...].astype(F32) * y).astype(BF16)


def _hy_inverse(cmat, smat, yc, ys, yn, x0, w, bias, L, nb, blk0, tt):
    per = L // tt
    tile = pl.BlockSpec((tt, L), lambda b, t: (t, 0))
    seq = pl.BlockSpec((L, W_D), lambda b, t: (b, 0), pipeline_mode=pl.Buffered(1))
    rows_in = pl.BlockSpec((tt, W_D), lambda b, t: ((blk0 + b) * per + t, 0))
    return pl.pallas_call(
        functools.partial(_hy_inv_kernel, tt=tt),
        grid=(nb, per),
        in_specs=[tile, tile, seq, seq,
                  pl.BlockSpec((None, 8, W_D), lambda b, t: (b, 0, 0)),
                  rows_in, rows_in,
                  pl.BlockSpec((1, W_D), lambda b, t: (0, 0))],
        out_specs=pl.BlockSpec((tt, W_D), lambda b, t: (b * per + t, 0)),
        out_shape=jax.ShapeDtypeStruct((nb * L, W_D), BF16),
        compiler_params=_params("arbitrary", "arbitrary"),
        name="hy_inverse",
    )(cmat, smat, yc, ys, yn, x0, w, bias.reshape(1, W_D))


def _hyena_long_conv(x0, w, L, nb, blk0, filt_params, bias):
    tk = min(L, 512)
    cmat, smat = _dft_matrices(L)
    fsum, fdif, hn = _hy_filter(L, *filt_params)
    hc, hs = _hy_spectrum(cmat, smat, fsum, fdif, tk)
    yc, ys, yn = _hy_forward(cmat, smat, w, hc, hs, hn, L, nb, blk0, tk)
    return _hy_inverse(cmat, smat, yc, ys, yn, x0, w, bias, L, nb, blk0, tk)


def kernel(x_prompt, x_sample, cache_k, cache_v, c, c_ctx, ada_w, ada_b, norm_mix, norm_ffn,
           e_w_in, e_conv_a, e_q_norm, e_k_norm, e_rpb, e_w_out,
           o_w_in, o_conf_dw, o_conf_dw_b, o_conf_ln_g, o_conf_ln_b, o_hy_short, o_hy_short_b,
           o_hy_w1, o_hy_b1, o_hy_f1, o_hy_w2, o_hy_b2, o_hy_f2, o_hy_w3, o_hy_bias, o_w_out,
           ffn_in, ffn_conv, ffn_out):
    x_parts = (x_prompt.reshape(N_P, D_MODEL), x_sample.reshape(N_S, D_MODEL))
    cvec = jnp.concatenate([c_ctx[None, :], c, jnp.zeros((N_SEG_PAD - 1 - DEC_BATCH, D_MODEL), F32)], axis=0)
    mod = _adaln(cvec, ada_w, ada_b)
    ks_new, vs_new = [], []
    for layer in range(DEPTH):
        j = layer // 2
        last = layer == DEPTH - 1
        h = _norm_mod_call(x_parts, norm_mix[layer], mod, layer, 0)
        if layer % 2 == 0:
            proj, w_out = _proj(h, e_w_in[j], e_w_out[j])
            ya = _short_gated_conv(proj, e_conv_a[j])
            yb_p, k_p, v_p = _ctx_attention(proj, e_q_norm[j], e_k_norm[j])
            yb_s = _na_attention(proj, cache_k[:, j], cache_v[:, j], e_rpb[j], e_q_norm[j], e_k_norm[j])
            yb = (yb_p, yb_s)
            ks_new.append(k_p.reshape(BATCH, SEQ, NA_HEADS, NA_HEAD_DIM))
            vs_new.append(v_p.reshape(BATCH, SEQ, NA_HEADS, NA_HEAD_DIM))
        else:
            proj, w_out = _proj(h, o_w_in[j], o_w_out[j])
            ya = _conformer(proj, o_conf_dw[j], o_conf_dw_b[j], o_conf_ln_g[j], o_conf_ln_b[j])
            x0, w = _hy_short(proj, o_hy_short[j], o_hy_short_b[j])
            fp = (o_hy_w1[j], o_hy_b1[j], o_hy_f1[j], o_hy_w2[j], o_hy_b2[j], o_hy_f2[j], o_hy_w3[j])
            z_p = _hyena_long_conv(x0, w, SEQ, BATCH, 0, fp, o_hy_bias[j])
            z_s = _hyena_long_conv(x0, w, DEC_SEQ, DEC_BATCH, N_P // DEC_SEQ, fp, o_hy_bias[j])
            yb = (z_p, z_s)
        x, h_ffn = _out_proj(ya, yb, w_out, x_parts, mod, layer, norm_ffn[layer])
        u, w_ffn_out = _ffn_in(h_ffn, ffn_in, ffn_conv, ffn_out, layer)
        if last:
            x_parts = (_ffn_out(u, w_ffn_out, x, mod, layer, 0, N_P),
                       _ffn_out(u, w_ffn_out, x, mod, layer, N_P, N_S))
        else:
            x_parts = (_ffn_out(u, w_ffn_out, x, mod, layer),)
    xp = x_parts[0].reshape(BATCH, SEQ, D_MODEL)
    xs = x_parts[1].reshape(DEC_BATCH, DEC_SEQ, D_MODEL)
    return (xp, xs, jnp.stack(ks_new, axis=1), jnp.stack(vs_new, axis=1))
```

```python
import functools
import math

import jax
import jax.numpy as jnp
from jax import lax
from jax.experimental import pallas as pl
from jax.experimental.pallas import tpu as pltpu

D_MODEL = 2048
BATCH = 16
SEQ = 256
DEPTH = 2
DEC_BATCH = 2
DEC_SEQ = 4096
PAST_LEN = 512
GRID_W = 64
W_A = 1024
NA_HEADS = 8
NA_HEAD_DIM = 128
W_B = NA_HEADS * NA_HEAD_DIM
W_C = 1024
W_D = 1024
NA_WIN_R = 8
NA_WIN_C = 16
SHORT_K = 3
CONF_K = 31
D_FF = 5632
HY_EMB = 33
HY_HIDDEN = 64
HY_FAST_PCT = 0.3
HY_SLOW_PCT = 1.5
HY_TARGET = 1e-2
EPS = 1e-6
NEG_INF = -1e30

N_P = BATCH * SEQ
N_S = DEC_BATCH * DEC_SEQ
N_TOK = N_P + N_S
SEG = DEC_SEQ
N_SEG_PAD = 8
N_MOD = 6 * D_MODEL
HALO = 16
VMEM_LIMIT_BYTES = 60 * 1024 * 1024
BF16 = jnp.bfloat16
F32 = jnp.float32


def _params(*sem):
    return pltpu.CompilerParams(dimension_semantics=sem, vmem_limit_bytes=VMEM_LIMIT_BYTES)


def _seq_len(row0):
    return jnp.where(row0 < N_P, SEQ, DEC_SEQ)


def _mod_row(mod_ref, row0):
    return mod_ref[pl.ds(row0 // SEG, 1), :]


def _mod_spec(layer, blk, tn, index_pos):
    per = D_MODEL // tn

    def imap(*idx):
        j = idx[index_pos] if index_pos is not None else 0
        return (layer, 0, blk * per + j)

    return pl.BlockSpec((None, N_SEG_PAD, tn), imap)


def _adaln_kernel(c_ref, w_ref, b_ref, o_ref):
    c = c_ref[...]
    s = (c * jax.nn.sigmoid(c)).astype(BF16)
    o_ref[...] = jnp.dot(s, w_ref[...].astype(BF16), preferred_element_type=F32) + b_ref[...]


def _adaln(cvec, ada_w, ada_b, tn=1024):
    return pl.pallas_call(
        _adaln_kernel,
        grid=(DEPTH, N_MOD // tn),
        in_specs=[pl.BlockSpec((N_SEG_PAD, D_MODEL), lambda l, j: (0, 0)),
                  pl.BlockSpec((None, D_MODEL, tn), lambda l, j: (l, 0, j)),
                  pl.BlockSpec((None, 1, tn), lambda l, j: (l, 0, j))],
        out_specs=pl.BlockSpec((None, N_SEG_PAD, tn), lambda l, j: (l, 0, j)),
        out_shape=jax.ShapeDtypeStruct((DEPTH, N_SEG_PAD, N_MOD), F32),
        compiler_params=_params("arbitrary", "arbitrary"),
        name="adaln",
    )(cvec, ada_w, ada_b.reshape(DEPTH, 1, N_MOD))


def _norm_mod(x, g, shift, scale):
    y = x * lax.rsqrt(jnp.mean(x * x, axis=-1, keepdims=True) + EPS) * g
    return y * (1.0 + scale) + shift


def _stream_specs(x_parts, tr, width, pos):
    if len(x_parts) == 1:
        return [pl.BlockSpec((tr, width), lambda *idx: (idx[pos], 0))]
    n_pt = N_P // tr
    return [pl.BlockSpec((tr, width), lambda *idx: (jnp.minimum(idx[pos], n_pt - 1), 0)),
            pl.BlockSpec((tr, width), lambda *idx: (jnp.maximum(idx[pos] - n_pt, 0), 0))]


def _stream_tile(x_refs, row0):
    if len(x_refs) == 1:
        return x_refs[0][...]
    return jnp.where(row0 < N_P, x_refs[0][...], x_refs[1][...])


def _norm_mod_kernel(*refs, tr, n_parts):
    x_refs, (g_ref, sh_ref, sc_ref, o_ref) = refs[:n_parts], refs[n_parts:]
    row0 = pl.program_id(0) * tr
    x = _stream_tile(x_refs, row0)
    o_ref[...] = _norm_mod(x, g_ref[...], _mod_row(sh_ref, row0), _mod_row(sc_ref, row0)).astype(BF16)


def _norm_mod_call(x_parts, g, mod, layer, blk, tr=512):
    return pl.pallas_call(
        functools.partial(_norm_mod_kernel, tr=tr, n_parts=len(x_parts)),
        grid=(N_TOK // tr,),
        in_specs=_stream_specs(x_parts, tr, D_MODEL, 0) + [
            pl.BlockSpec((1, D_MODEL), lambda i: (0, 0)),
            _mod_spec(layer, blk, D_MODEL, None),
            _mod_spec(layer, blk + 1, D_MODEL, None)],
        out_specs=pl.BlockSpec((tr, D_MODEL), lambda i: (i, 0)),
        out_shape=jax.ShapeDtypeStruct((N_TOK, D_MODEL), BF16),
        compiler_params=_params("arbitrary"),
        name="norm_mod",
    )(*x_parts, g.reshape(1, D_MODEL), mod, mod)


CAST_ROWS = 128


def _proj_kernel(h_ref, w_ref, wo_ref, o_ref, wob_ref, wb_ref):
    @pl.when(pl.program_id(1) == 0)
    def _():
        wb_ref[...] = w_ref[...].astype(BF16)

    wob_ref[...] = wo_ref[...].astype(BF16)
    o_ref[...] = jnp.dot(h_ref[...], wb_ref[...], preferred_element_type=F32).astype(o_ref.dtype)


def _proj(h, w, w_out, tm=2048, tn=1024):
    m, k = h.shape
    n = w.shape[1]
    k2, n2 = w_out.shape
    n_i = m // tm
    n_cast = k2 // CAST_ROWS
    assert n_cast * CAST_ROWS == k2 and n_cast <= (n // tn) * n_i

    def cast_block(j, i):
        return (jnp.minimum(j * n_i + i, n_cast - 1), 0)

    return pl.pallas_call(
        _proj_kernel,
        grid=(n // tn, n_i),
        in_specs=[pl.BlockSpec((tm, k), lambda j, i: (i, 0)),
                  pl.BlockSpec((k, tn), lambda j, i: (0, j)),
                  pl.BlockSpec((CAST_ROWS, n2), cast_block)],
        out_specs=[pl.BlockSpec((tm, tn), lambda j, i: (i, j)),
                   pl.BlockSpec((CAST_ROWS, n2), cast_block)],
        out_shape=[jax.ShapeDtypeStruct((m, n), BF16),
                   jax.ShapeDtypeStruct((k2, n2), BF16)],
        scratch_shapes=[pltpu.VMEM((k, tn), BF16)],
        compiler_params=_params("arbitrary", "arbitrary"),
        name="proj",
    )(h, w, w_out)


def _shift_rows(p, prev_row, next_row, row0, seq_len):
    n = p.shape[0]
    ridx = lax.broadcasted_iota(jnp.int32, (n, 1), 0)
    pos = (row0 + ridx) & (seq_len - 1)
    p_prev = jnp.where(ridx == 0, prev_row, pltpu.roll(p, 1, 0))
    p_prev = jnp.where(pos == 0, 0.0, p_prev)
    p_next = jnp.where(ridx == n - 1, next_row, pltpu.roll(p, n - 1, 0))
    p_next = jnp.where(pos == seq_len - 1, 0.0, p_next)
    return p_prev, p_next


def _sconv_kernel(ab_ref, ac_ref, ax_ref, pc_ref, px_ref, nc_ref, nx_ref, w_ref, o_ref, *, tr):
    row0 = pl.program_id(0) * tr
    p = ac_ref[...].astype(F32) * ax_ref[...].astype(F32)
    prev_row = pc_ref[HALO - 1:HALO, :].astype(F32) * px_ref[HALO - 1:HALO, :].astype(F32)
    next_row = nc_ref[0:1, :].astype(F32) * nx_ref[0:1, :].astype(F32)
    p_prev, p_next = _shift_rows(p, prev_row, next_row, row0, _seq_len(row0))
    w = w_ref[...]
    conv = w[0:1, :] * p_prev + w[1:2, :] * p + w[2:3, :] * p_next
    o_ref[...] = (ab_ref[...].astype(F32) * conv).astype(BF16)


def _halo_specs(tr, width, col):
    per = tr // HALO
    last = N_TOK // HALO - 1
    prev = pl.BlockSpec((HALO, width), lambda i: (jnp.maximum(i * per - 1, 0), col))
    nxt = pl.BlockSpec((HALO, width), lambda i: (jnp.minimum((i + 1) * per, last), col))
    return prev, nxt


def _short_gated_conv(proj, conv_a, tr=512):
    pc, nc = _halo_specs(tr, W_A, 1)
    px, nx = _halo_specs(tr, W_A, 2)
    return pl.pallas_call(
        functools.partial(_sconv_kernel, tr=tr),
        grid=(N_TOK // tr,),
        in_specs=[pl.BlockSpec((tr, W_A), lambda i: (i, 0)),
                  pl.BlockSpec((tr, W_A), lambda i: (i, 1)),
                  pl.BlockSpec((tr, W_A), lambda i: (i, 2)),
                  pc, px, nc, nx,
                  pl.BlockSpec((SHORT_K, W_A), lambda i: (0, 0))],
        out_specs=pl.BlockSpec((tr, W_A), lambda i: (i, 0)),
        out_shape=jax.ShapeDtypeStruct((N_TOK, W_A), BF16),
        compiler_params=_params("arbitrary"),
        name="short_gated_conv",
    )(proj, proj, proj, proj, proj, proj, proj, conv_a)


ROWS = DEC_SEQ // GRID_W
NA_QROWS = 8
NA_KROWS = 16
NA_NQ = NA_QROWS * GRID_W
NA_NK = NA_KROWS * GRID_W
NA_STEPS = ROWS // NA_QROWS
QKV_COL0 = 3 * W_A // NA_HEAD_DIM
HEAD_SCALE = NA_HEAD_DIM ** -0.5
LOG2E = math.log2(math.e)
NA_CHUNK = 16
NT_DIMS = (((1,), (1,)), ((), ()))


def _head_rmsnorm(x, g):
    return x * lax.rsqrt(jnp.mean(x * x, axis=-1, keepdims=True) + EPS) * g


def _na_key_row0(j):
    return jnp.clip(j * NA_QROWS - NA_WIN_R // 2, 0, ROWS - NA_KROWS)


NA_SLOT_LO = NA_QROWS
NA_SLOTS = 2 * NA_KROWS


def _na_geometry(j):
    kr0 = min(max(j * NA_QROWS - NA_WIN_R // 2, 0), ROWS - NA_KROWS)
    per_a = []
    for a in range(NA_QROWS):
        r = j * NA_QROWS + a
        r0 = min(max(r - NA_WIN_R // 2, 0), ROWS - NA_WIN_R)
        per_a.append((kr0 - r + NA_WIN_R - 1 + NA_SLOT_LO, r0 - kr0))
    return kr0, per_a


def _na_bias_slots(rpb):
    n_dr, n_dc = rpb.shape[1], rpb.shape[2]
    out = jax.ShapeDtypeStruct((NA_HEADS, GRID_W, NA_SLOTS * GRID_W), F32)
    spec = pl.BlockSpec((None, GRID_W, NA_SLOTS * GRID_W), lambda h: (h, 0, 0))
    return pl.pallas_call(
        functools.partial(_na_slots_kernel, n_dr=n_dr, n_dc=n_dc),
        grid=(NA_HEADS,),
        in_specs=[pl.BlockSpec(memory_space=pltpu.SMEM)],
        out_specs=[spec, spec],
        out_shape=[out, out],
        compiler_params=_params("arbitrary"),
        name="na_bias_slots",
    )(rpb.astype(F32).reshape(NA_HEADS, n_dr * n_dc))


def _na_slots_kernel(rpb_ref, even_ref, odd_ref, *, n_dr, n_dc):
    h = pl.program_id(0)
    pair = 2 * GRID_W
    qc = lax.broadcasted_iota(jnp.int32, (GRID_W, pair), 0)
    lane = lax.broadcasted_iota(jnp.int32, (GRID_W, pair), 1)
    kc = lane & (GRID_W - 1)
    delta = kc - qc + NA_WIN_C - 1
    cs = jnp.clip(qc - NA_WIN_C // 2, 0, GRID_W - NA_WIN_C)
    col_ok = (kc >= cs) & (kc < cs + NA_WIN_C)
    upper = lax.broadcasted_iota(jnp.int32, (1, pair), 1) >= GRID_W
    for shift, ref in ((0, even_ref), (1, odd_ref)):
        for t in range(NA_SLOTS // 2):
            drs = [2 * t + shift + half - NA_SLOT_LO for half in (0, 1)]
            acc = jnp.full((GRID_W, pair), NEG_INF, F32)
            if any(0 <= dr < n_dr for dr in drs):
                for dc in range(n_dc):
                    lo, hi = [rpb_ref[h, dr * n_dc + dc] if 0 <= dr < n_dr else NEG_INF for dr in drs]
                    acc = jnp.where(delta == dc, jnp.where(upper, hi, lo), acc)
                acc = jnp.where(col_ok, acc, NEG_INF)
            ref[:, t * pair:(t + 1) * pair] = acc


def _na_build_bias(j_static, even_ref, odd_ref, bias_ref):
    _, per_a = _na_geometry(j_static)
    key_row = lax.broadcasted_iota(jnp.int32, (1, NA_NK), 1) // GRID_W
    for a, (m, lo) in enumerate(per_a):
        src = even_ref if m % 2 == 0 else odd_ref
        start = (m - m % 2) * GRID_W
        slab = src[:, start:start + NA_NK] * LOG2E
        ok = (key_row >= lo) & (key_row < lo + NA_WIN_R)
        bias_ref[a * GRID_W:(a + 1) * GRID_W, :] = jnp.where(ok, slab, NEG_INF)


def _with_ones_column(v):
    lane = lax.broadcasted_iota(jnp.int32, v.shape, 1)
    return jnp.concatenate([v.astype(BF16), jnp.where(lane == 0, 1.0, 0.0).astype(BF16)], axis=1)


def _na_kernel(q_ref, k_ref, v_ref, ck_ref, cv_ref, even_ref, odd_ref, qg_ref, kg_ref, o_ref,
               kn_ref, vx_ref, kall_ref, vall_ref, bias_ref, s_ref, p_ref):
    j = pl.program_id(2)
    hd = NA_HEAD_DIM

    @pl.when(j == 0)
    def _():
        kn_ref[...] = _head_rmsnorm(k_ref[...].astype(F32), kg_ref[...]).astype(BF16)
        vx_ref[...] = _with_ones_column(v_ref[...])
        kall_ref[NA_NK:, :] = ck_ref[...].astype(BF16)
        vall_ref[NA_NK:, :] = _with_ones_column(cv_ref[...])

    for j_static in (0, 1, NA_STEPS - 1):
        @pl.when(j == j_static)
        def _(j_static=j_static):
            _na_build_bias(j_static, even_ref, odd_ref, bias_ref)

    q = (_head_rmsnorm(q_ref[...].astype(F32), qg_ref[...]) * (HEAD_SCALE * LOG2E)).astype(BF16)
    tok0 = pl.multiple_of(_na_key_row0(j) * GRID_W, GRID_W)
    kall_ref[:NA_NK, :] = kn_ref[pl.ds(tok0, NA_NK), :]
    vall_ref[:NA_NK, :] = vx_ref[pl.ds(tok0, NA_NK), :]
    s_ref[...] = lax.dot_general(q, kall_ref[...], NT_DIMS, preferred_element_type=F32)
    for c in range(NA_NQ // NA_CHUNK):
        rows = slice(c * NA_CHUNK, (c + 1) * NA_CHUNK)
        s_loc = s_ref[rows, :NA_NK] + bias_ref[rows, :]
        s_ctx = s_ref[rows, NA_NK:]
        m = functools.reduce(jnp.maximum, [blk[:, t:t + hd] for blk in (s_loc, s_ctx)
                                           for t in range(0, blk.shape[1], hd)])
        m = jnp.max(m, axis=-1, keepdims=True)
        p_ref[rows, :NA_NK] = jnp.exp2(s_loc - m).astype(BF16)
        p_ref[rows, NA_NK:] = jnp.exp2(s_ctx - m).astype(BF16)
    o = jnp.dot(p_ref[...], vall_ref[...], preferred_element_type=F32)
    o_ref[...] = (o[:, :hd] / o[:, hd:hd + 1]).astype(BF16)


def _na_attention(proj, cache_k, cache_v, rpb, q_norm, k_norm):
    assert NA_STEPS >= 3 and all(_na_geometry(j)[1] == _na_geometry(1)[1] for j in range(1, NA_STEPS - 1))
    even, odd = _na_bias_slots(rpb)
    ck = cache_k.reshape(DEC_BATCH, PAST_LEN, W_B)
    cv = cache_v.reshape(DEC_BATCH, PAST_LEN, W_B)
    hd = NA_HEAD_DIM
    ctx = pl.BlockSpec((None, PAST_LEN, hd), lambda b, h, j: (b, 0, h))
    q_blk0 = N_P // NA_NQ
    kv_blk0 = N_P // DEC_SEQ
    slots = pl.BlockSpec((None, GRID_W, NA_SLOTS * GRID_W), lambda b, h, j: (h, 0, 0))
    return pl.pallas_call(
        _na_kernel,
        grid=(DEC_BATCH, NA_HEADS, NA_STEPS),
        in_specs=[
            pl.BlockSpec((NA_NQ, hd), lambda b, h, j: (q_blk0 + b * NA_STEPS + j, QKV_COL0 + h)),
            pl.BlockSpec((DEC_SEQ, hd), lambda b, h, j: (kv_blk0 + b, QKV_COL0 + NA_HEADS + h)),
            pl.BlockSpec((DEC_SEQ, hd), lambda b, h, j: (kv_blk0 + b, QKV_COL0 + 2 * NA_HEADS + h)),
            ctx, ctx,
            slots, slots,
            pl.BlockSpec((1, hd), lambda b, h, j: (0, 0)),
            pl.BlockSpec((1, hd), lambda b, h, j: (0, 0)),
        ],
        out_specs=pl.BlockSpec((NA_NQ, hd), lambda b, h, j: (b * NA_STEPS + j, h)),
        out_shape=jax.ShapeDtypeStruct((N_S, W_B), BF16),
        scratch_shapes=[pltpu.VMEM((DEC_SEQ, hd), BF16),
                        pltpu.VMEM((DEC_SEQ, 2 * hd), BF16),
                        pltpu.VMEM((NA_NK + PAST_LEN, hd), BF16),
                        pltpu.VMEM((NA_NK + PAST_LEN, 2 * hd), BF16),
                        pltpu.VMEM((NA_NQ, NA_NK), F32),
                        pltpu.VMEM((NA_NQ, NA_NK + PAST_LEN), F32),
                        pltpu.VMEM((NA_NQ, NA_NK + PAST_LEN), BF16)],
        compiler_params=_params("arbitrary", "arbitrary", "arbitrary"),
        name="na_attention",
    )(proj, proj, proj, ck, cv, even, odd, q_norm.reshape(1, hd), k_norm.reshape(1, hd))


def _ctx_kernel(q_ref, k_ref, v_ref, qg_ref, kg_ref, o_ref, kn_ref, vo_ref):
    vo_ref[...] = v_ref[...].astype(F32)
    for h in range(NA_HEADS):
        cols = slice(h * NA_HEAD_DIM, (h + 1) * NA_HEAD_DIM)
        q = (_head_rmsnorm(q_ref[:, cols].astype(F32), qg_ref[...]) * HEAD_SCALE).astype(BF16)
        kn = _head_rmsnorm(k_ref[:, cols].astype(F32), kg_ref[...])
        kn_ref[:, cols] = kn
        s = lax.dot_general(q, kn.astype(BF16), NT_DIMS, preferred_element_type=F32)
        p = jnp.exp(s - jnp.max(s, axis=-1, keepdims=True))
        denom = jnp.sum(p, axis=-1, keepdims=True)
        o = jnp.dot(p.astype(BF16), v_ref[:, cols], preferred_element_type=F32)
        o_ref[:, cols] = (o / denom).astype(BF16)


def _ctx_attention(proj, q_norm, k_norm):
    hd = NA_HEAD_DIM
    col0 = 3 * W_A // W_B
    ospec = pl.BlockSpec((SEQ, W_B), lambda b: (b, 0))
    return pl.pallas_call(
        _ctx_kernel,
        grid=(BATCH,),
        in_specs=[
            pl.BlockSpec((SEQ, W_B), lambda b: (b, col0)),
            pl.BlockSpec((SEQ, W_B), lambda b: (b, col0 + 1)),
            pl.BlockSpec((SEQ, W_B), lambda b: (b, col0 + 2)),
            pl.BlockSpec((1, hd), lambda b: (0, 0)),
            pl.BlockSpec((1, hd), lambda b: (0, 0)),
        ],
        out_specs=[ospec, ospec, ospec],
        out_shape=[jax.ShapeDtypeStruct((N_P, W_B), BF16),
                   jax.ShapeDtypeStruct((N_P, W_B), F32),
                   jax.ShapeDtypeStruct((N_P, W_B), F32)],
        compiler_params=_params("arbitrary"),
        name="ctx_attention",
    )(proj, proj, proj, q_norm.reshape(1, hd), k_norm.reshape(1, hd))


def _out_proj_kernel(*refs, tm, ka, n_ya, n_yb, n_x):
    ya_refs, refs = refs[:n_ya], refs[n_ya:]
    yb_refs, refs = refs[:n_yb], refs[n_yb:]
    w_ref, refs = refs[0], refs[1:]
    x_refs, refs = refs[:n_x], refs[n_x:]
    gate_ref, g_ref, sh_ref, sc_ref, xo_ref, ho_ref = refs
    row0 = pl.program_id(0) * tm
    o = (jnp.dot(_stream_tile(ya_refs, row0), w_ref[0:ka, :], preferred_element_type=F32)
         + jnp.dot(_stream_tile(yb_refs, row0), w_ref[ka:, :], preferred_element_type=F32))
    x_new = _stream_tile(x_refs, row0) + _mod_row(gate_ref, row0) * o
    xo_ref[...] = x_new
    ho_ref[...] = _norm_mod(x_new, g_ref[...], _mod_row(sh_ref, row0), _mod_row(sc_ref, row0)).astype(BF16)


def _out_proj(ya_parts, yb_parts, w_bf16, x_parts, mod, layer, g_ffn, tm=512):
    ka, kb = ya_parts[0].shape[1], yb_parts[0].shape[1]
    return pl.pallas_call(
        functools.partial(_out_proj_kernel, tm=tm, ka=ka, n_ya=len(ya_parts), n_yb=len(yb_parts),
                          n_x=len(x_parts)),
        grid=(N_TOK // tm,),
        in_specs=_stream_specs(ya_parts, tm, ka, 0)
        + _stream_specs(yb_parts, tm, kb, 0)
        + [pl.BlockSpec((ka + kb, D_MODEL), lambda i: (0, 0))]
        + _stream_specs(x_parts, tm, D_MODEL, 0) + [
            _mod_spec(layer, 2, D_MODEL, None),
            pl.BlockSpec((1, D_MODEL), lambda i: (0, 0)),
            _mod_spec(layer, 3, D_MODEL, None),
            _mod_spec(layer, 4, D_MODEL, None)],
        out_specs=[pl.BlockSpec((tm, D_MODEL), lambda i: (i, 0)),
                   pl.BlockSpec((tm, D_MODEL), lambda i: (i, 0))],
        out_shape=[jax.ShapeDtypeStruct((N_TOK, D_MODEL), F32),
                   jax.ShapeDtypeStruct((N_TOK, D_MODEL), BF16)],
        compiler_params=_params("arbitrary"),
        name="out_proj",
    )(*ya_parts, *yb_parts, w_bf16, *x_parts, mod, g_ffn.reshape(1, D_MODEL), mod, mod)


GELU_C = math.sqrt(2.0 / math.pi)


def _gelu_tanh(x):
    return 0.5 * x * (1.0 + jnp.tanh(GELU_C * (x + 0.044715 * (x * x * x))))


def _ffn_in_kernel(h_ref, hp_ref, hn_ref, wa_ref, wg_ref, cw_ref, wo_ref, o_ref, wob_ref, wab_ref, wgb_ref,
                   *, tm, tn):
    i = pl.program_id(1)

    @pl.when(i == 0)
    def _():
        wab_ref[...] = wa_ref[...].astype(BF16)
        wgb_ref[...] = wg_ref[...].astype(BF16)

    wob_ref[...] = wo_ref[...].astype(BF16)

    row0 = i * tm
    seq_len = _seq_len(row0)
    h = h_ref[...]
    h_ext = jnp.concatenate([hp_ref[...], h, hn_ref[...]], axis=0)
    n = tm + 2 * HALO
    pos = (row0 + lax.broadcasted_iota(jnp.int32, (tm, 1), 0)) & (seq_len - 1)
    first = pos == 0
    last = pos == seq_len - 1
    a_ext = jnp.dot(h_ext, wab_ref[...], preferred_element_type=F32)
    g = jnp.dot(h, wgb_ref[...], preferred_element_type=F32)
    a_mid = a_ext[HALO:HALO + tm, :]
    a_prev = jnp.where(first, 0.0, pltpu.roll(a_ext, 1, 0)[HALO:HALO + tm, :])
    a_next = jnp.where(last, 0.0, pltpu.roll(a_ext, n - 1, 0)[HALO:HALO + tm, :])
    cw = cw_ref[...]
    conv = cw[0:1, :] * a_prev + cw[1:2, :] * a_mid + cw[2:3, :] * a_next
    o_ref[...] = (_gelu_tanh(conv) * g).astype(BF16)


def _ffn_in(h, w_in, conv, w_out, layer, tm=1024, tn=512):
    per = tm // HALO
    last = N_TOK // HALO - 1
    ng = D_FF // tn
    n_i = N_TOK // tm
    per_j = D_FF // ng // CAST_ROWS
    assert per_j <= n_i and per_j * CAST_ROWS * ng == D_FF

    def cast_block(j, i):
        return j * per_j + jnp.minimum(i, per_j - 1)

    return pl.pallas_call(
        functools.partial(_ffn_in_kernel, tm=tm, tn=tn),
        grid=(ng, n_i),
        in_specs=[pl.BlockSpec((tm, D_MODEL), lambda j, i: (i, 0)),
                  pl.BlockSpec((HALO, D_MODEL), lambda j, i: (jnp.maximum(i * per - 1, 0), 0)),
                  pl.BlockSpec((HALO, D_MODEL), lambda j, i: (jnp.minimum((i + 1) * per, last), 0)),
                  pl.BlockSpec((None, D_MODEL, tn), lambda j, i: (layer, 0, j)),
                  pl.BlockSpec((None, D_MODEL, tn), lambda j, i: (layer, 0, ng + j)),
                  pl.BlockSpec((None, SHORT_K, tn), lambda j, i: (layer, 0, j)),
                  pl.BlockSpec((None, CAST_ROWS, D_MODEL), lambda j, i: (layer, cast_block(j, i), 0))],
        out_specs=[pl.BlockSpec((tm, tn), lambda j, i: (i, j)),
                   pl.BlockSpec((CAST_ROWS, D_MODEL), lambda j, i: (cast_block(j, i), 0))],
        out_shape=[jax.ShapeDtypeStruct((N_TOK, D_FF), BF16),
                   jax.ShapeDtypeStruct((D_FF, D_MODEL), BF16)],
        scratch_shapes=[pltpu.VMEM((D_MODEL, tn), BF16), pltpu.VMEM((D_MODEL, tn), BF16)],
        compiler_params=_params("arbitrary", "arbitrary"),
        name="ffn_in",
    )(h, h, h, w_in, w_in, conv, w_out)


def _ffn_out_kernel(u_ref, w_ref, x_ref, gate_ref, o_ref, *, tm, row_start):
    row0 = row_start + pl.program_id(1) * tm
    o = jnp.dot(u_ref[...], w_ref[...], preferred_element_type=F32)
    o_ref[...] = x_ref[...] + _mod_row(gate_ref, row0) * o


def _ffn_out(u, w_bf16, x, mod, layer, row_start=0, n_rows=N_TOK, tm=512, tn=1024):
    t0 = row_start // tm
    return pl.pallas_call(
        functools.partial(_ffn_out_kernel, tm=tm, row_start=row_start),
        grid=(D_MODEL // tn, n_rows // tm),
        in_specs=[pl.BlockSpec((tm, D_FF), lambda j, i: (t0 + i, 0)),
                  pl.BlockSpec((D_FF, tn), lambda j, i: (0, j)),
                  pl.BlockSpec((tm, tn), lambda j, i: (t0 + i, j)),
                  _mod_spec(layer, 5, tn, 0)],
        out_specs=pl.BlockSpec((tm, tn), lambda j, i: (i, j)),
        out_shape=jax.ShapeDtypeStruct((n_rows, D_MODEL), F32),
        compiler_params=_params("arbitrary", "arbitrary"),
        name="ffn_out",
    )(u, w_bf16, x, mod)


CONF_TR = 256
CONF_PAD = CONF_K // 2
CONF_RB = 128
CONF_CB = 128
CONF_LN_ROWS = 16


def _conformer_tile(ca_ref, cg_ref, pa_ref, pg_ref, na_ref, ng_ref, w_ref, b_ref, lg_ref, lb_ref, o_ref,
                    u_ref, v_ref, tile):
    tr = CONF_TR
    row0 = tile * tr
    seq_len = _seq_len(row0)
    pos0 = row0 & (seq_len - 1)

    def glu(a_ref, g_ref):
        return a_ref[...].astype(F32) * jax.nn.sigmoid(g_ref[...].astype(F32))

    u_ref[0:HALO, :] = jnp.where(pos0 == 0, 0.0, glu(pa_ref, pg_ref))
    u_ref[HALO:HALO + tr, :] = glu(ca_ref, cg_ref)
    u_ref[HALO + tr:, :] = jnp.where(pos0 + tr == seq_len, 0.0, glu(na_ref, ng_ref))
    base = HALO - CONF_PAD
    n_in = CONF_RB + 2 * HALO
    for cb in range(W_C // CONF_CB):
        cols = slice(cb * CONF_CB, (cb + 1) * CONF_CB)
        w = w_ref[:, cols]
        for rb in range(tr // CONF_RB):
            u_blk = u_ref[rb * CONF_RB:rb * CONF_RB + n_in, cols]
            acc = jnp.zeros((CONF_RB, CONF_CB), F32) + b_ref[:, cols]
            for r in range(8):
                u_r = pltpu.roll(u_blk, n_in - (base + r), 0)
                for a in range(len(range(r, CONF_K, 8))):
                    k = 8 * a + r
                    acc = acc + w[k:k + 1, :] * u_r[8 * a:8 * a + CONF_RB, :]
            v_ref[rb * CONF_RB:(rb + 1) * CONF_RB, cols] = acc
    for c in range(tr // CONF_LN_ROWS):
        rows = slice(c * CONF_LN_ROWS, (c + 1) * CONF_LN_ROWS)
        v = v_ref[rows, :]
        mu = jnp.mean(v, axis=-1, keepdims=True)
        d = v - mu
        var = jnp.mean(d * d, axis=-1, keepdims=True)
        y = d * lax.rsqrt(var + EPS) * lg_ref[...] + lb_ref[...]
        o_ref[rows, :] = (y * jax.nn.sigmoid(y)).astype(BF16)


def _conformer_kernel(*refs):
    _conformer_tile(*refs, pl.program_id(0))


def _conformer(proj, dw, dw_b, ln_g, ln_b):
    pa, na = _halo_specs(CONF_TR, W_C, 0)
    pg, ng = _halo_specs(CONF_TR, W_C, 1)
    row = pl.BlockSpec((1, W_C), lambda i: (0, 0))
    return pl.pallas_call(
        _conformer_kernel,
        grid=(N_TOK // CONF_TR,),
        in_specs=[pl.BlockSpec((CONF_TR, W_C), lambda i: (i, 0)),
                  pl.BlockSpec((CONF_TR, W_C), lambda i: (i, 1)),
                  pa, pg, na, ng,
                  pl.BlockSpec((CONF_K, W_C), lambda i: (0, 0)), row, row, row],
        out_specs=pl.BlockSpec((CONF_TR, W_C), lambda i: (i, 0)),
        out_shape=jax.ShapeDtypeStruct((N_TOK, W_C), BF16),
        scratch_shapes=[pltpu.VMEM((CONF_TR + 2 * HALO, W_C), F32), pltpu.VMEM((CONF_TR, W_C), F32)],
        compiler_params=_params("arbitrary"),
        name="conformer",
    )(proj, proj, proj, proj, proj, proj, dw, dw_b.reshape(1, W_C), ln_g.reshape(1, W_C), ln_b.reshape(1, W_C))


HY_COL0 = 2 * W_C // W_D


def _hy_short_kernel(*refs, tr):
    (x0_ref, x1_ref, vv_ref, p0_ref, p1_ref, pv_ref, n0_ref, n1_ref, nv_ref,
     w0_ref, w1_ref, wv_ref, b0_ref, b1_ref, bv_ref, x0o_ref, wo_ref) = refs
    row0 = pl.program_id(0) * tr
    seq_len = _seq_len(row0)

    def conv(c_ref, p_ref, n_ref, w_ref, b_ref):
        p = c_ref[...].astype(F32)
        p_prev, p_next = _shift_rows(p, p_ref[HALO - 1:HALO, :].astype(F32), n_ref[0:1, :].astype(F32),
                                     row0, seq_len)
        w = w_ref[...]
        return w[0:1, :] * p_prev + w[1:2, :] * p + w[2:3, :] * p_next + b_ref[...]

    x0o_ref[...] = conv(x0_ref, p0_ref, n0_ref, w0_ref, b0_ref).astype(BF16)
    x1 = conv(x1_ref, p1_ref, n1_ref, w1_ref, b1_ref)
    vv = conv(vv_ref, pv_ref, nv_ref, wv_ref, bv_ref)
    wo_ref[...] = (vv * x1).astype(BF16)


def _hy_short(proj, hy_short, hy_short_b, tr=512):
    main = [pl.BlockSpec((tr, W_D), functools.partial(lambda i, c: (i, c), c=HY_COL0 + c)) for c in range(3)]
    halos = [_halo_specs(tr, W_D, HY_COL0 + c) for c in range(3)]
    wspec = [pl.BlockSpec((SHORT_K, W_D), functools.partial(lambda i, c: (0, c), c=c)) for c in range(3)]
    bspec = [pl.BlockSpec((1, W_D), functools.partial(lambda i, c: (0, c), c=c)) for c in range(3)]
    out = jax.ShapeDtypeStruct((N_TOK, W_D), BF16)
    ospec = pl.BlockSpec((tr, W_D), lambda i: (i, 0))
    b2 = hy_short_b.reshape(1, 3 * W_D)
    return pl.pallas_call(
        functools.partial(_hy_short_kernel, tr=tr),
        grid=(N_TOK // tr,),
        in_specs=main + [h[0] for h in halos] + [h[1] for h in halos] + wspec + bspec,
        out_specs=[ospec, ospec],
        out_shape=[out, out],
        compiler_params=_params("arbitrary"),
        name="hy_short",
    )(*([proj] * 9), hy_short, hy_short, hy_short, b2, b2, b2)


HIGHEST = lax.Precision.HIGHEST
HY_EMB_PAD = 128
W2_D = 2 * W_D


def _dft_gen_kernel(ct_ref, st_ref, cr_ref, sr_ref, c_ref, s_ref):
    t = pl.program_id(0)
    cr = cr_ref[pl.ds(t, 1), :]
    sr = sr_ref[pl.ds(t, 1), :]
    ct = ct_ref[...]
    st = st_ref[...]
    c_ref[...] = (cr * ct - sr * st).astype(BF16)
    s_ref[...] = (sr * ct + cr * st).astype(BF16)


def _dft_matrices(L):
    tr = min(L, 256)
    nt = L // tr
    nt_pad = -(-nt // 8) * 8
    n = jnp.arange(L, dtype=jnp.int32)[None, :]

    def tables(k):
        ang = ((k[:, None] * n) % (2 * L)).astype(F32) * (math.pi / L)
        return jnp.cos(ang), jnp.sin(ang)

    ct, st = tables(jnp.arange(tr, dtype=jnp.int32))
    cr, sr = tables(jnp.arange(nt_pad, dtype=jnp.int32) * tr)
    tile = pl.BlockSpec((tr, L), lambda t: (0, 0))
    rows = pl.BlockSpec((nt_pad, L), lambda t: (0, 0))
    out = pl.BlockSpec((tr, L), lambda t: (t, 0))
    return pl.pallas_call(
        _dft_gen_kernel,
        grid=(nt,),
        in_specs=[tile, tile, rows, rows],
        out_specs=[out, out],
        out_shape=[jax.ShapeDtypeStruct((L, L), BF16)] * 2,
        compiler_params=_params("arbitrary"),
        name="dft_gen",
    )(ct, st, cr, sr)


def _hy_features(L):
    t = jnp.linspace(0.0, 1.0, L, dtype=F32)[:, None]
    bands = (HY_EMB - 1) // 2
    ang = 2 * math.pi * jnp.arange(L, dtype=F32)[:, None] / L
    freqs = jnp.linspace(1e-4, bands - 1, bands, dtype=F32)[None, :]
    z = jnp.concatenate([t, jnp.cos(freqs * ang), -jnp.sin(freqs * ang)], axis=-1)
    return jnp.pad(z, ((0, 0), (0, HY_EMB_PAD - HY_EMB))), t


def _hy_filter_kernel(z_ref, t_ref, w1_ref, b1_ref, f1_ref, w2_ref, b2_ref, f2_ref, w3_ref, dl_ref,
                      fs_ref, fd_ref, hn_ref, *, tr):
    i = pl.program_id(0)
    hid = jnp.sin(f1_ref[...] * (jnp.dot(z_ref[...], w1_ref[...], precision=HIGHEST,
                                         preferred_element_type=F32) + b1_ref[...]))
    hid = jnp.sin(f2_ref[...] * (jnp.dot(hid, w2_ref[...], precision=HIGHEST,
                                         preferred_element_type=F32) + b2_ref[...]))
    hf = jnp.dot(hid.astype(BF16), w3_ref[...].astype(BF16), preferred_element_type=F32)
    decay = jnp.exp(-t_ref[...] * jnp.abs(dl_ref[...]))
    ridx = i * tr + lax.broadcasted_iota(jnp.int32, (tr, 1), 0)
    fwd = hf[:, :W_D] * decay
    bwd = jnp.where(ridx == 0, 0.0, hf[:, W_D:] * decay)
    fsum = fwd + bwd
    fdif = bwd - fwd
    fs_ref[...] = fsum.astype(BF16)
    fd_ref[...] = fdif.astype(BF16)
    sgn = (1 - (ridx & 2)).astype(F32)
    even = (ridx & 1) == 0
    hc_part = jnp.sum(jnp.where(even, sgn, 0.0) * fsum, axis=0, keepdims=True)
    hs_part = jnp.sum(jnp.where(even, 0.0, sgn) * fdif, axis=0, keepdims=True)

    @pl.when(i == 0)
    def _():
        hn_ref[...] = jnp.zeros_like(hn_ref)

    hn_ref[0:1, :] += hc_part
    hn_ref[1:2, :] += hs_part


def _hy_filter(L, w1, b1, f1, w2, b2, f2, w3):
    tr = min(L, 512)
    z, t = _hy_features(L)
    max_decay = math.log(HY_TARGET) / HY_FAST_PCT
    min_decay = math.log(HY_TARGET) / HY_SLOW_PCT
    deltas = jnp.linspace(min_decay, max_decay, W_D, dtype=F32)[None, :]
    w1p = jnp.pad(w1, ((0, HY_EMB_PAD - HY_EMB), (0, 0)))
    full = lambda shape: pl.BlockSpec(shape, lambda i: (0, 0))
    return pl.pallas_call(
        functools.partial(_hy_filter_kernel, tr=tr),
        grid=(L // tr,),
        in_specs=[pl.BlockSpec((tr, HY_EMB_PAD), lambda i: (i, 0)),
                  pl.BlockSpec((tr, 1), lambda i: (i, 0)),
                  full((HY_EMB_PAD, HY_HIDDEN)), full((1, HY_HIDDEN)), full((1, HY_HIDDEN)),
                  full((HY_HIDDEN, HY_HIDDEN)), full((1, HY_HIDDEN)), full((1, HY_HIDDEN)),
                  full((HY_HIDDEN, 2 * W_D)), full((1, W_D))],
        out_specs=[pl.BlockSpec((tr, W_D), lambda i: (i, 0)),
                   pl.BlockSpec((tr, W_D), lambda i: (i, 0)),
                   pl.BlockSpec((8, W_D), lambda i: (0, 0))],
        out_shape=[jax.ShapeDtypeStruct((L, W_D), BF16),
                   jax.ShapeDtypeStruct((L, W_D), BF16),
                   jax.ShapeDtypeStruct((8, W_D), F32)],
        compiler_params=_params("arbitrary"),
        name="hy_filter",
    )(z, t, w1p, b1.reshape(1, -1), f1.reshape(1, -1), w2, b2.reshape(1, -1), f2.reshape(1, -1), w3, deltas)


def _twiddles(L):
    k = jnp.arange(L // 2, dtype=F32)[:, None] * (math.pi / L)
    return jnp.cos(k), jnp.sin(k)


def _butterfly(c_ref, s_ref, x2, twc, tws):
    xc2 = jnp.dot(c_ref[...], x2, preferred_element_type=F32)
    xs2 = jnp.dot(s_ref[...], x2, preferred_element_type=F32)
    ec, oc = xc2[:, :W_D], xc2[:, W_D:]
    es, os_ = xs2[:, :W_D], xs2[:, W_D:]
    tc = twc * oc - tws * os_
    ts = twc * os_ + tws * oc
    return ec + tc, es + ts, ec - tc, es - ts


def _hy_spec_kernel(c_ref, s_ref, fs_ref, fd_ref, twc_ref, tws_ref, hac_ref, has_ref, hrc_ref, hrs_ref):
    twc, tws = twc_ref[...], tws_ref[...]
    ac, _, bc, _ = _butterfly(c_ref, s_ref, fs_ref[...], twc, tws)
    _, as_, _, bs = _butterfly(c_ref, s_ref, fd_ref[...], twc, tws)
    hac_ref[...] = ac
    has_ref[...] = as_
    hrc_ref[...] = bc
    hrs_ref[...] = -bs


def _hy_spectrum(cmat, smat, fs2, fd2, twc, tws, tk):
    M = cmat.shape[0]
    tile = pl.BlockSpec((tk, M), lambda i: (i, 0))
    full = pl.BlockSpec((M, W2_D), lambda i: (0, 0), pipeline_mode=pl.Buffered(1))
    col = pl.BlockSpec((tk, 1), lambda i: (i, 0))
    out = pl.BlockSpec((tk, W_D), lambda i: (i, 0))
    return pl.pallas_call(
        _hy_spec_kernel,
        grid=(M // tk,),
        in_specs=[tile, tile, full, full, col, col],
        out_specs=[out] * 4,
        out_shape=[jax.ShapeDtypeStruct((M, W_D), F32)] * 4,
        compiler_params=_params("arbitrary"),
        name="hy_spectrum",
    )(cmat, smat, fs2, fd2, twc, tws)


def _hy_fwd_kernel(c_ref, s_ref, w_ref, hac_ref, has_ref, hrc_ref, hrs_ref, twc_ref, tws_ref, hn_ref,
                   vc_ref, vs_ref, yn_ref, *, L, tk):
    kt = pl.program_id(1)
    w2 = w_ref[...]
    twc, tws = twc_ref[...], tws_ref[...]
    ac, as_, bc, bs = _butterfly(c_ref, s_ref, w2, twc, tws)
    hac, has, hrc, hrs = hac_ref[...], has_ref[...], hrc_ref[...], hrs_ref[...]
    yac = ac * hac + as_ * has
    yai = ac * has - as_ * hac
    pc = bc * hrc - bs * hrs
    pi = -(bc * hrs + bs * hrc)
    dc, di = yac - pc, yai - pi
    kidx = kt * tk + lax.broadcasted_iota(jnp.int32, (tk, 1), 0)
    om = jnp.where(kidx == 0, 0.5 / L, 1.0 / L)
    vc_ref[...] = (om * jnp.concatenate([yac + pc, twc * dc - tws * di], axis=1)).astype(BF16)
    vs_ref[...] = ((-1.0 / L) * jnp.concatenate([yai + pi, twc * di + tws * dc], axis=1)).astype(BF16)

    @pl.when(kt == 0)
    def _():
        m = lax.broadcasted_iota(jnp.int32, (L // 2, 1), 0)
        sgn = (1 - 2 * (m & 1)).astype(F32)
        xn2 = jnp.sum(w2.astype(F32) * sgn, axis=0, keepdims=True)
        en, on = xn2[:, :W_D], xn2[:, W_D:]
        hc, hs = hn_ref[0:1, :], hn_ref[1:2, :]
        yn = jnp.concatenate([en * hc + on * hs, on * hc - en * hs], axis=1) * (1.0 / L)
        yn_ref[...] = jnp.broadcast_to(yn, yn_ref.shape)


def _hy_forward(cmat, smat, w2, h4, twc, tws, hn, L, nb, blk0, tk):
    M = L // 2
    tile = pl.BlockSpec((tk, M), lambda b, k: (k, 0))
    htile = pl.BlockSpec((tk, W_D), lambda b, k: (k, 0))
    col = pl.BlockSpec((tk, 1), lambda b, k: (k, 0))
    out = pl.BlockSpec((tk, W2_D), lambda b, k: (b * (M // tk) + k, 0))
    return pl.pallas_call(
        functools.partial(_hy_fwd_kernel, L=L, tk=tk),
        grid=(nb, M // tk),
        in_specs=[tile, tile, pl.BlockSpec((M, W2_D), lambda b, k: (blk0 + b, 0)),
                  htile, htile, htile, htile, col, col,
                  pl.BlockSpec((8, W_D), lambda b, k: (0, 0))],
        out_specs=[out, out, pl.BlockSpec((None, 8, W2_D), lambda b, k: (b, 0, 0))],
        out_shape=[jax.ShapeDtypeStruct((nb * M, W2_D), BF16),
                   jax.ShapeDtypeStruct((nb * M, W2_D), BF16),
                   jax.ShapeDtypeStruct((nb, 8, W2_D), F32)],
        compiler_params=_params("arbitrary", "arbitrary"),
        name="hy_forward",
    )(cmat, smat, w2, *h4, twc, tws, hn)


def _hy_inv_kernel(c_ref, s_ref, vc_ref, vs_ref, yn_ref, x0_ref, w_ref, bias_ref, z_ref, *, tt):
    ti = pl.program_id(1)
    y = (jnp.dot(c_ref[...], vc_ref[...], preferred_element_type=F32)
         + jnp.dot(s_ref[...], vs_ref[...], preferred_element_type=F32))
    m = ti * tt + lax.broadcasted_iota(jnp.int32, (tt, 1), 0)
    sgn = (1 - 2 * (m & 1)).astype(F32)
    y = y + sgn * yn_ref[0:1, :] + w_ref[...].astype(F32) * bias_ref[...]
    z_ref[...] = (x0_ref[...].astype(F32) * y).astype(BF16)


def _hy_inverse(cmat, smat, vc, vs, yn, x02, w2, bias, L, nb, blk0, tt):
    M = L // 2
    per = M // tt
    tile = pl.BlockSpec((tt, M), lambda b, t: (t, 0))
    seq = pl.BlockSpec((M, W2_D), lambda b, t: (b, 0))
    rows_in = pl.BlockSpec((tt, W2_D), lambda b, t: ((blk0 + b) * per + t, 0))
    return pl.pallas_call(
        functools.partial(_hy_inv_kernel, tt=tt),
        grid=(nb, per),
        in_specs=[tile, tile, seq, seq,
                  pl.BlockSpec((None, 8, W2_D), lambda b, t: (b, 0, 0)),
                  rows_in, rows_in,
                  pl.BlockSpec((1, W2_D), lambda b, t: (0, 0))],
        out_specs=pl.BlockSpec((tt, W2_D), lambda b, t: (b * per + t, 0)),
        out_shape=jax.ShapeDtypeStruct((nb * M, W2_D), BF16),
        compiler_params=_params("arbitrary", "arbitrary"),
        name="hy_inverse",
    )(cmat, smat, vc, vs, yn, x02, w2, jnp.concatenate([bias, bias]).reshape(1, W2_D))


def _hyena_long_conv(x0, w, L, nb, blk0, filt_params, bias):
    M = L // 2
    tk = min(M, 256)
    cmat, smat = _dft_matrices(M)
    twc, tws = _twiddles(L)
    fsum, fdif, hn = _hy_filter(L, *filt_params)
    h4 = _hy_spectrum(cmat, smat, fsum.reshape(M, W2_D), fdif.reshape(M, W2_D), twc, tws, tk)
    w2 = w.reshape(-1, W2_D)
    x02 = x0.reshape(-1, W2_D)
    vc, vs, yn = _hy_forward(cmat, smat, w2, h4, twc, tws, hn, L, nb, blk0, tk)
    z2 = _hy_inverse(cmat, smat, vc, vs, yn, x02, w2, bias, L, nb, blk0, tk)
    return z2.reshape(nb * L, W_D)


def kernel(x_prompt, x_sample, cache_k, cache_v, c, c_ctx, ada_w, ada_b, norm_mix, norm_ffn,
           e_w_in, e_conv_a, e_q_norm, e_k_norm, e_rpb, e_w_out,
           o_w_in, o_conf_dw, o_conf_dw_b, o_conf_ln_g, o_conf_ln_b, o_hy_short, o_hy_short_b,
           o_hy_w1, o_hy_b1, o_hy_f1, o_hy_w2, o_hy_b2, o_hy_f2, o_hy_w3, o_hy_bias, o_w_out,
           ffn_in, ffn_conv, ffn_out):
    x_parts = (x_prompt.reshape(N_P, D_MODEL), x_sample.reshape(N_S, D_MODEL))
    cvec = jnp.concatenate([c_ctx[None, :], c, jnp.zeros((N_SEG_PAD - 1 - DEC_BATCH, D_MODEL), F32)], axis=0)
    mod = _adaln(cvec, ada_w, ada_b)
    ks_new, vs_new = [], []
    for layer in range(DEPTH):
        j = layer // 2
        last = layer == DEPTH - 1
        h = _norm_mod_call(x_parts, norm_mix[layer], mod, layer, 0)
        if layer % 2 == 0:
            proj, w_out = _proj(h, e_w_in[j], e_w_out[j])
            ya = (_short_gated_conv(proj, e_conv_a[j]),)
            yb_p, k_p, v_p = _ctx_attention(proj, e_q_norm[j], e_k_norm[j])
            yb_s = _na_attention(proj, cache_k[:, j], cache_v[:, j], e_rpb[j], e_q_norm[j], e_k_norm[j])
            yb = (yb_p, yb_s)
            ks_new.append(k_p.reshape(BATCH, SEQ, NA_HEADS, NA_HEAD_DIM))
            vs_new.append(v_p.reshape(BATCH, SEQ, NA_HEADS, NA_HEAD_DIM))
        else:
            proj, w_out = _proj(h, o_w_in[j], o_w_out[j])
            ya = (_conformer(proj, o_conf_dw[j], o_conf_dw_b[j], o_conf_ln_g[j], o_conf_ln_b[j]),)
            x0, w = _hy_short(proj, o_hy_short[j], o_hy_short_b[j])
            fp = (o_hy_w1[j], o_hy_b1[j], o_hy_f1[j], o_hy_w2[j], o_hy_b2[j], o_hy_f2[j], o_hy_w3[j])
            z_p = _hyena_long_conv(x0, w, SEQ, BATCH, 0, fp, o_hy_bias[j])
            z_s = _hyena_long_conv(x0, w, DEC_SEQ, DEC_BATCH, N_P // DEC_SEQ, fp, o_hy_bias[j])
            yb = (z_p, z_s)
        x, h_ffn = _out_proj(ya, yb, w_out, x_parts, mod, layer, norm_ffn[layer])
        u, w_ffn_out = _ffn_in(h_ffn, ffn_in, ffn_conv, ffn_out, layer)
        if last:
            x_parts = (_ffn_out(u, w_ffn_out, x, mod, layer, 0, N_P),
                       _ffn_out(u, w_ffn_out, x, mod, layer, N_P, N_S))
        else:
            x_parts = (_ffn_out(u, w_ffn_out, x, mod, layer),)
    xp = x_parts[0].reshape(BATCH, SEQ, D_MODEL)
    xs = x_parts[1].reshape(DEC_BATCH, DEC_SEQ, D_MODEL)
    return (xp, xs, jnp.stack(ks_new, axis=1), jnp.stack(vs_new, axis=1))
```

```python
import functools
import math

import jax
import jax.numpy as jnp
from jax import lax
from jax.experimental import pallas as pl
from jax.experimental.pallas import tpu as pltpu

D_MODEL = 2048
BATCH = 16
SEQ = 256
DEPTH = 2
DEC_BATCH = 2
DEC_SEQ = 4096
PAST_LEN = 512
GRID_W = 64
W_A = 1024
NA_HEADS = 8
NA_HEAD_DIM = 128
W_B = NA_HEADS * NA_HEAD_DIM
W_C = 1024
W_D = 1024
NA_WIN_R = 8
NA_WIN_C = 16
SHORT_K = 3
CONF_K = 31
D_FF = 5632
HY_EMB = 33
HY_HIDDEN = 64
HY_FAST_PCT = 0.3
HY_SLOW_PCT = 1.5
HY_TARGET = 1e-2
EPS = 1e-6
NEG_INF = -1e30

N_P = BATCH * SEQ
N_S = DEC_BATCH * DEC_SEQ
N_TOK = N_P + N_S
SEG = DEC_SEQ
N_SEG_PAD = 8
N_MOD = 6 * D_MODEL
HALO = 16
LANES = 128
VMEM_LIMIT_BYTES = 60 * 1024 * 1024
BF16 = jnp.bfloat16
F32 = jnp.float32


def _params(*sem):
    return pltpu.CompilerParams(dimension_semantics=sem, vmem_limit_bytes=VMEM_LIMIT_BYTES)


def _seq_len(row0):
    return jnp.where(row0 < N_P, SEQ, DEC_SEQ)


def _mod_row(mod_ref, row0):
    return mod_ref[pl.ds(row0 // SEG, 1), :]


def _mod_spec(layer, blk, tn, index_pos):
    per = D_MODEL // tn

    def imap(*idx):
        j = idx[index_pos] if index_pos is not None else 0
        return (layer, 0, blk * per + j)

    return pl.BlockSpec((None, N_SEG_PAD, tn), imap)


def _adaln_kernel(c_ref, w_ref, b_ref, o_ref):
    c = c_ref[...]
    s = (c * jax.nn.sigmoid(c)).astype(BF16)
    o_ref[...] = jnp.dot(s, w_ref[...].astype(BF16), preferred_element_type=F32) + b_ref[...]


def _adaln(cvec, ada_w, ada_b, tn=1024):
    return pl.pallas_call(
        _adaln_kernel,
        grid=(DEPTH, N_MOD // tn),
        in_specs=[pl.BlockSpec((N_SEG_PAD, D_MODEL), lambda l, j: (0, 0)),
                  pl.BlockSpec((None, D_MODEL, tn), lambda l, j: (l, 0, j)),
                  pl.BlockSpec((None, 1, tn), lambda l, j: (l, 0, j))],
        out_specs=pl.BlockSpec((None, N_SEG_PAD, tn), lambda l, j: (l, 0, j)),
        out_shape=jax.ShapeDtypeStruct((DEPTH, N_SEG_PAD, N_MOD), F32),
        compiler_params=_params("arbitrary", "arbitrary"),
        name="adaln",
    )(cvec, ada_w, ada_b.reshape(DEPTH, 1, N_MOD))


def _norm_mod(x, g, shift, scale):
    y = x * lax.rsqrt(jnp.mean(x * x, axis=-1, keepdims=True) + EPS) * g
    return y * (1.0 + scale) + shift


def _stream_specs(x_parts, tr, width, pos):
    if len(x_parts) == 1:
        return [pl.BlockSpec((tr, width), lambda *idx: (idx[pos], 0))]
    n_pt = N_P // tr
    return [pl.BlockSpec((tr, width), lambda *idx: (jnp.minimum(idx[pos], n_pt - 1), 0)),
            pl.BlockSpec((tr, width), lambda *idx: (jnp.maximum(idx[pos] - n_pt, 0), 0))]


def _stream_tile(x_refs, row0):
    if len(x_refs) == 1:
        return x_refs[0][...]
    return jnp.where(row0 < N_P, x_refs[0][...], x_refs[1][...])


def _norm_mod_kernel(*refs, tr, n_parts):
    x_refs, (g_ref, sh_ref, sc_ref, o_ref) = refs[:n_parts], refs[n_parts:]
    row0 = pl.program_id(0) * tr
    x = _stream_tile(x_refs, row0)
    o_ref[...] = _norm_mod(x, g_ref[...], _mod_row(sh_ref, row0), _mod_row(sc_ref, row0)).astype(BF16)


def _norm_mod_call(x_parts, g, mod, layer, blk, tr=512):
    return pl.pallas_call(
        functools.partial(_norm_mod_kernel, tr=tr, n_parts=len(x_parts)),
        grid=(N_TOK // tr,),
        in_specs=_stream_specs(x_parts, tr, D_MODEL, 0) + [
            pl.BlockSpec((1, D_MODEL), lambda i: (0, 0)),
            _mod_spec(layer, blk, D_MODEL, None),
            _mod_spec(layer, blk + 1, D_MODEL, None)],
        out_specs=pl.BlockSpec((tr, D_MODEL), lambda i: (i, 0)),
        out_shape=jax.ShapeDtypeStruct((N_TOK, D_MODEL), BF16),
        compiler_params=_params("arbitrary"),
        name="norm_mod",
    )(*x_parts, g.reshape(1, D_MODEL), mod, mod)


CAST_ROWS = 128


def _proj_kernel(h_ref, w_ref, wo_ref, o_ref, wob_ref, wb_ref):
    @pl.when(pl.program_id(1) == 0)
    def _():
        wb_ref[...] = w_ref[...].astype(BF16)

    wob_ref[...] = wo_ref[...].astype(BF16)
    o_ref[...] = jnp.dot(h_ref[...], wb_ref[...], preferred_element_type=F32).astype(o_ref.dtype)


def _proj(h, w, w_out, tm=2048, tn=1024):
    m, k = h.shape
    n = w.shape[1]
    k2, n2 = w_out.shape
    n_i = m // tm
    n_cast = k2 // CAST_ROWS
    assert n_cast * CAST_ROWS == k2 and n_cast <= (n // tn) * n_i

    def cast_block(j, i):
        return (jnp.minimum(j * n_i + i, n_cast - 1), 0)

    return pl.pallas_call(
        _proj_kernel,
        grid=(n // tn, n_i),
        in_specs=[pl.BlockSpec((tm, k), lambda j, i: (i, 0)),
                  pl.BlockSpec((k, tn), lambda j, i: (0, j)),
                  pl.BlockSpec((CAST_ROWS, n2), cast_block)],
        out_specs=[pl.BlockSpec((tm, tn), lambda j, i: (i, j)),
                   pl.BlockSpec((CAST_ROWS, n2), cast_block)],
        out_shape=[jax.ShapeDtypeStruct((m, n), BF16),
                   jax.ShapeDtypeStruct((k2, n2), BF16)],
        scratch_shapes=[pltpu.VMEM((k, tn), BF16)],
        compiler_params=_params("arbitrary", "arbitrary"),
        name="proj",
    )(h, w, w_out)


def _shift_rows(p, prev_row, next_row, row0, seq_len):
    n = p.shape[0]
    ridx = lax.broadcasted_iota(jnp.int32, (n, 1), 0)
    pos = (row0 + ridx) & (seq_len - 1)
    p_prev = jnp.where(ridx == 0, prev_row, pltpu.roll(p, 1, 0))
    p_prev = jnp.where(pos == 0, 0.0, p_prev)
    p_next = jnp.where(ridx == n - 1, next_row, pltpu.roll(p, n - 1, 0))
    p_next = jnp.where(pos == seq_len - 1, 0.0, p_next)
    return p_prev, p_next


def _sconv_kernel(ab_ref, ac_ref, ax_ref, pc_ref, px_ref, nc_ref, nx_ref, w_ref, o_ref, *, tr):
    row0 = pl.program_id(0) * tr
    p = ac_ref[...].astype(F32) * ax_ref[...].astype(F32)
    prev_row = pc_ref[HALO - 1:HALO, :].astype(F32) * px_ref[HALO - 1:HALO, :].astype(F32)
    next_row = nc_ref[0:1, :].astype(F32) * nx_ref[0:1, :].astype(F32)
    p_prev, p_next = _shift_rows(p, prev_row, next_row, row0, _seq_len(row0))
    w = w_ref[...]
    conv = w[0:1, :] * p_prev + w[1:2, :] * p + w[2:3, :] * p_next
    o_ref[...] = (ab_ref[...].astype(F32) * conv).astype(BF16)


def _halo_specs(tr, width, col):
    per = tr // HALO
    last = N_TOK // HALO - 1
    prev = pl.BlockSpec((HALO, width), lambda i: (jnp.maximum(i * per - 1, 0), col))
    nxt = pl.BlockSpec((HALO, width), lambda i: (jnp.minimum((i + 1) * per, last), col))
    return prev, nxt


def _short_gated_conv(proj, conv_a, tr=512):
    pc, nc = _halo_specs(tr, W_A, 1)
    px, nx = _halo_specs(tr, W_A, 2)
    return pl.pallas_call(
        functools.partial(_sconv_kernel, tr=tr),
        grid=(N_TOK // tr,),
        in_specs=[pl.BlockSpec((tr, W_A), lambda i: (i, 0)),
                  pl.BlockSpec((tr, W_A), lambda i: (i, 1)),
                  pl.BlockSpec((tr, W_A), lambda i: (i, 2)),
                  pc, px, nc, nx,
                  pl.BlockSpec((SHORT_K, W_A), lambda i: (0, 0))],
        out_specs=pl.BlockSpec((tr, W_A), lambda i: (i, 0)),
        out_shape=jax.ShapeDtypeStruct((N_TOK, W_A), BF16),
        compiler_params=_params("arbitrary"),
        name="short_gated_conv",
    )(proj, proj, proj, proj, proj, proj, proj, conv_a)


ROWS = DEC_SEQ // GRID_W
NA_QROWS = 8
NA_KROWS = 16
NA_NQ = NA_QROWS * GRID_W
NA_NK = NA_KROWS * GRID_W
NA_STEPS = ROWS // NA_QROWS
QKV_COL0 = 3 * W_A // NA_HEAD_DIM
HEAD_SCALE = NA_HEAD_DIM ** -0.5
LOG2E = math.log2(math.e)
NA_CHUNK = 16
NT_DIMS = (((1,), (1,)), ((), ()))


def _head_rmsnorm(x, g):
    return x * lax.rsqrt(jnp.mean(x * x, axis=-1, keepdims=True) + EPS) * g


def _na_key_row0(j):
    return jnp.clip(j * NA_QROWS - NA_WIN_R // 2, 0, ROWS - NA_KROWS)


NA_SLOT_LO = NA_QROWS
NA_SLOTS = 2 * NA_KROWS


def _na_geometry(j):
    kr0 = min(max(j * NA_QROWS - NA_WIN_R // 2, 0), ROWS - NA_KROWS)
    per_a = []
    for a in range(NA_QROWS):
        r = j * NA_QROWS + a
        r0 = min(max(r - NA_WIN_R // 2, 0), ROWS - NA_WIN_R)
        per_a.append((kr0 - r + NA_WIN_R - 1 + NA_SLOT_LO, r0 - kr0))
    return kr0, per_a


def _na_bias_slots(rpb):
    n_dr, n_dc = rpb.shape[1], rpb.shape[2]
    out = jax.ShapeDtypeStruct((NA_HEADS, GRID_W, NA_SLOTS * GRID_W), F32)
    spec = pl.BlockSpec((None, GRID_W, NA_SLOTS * GRID_W), lambda h: (h, 0, 0))
    return pl.pallas_call(
        functools.partial(_na_slots_kernel, n_dr=n_dr, n_dc=n_dc),
        grid=(NA_HEADS,),
        in_specs=[pl.BlockSpec(memory_space=pltpu.SMEM)],
        out_specs=[spec, spec],
        out_shape=[out, out],
        compiler_params=_params("arbitrary"),
        name="na_bias_slots",
    )(rpb.astype(F32).reshape(NA_HEADS, n_dr * n_dc))


def _na_slots_kernel(rpb_ref, even_ref, odd_ref, *, n_dr, n_dc):
    h = pl.program_id(0)
    pair = 2 * GRID_W
    qc = lax.broadcasted_iota(jnp.int32, (GRID_W, pair), 0)
    lane = lax.broadcasted_iota(jnp.int32, (GRID_W, pair), 1)
    kc = lane & (GRID_W - 1)
    delta = kc - qc + NA_WIN_C - 1
    cs = jnp.clip(qc - NA_WIN_C // 2, 0, GRID_W - NA_WIN_C)
    col_ok = (kc >= cs) & (kc < cs + NA_WIN_C)
    upper = lax.broadcasted_iota(jnp.int32, (1, pair), 1) >= GRID_W
    for shift, ref in ((0, even_ref), (1, odd_ref)):
        for t in range(NA_SLOTS // 2):
            drs = [2 * t + shift + half - NA_SLOT_LO for half in (0, 1)]
            acc = jnp.full((GRID_W, pair), NEG_INF, F32)
            if any(0 <= dr < n_dr for dr in drs):
                for dc in range(n_dc):
                    lo, hi = [rpb_ref[h, dr * n_dc + dc] if 0 <= dr < n_dr else NEG_INF for dr in drs]
                    acc = jnp.where(delta == dc, jnp.where(upper, hi, lo), acc)
                acc = jnp.where(col_ok, acc, NEG_INF)
            ref[:, t * pair:(t + 1) * pair] = acc


def _na_build_bias(j_static, even_ref, odd_ref, bias_ref):
    _, per_a = _na_geometry(j_static)
    key_row = lax.broadcasted_iota(jnp.int32, (1, NA_NK), 1) // GRID_W
    for a, (m, lo) in enumerate(per_a):
        src = even_ref if m % 2 == 0 else odd_ref
        start = (m - m % 2) * GRID_W
        slab = src[:, start:start + NA_NK] * LOG2E
        ok = (key_row >= lo) & (key_row < lo + NA_WIN_R)
        bias_ref[a * GRID_W:(a + 1) * GRID_W, :] = jnp.where(ok, slab, NEG_INF)


def _with_ones_column(v):
    lane = lax.broadcasted_iota(jnp.int32, v.shape, 1)
    return jnp.concatenate([v.astype(BF16), jnp.where(lane == 0, 1.0, 0.0).astype(BF16)], axis=1)


def _na_kernel(q_ref, k_ref, v_ref, ck_ref, cv_ref, even_ref, odd_ref, qg_ref, kg_ref, o_ref,
               kn_ref, vx_ref, kall_ref, vall_ref, bias_ref, s_ref, p_ref):
    j = pl.program_id(2)
    hd = NA_HEAD_DIM

    @pl.when(j == 0)
    def _():
        kn_ref[...] = _head_rmsnorm(k_ref[...].astype(F32), kg_ref[...]).astype(BF16)
        vx_ref[...] = _with_ones_column(v_ref[...])
        kall_ref[NA_NK:, :] = ck_ref[...].astype(BF16)
        vall_ref[NA_NK:, :] = _with_ones_column(cv_ref[...])

    for j_static in (0, 1, NA_STEPS - 1):
        @pl.when(j == j_static)
        def _(j_static=j_static):
            _na_build_bias(j_static, even_ref, odd_ref, bias_ref)

    q = (_head_rmsnorm(q_ref[...].astype(F32), qg_ref[...]) * (HEAD_SCALE * LOG2E)).astype(BF16)
    tok0 = pl.multiple_of(_na_key_row0(j) * GRID_W, GRID_W)
    kall_ref[:NA_NK, :] = kn_ref[pl.ds(tok0, NA_NK), :]
    vall_ref[:NA_NK, :] = vx_ref[pl.ds(tok0, NA_NK), :]
    s_ref[...] = lax.dot_general(q, kall_ref[...], NT_DIMS, preferred_element_type=F32)
    for c in range(NA_NQ // NA_CHUNK):
        rows = slice(c * NA_CHUNK, (c + 1) * NA_CHUNK)
        s_loc = s_ref[rows, :NA_NK] + bias_ref[rows, :]
        s_ctx = s_ref[rows, NA_NK:]
        m = functools.reduce(jnp.maximum, [blk[:, t:t + hd] for blk in (s_loc, s_ctx)
                                           for t in range(0, blk.shape[1], hd)])
        m = jnp.max(m, axis=-1, keepdims=True)
        p_ref[rows, :NA_NK] = jnp.exp2(s_loc - m).astype(BF16)
        p_ref[rows, NA_NK:] = jnp.exp2(s_ctx - m).astype(BF16)
    o = jnp.dot(p_ref[...], vall_ref[...], preferred_element_type=F32)
    o_ref[...] = (o[:, :hd] / o[:, hd:hd + 1]).astype(BF16)


def _na_attention(proj, cache_k, cache_v, rpb, q_norm, k_norm):
    assert NA_STEPS >= 3 and all(_na_geometry(j)[1] == _na_geometry(1)[1] for j in range(1, NA_STEPS - 1))
    even, odd = _na_bias_slots(rpb)
    ck = cache_k.reshape(DEC_BATCH, PAST_LEN, W_B)
    cv = cache_v.reshape(DEC_BATCH, PAST_LEN, W_B)
    hd = NA_HEAD_DIM
    ctx = pl.BlockSpec((None, PAST_LEN, hd), lambda b, h, j: (b, 0, h))
    q_blk0 = N_P // NA_NQ
    kv_blk0 = N_P // DEC_SEQ
    slots = pl.BlockSpec((None, GRID_W, NA_SLOTS * GRID_W), lambda b, h, j: (h, 0, 0))
    return pl.pallas_call(
        _na_kernel,
        grid=(DEC_BATCH, NA_HEADS, NA_STEPS),
        in_specs=[
            pl.BlockSpec((NA_NQ, hd), lambda b, h, j: (q_blk0 + b * NA_STEPS + j, QKV_COL0 + h)),
            pl.BlockSpec((DEC_SEQ, hd), lambda b, h, j: (kv_blk0 + b, QKV_COL0 + NA_HEADS + h)),
            pl.BlockSpec((DEC_SEQ, hd), lambda b, h, j: (kv_blk0 + b, QKV_COL0 + 2 * NA_HEADS + h)),
            ctx, ctx,
            slots, slots,
            pl.BlockSpec((1, hd), lambda b, h, j: (0, 0)),
            pl.BlockSpec((1, hd), lambda b, h, j: (0, 0)),
        ],
        out_specs=pl.BlockSpec((NA_NQ, hd), lambda b, h, j: (b * NA_STEPS + j, h)),
        out_shape=jax.ShapeDtypeStruct((N_S, W_B), BF16),
        scratch_shapes=[pltpu.VMEM((DEC_SEQ, hd), BF16),
                        pltpu.VMEM((DEC_SEQ, 2 * hd), BF16),
                        pltpu.VMEM((NA_NK + PAST_LEN, hd), BF16),
                        pltpu.VMEM((NA_NK + PAST_LEN, 2 * hd), BF16),
                        pltpu.VMEM((NA_NQ, NA_NK), F32),
                        pltpu.VMEM((NA_NQ, NA_NK + PAST_LEN), F32),
                        pltpu.VMEM((NA_NQ, NA_NK + PAST_LEN), BF16)],
        compiler_params=_params("arbitrary", "arbitrary", "arbitrary"),
        name="na_attention",
    )(proj, proj, proj, ck, cv, even, odd, q_norm.reshape(1, hd), k_norm.reshape(1, hd))


def _ctx_kernel(q_ref, k_ref, v_ref, qg_ref, kg_ref, o_ref, kn_ref, vo_ref):
    vo_ref[...] = v_ref[...].astype(F32)
    for h in range(NA_HEADS):
        cols = slice(h * NA_HEAD_DIM, (h + 1) * NA_HEAD_DIM)
        q = (_head_rmsnorm(q_ref[:, cols].astype(F32), qg_ref[...]) * HEAD_SCALE).astype(BF16)
        kn = _head_rmsnorm(k_ref[:, cols].astype(F32), kg_ref[...])
        kn_ref[:, cols] = kn
        s = lax.dot_general(q, kn.astype(BF16), NT_DIMS, preferred_element_type=F32)
        p = jnp.exp(s - jnp.max(s, axis=-1, keepdims=True))
        denom = jnp.sum(p, axis=-1, keepdims=True)
        o = jnp.dot(p.astype(BF16), v_ref[:, cols], preferred_element_type=F32)
        o_ref[:, cols] = (o / denom).astype(BF16)


def _ctx_attention(proj, q_norm, k_norm):
    hd = NA_HEAD_DIM
    col0 = 3 * W_A // W_B
    ospec = pl.BlockSpec((SEQ, W_B), lambda b: (b, 0))
    return pl.pallas_call(
        _ctx_kernel,
        grid=(BATCH,),
        in_specs=[
            pl.BlockSpec((SEQ, W_B), lambda b: (b, col0)),
            pl.BlockSpec((SEQ, W_B), lambda b: (b, col0 + 1)),
            pl.BlockSpec((SEQ, W_B), lambda b: (b, col0 + 2)),
            pl.BlockSpec((1, hd), lambda b: (0, 0)),
            pl.BlockSpec((1, hd), lambda b: (0, 0)),
        ],
        out_specs=[ospec, ospec, ospec],
        out_shape=[jax.ShapeDtypeStruct((N_P, W_B), BF16),
                   jax.ShapeDtypeStruct((N_P, W_B), F32),
                   jax.ShapeDtypeStruct((N_P, W_B), F32)],
        compiler_params=_params("arbitrary"),
        name="ctx_attention",
    )(proj, proj, proj, q_norm.reshape(1, hd), k_norm.reshape(1, hd))


def _out_proj_kernel(*refs, tm, ka, n_ya, n_yb, n_x):
    ya_refs, refs = refs[:n_ya], refs[n_ya:]
    yb_refs, refs = refs[:n_yb], refs[n_yb:]
    w_ref, refs = refs[0], refs[1:]
    x_refs, refs = refs[:n_x], refs[n_x:]
    gate_ref, g_ref, sh_ref, sc_ref, xo_ref, ho_ref = refs
    row0 = pl.program_id(0) * tm
    o = (jnp.dot(_stream_tile(ya_refs, row0), w_ref[0:ka, :], preferred_element_type=F32)
         + jnp.dot(_stream_tile(yb_refs, row0), w_ref[ka:, :], preferred_element_type=F32))
    x_new = _stream_tile(x_refs, row0) + _mod_row(gate_ref, row0) * o
    xo_ref[...] = x_new
    ho_ref[...] = _norm_mod(x_new, g_ref[...], _mod_row(sh_ref, row0), _mod_row(sc_ref, row0)).astype(BF16)


def _out_proj(ya_parts, yb_parts, w_bf16, x_parts, mod, layer, g_ffn, tm=512):
    ka, kb = ya_parts[0].shape[1], yb_parts[0].shape[1]
    return pl.pallas_call(
        functools.partial(_out_proj_kernel, tm=tm, ka=ka, n_ya=len(ya_parts), n_yb=len(yb_parts),
                          n_x=len(x_parts)),
        grid=(N_TOK // tm,),
        in_specs=_stream_specs(ya_parts, tm, ka, 0)
        + _stream_specs(yb_parts, tm, kb, 0)
        + [pl.BlockSpec((ka + kb, D_MODEL), lambda i: (0, 0))]
        + _stream_specs(x_parts, tm, D_MODEL, 0) + [
            _mod_spec(layer, 2, D_MODEL, None),
            pl.BlockSpec((1, D_MODEL), lambda i: (0, 0)),
            _mod_spec(layer, 3, D_MODEL, None),
            _mod_spec(layer, 4, D_MODEL, None)],
        out_specs=[pl.BlockSpec((tm, D_MODEL), lambda i: (i, 0)),
                   pl.BlockSpec((tm, D_MODEL), lambda i: (i, 0))],
        out_shape=[jax.ShapeDtypeStruct((N_TOK, D_MODEL), F32),
                   jax.ShapeDtypeStruct((N_TOK, D_MODEL), BF16)],
        compiler_params=_params("arbitrary"),
        name="out_proj",
    )(*ya_parts, *yb_parts, w_bf16, *x_parts, mod, g_ffn.reshape(1, D_MODEL), mod, mod)


GELU_C = math.sqrt(2.0 / math.pi)


def _gelu_tanh(x):
    return 0.5 * x * (1.0 + jnp.tanh(GELU_C * (x + 0.044715 * (x * x * x))))


def _ffn_in_kernel(h_ref, hp_ref, hn_ref, wa_ref, wg_ref, cw_ref, wo_ref, o_ref, wob_ref, wab_ref, wgb_ref,
                   *, tm, tn):
    i = pl.program_id(1)

    @pl.when(i == 0)
    def _():
        wab_ref[...] = wa_ref[...].astype(BF16)
        wgb_ref[...] = wg_ref[...].astype(BF16)

    wob_ref[...] = wo_ref[...].astype(BF16)

    row0 = i * tm
    seq_len = _seq_len(row0)
    h = h_ref[...]
    h_ext = jnp.concatenate([hp_ref[...], h, hn_ref[...]], axis=0)
    n = tm + 2 * HALO
    pos = (row0 + lax.broadcasted_iota(jnp.int32, (tm, 1), 0)) & (seq_len - 1)
    first = pos == 0
    last = pos == seq_len - 1
    a_ext = jnp.dot(h_ext, wab_ref[...], preferred_element_type=F32)
    g = jnp.dot(h, wgb_ref[...], preferred_element_type=F32)
    a_mid = a_ext[HALO:HALO + tm, :]
    a_prev = jnp.where(first, 0.0, pltpu.roll(a_ext, 1, 0)[HALO:HALO + tm, :])
    a_next = jnp.where(last, 0.0, pltpu.roll(a_ext, n - 1, 0)[HALO:HALO + tm, :])
    cw = cw_ref[...]
    conv = cw[0:1, :] * a_prev + cw[1:2, :] * a_mid + cw[2:3, :] * a_next
    o_ref[...] = (_gelu_tanh(conv) * g).astype(BF16)


def _ffn_in(h, w_in, conv, w_out, layer, tm=1024, tn=512):
    per = tm // HALO
    last = N_TOK // HALO - 1
    ng = D_FF // tn
    n_i = N_TOK // tm
    per_j = D_FF // ng // CAST_ROWS
    assert per_j <= n_i and per_j * CAST_ROWS * ng == D_FF

    def cast_block(j, i):
        return j * per_j + jnp.minimum(i, per_j - 1)

    return pl.pallas_call(
        functools.partial(_ffn_in_kernel, tm=tm, tn=tn),
        grid=(ng, n_i),
        in_specs=[pl.BlockSpec((tm, D_MODEL), lambda j, i: (i, 0)),
                  pl.BlockSpec((HALO, D_MODEL), lambda j, i: (jnp.maximum(i * per - 1, 0), 0)),
                  pl.BlockSpec((HALO, D_MODEL), lambda j, i: (jnp.minimum((i + 1) * per, last), 0)),
                  pl.BlockSpec((None, D_MODEL, tn), lambda j, i: (layer, 0, j)),
                  pl.BlockSpec((None, D_MODEL, tn), lambda j, i: (layer, 0, ng + j)),
                  pl.BlockSpec((None, SHORT_K, tn), lambda j, i: (layer, 0, j)),
                  pl.BlockSpec((None, CAST_ROWS, D_MODEL), lambda j, i: (layer, cast_block(j, i), 0))],
        out_specs=[pl.BlockSpec((tm, tn), lambda j, i: (i, j)),
                   pl.BlockSpec((CAST_ROWS, D_MODEL), lambda j, i: (cast_block(j, i), 0))],
        out_shape=[jax.ShapeDtypeStruct((N_TOK, D_FF), BF16),
                   jax.ShapeDtypeStruct((D_FF, D_MODEL), BF16)],
        scratch_shapes=[pltpu.VMEM((D_MODEL, tn), BF16), pltpu.VMEM((D_MODEL, tn), BF16)],
        compiler_params=_params("arbitrary", "arbitrary"),
        name="ffn_in",
    )(h, h, h, w_in, w_in, conv, w_out)


def _ffn_out_kernel(u_ref, w_ref, x_ref, gate_ref, o_ref, *, tm, row_start):
    row0 = row_start + pl.program_id(1) * tm
    o = jnp.dot(u_ref[...], w_ref[...], preferred_element_type=F32)
    o_ref[...] = x_ref[...] + _mod_row(gate_ref, row0) * o


def _ffn_out(u, w_bf16, x, mod, layer, row_start=0, n_rows=N_TOK, tm=512, tn=1024):
    t0 = row_start // tm
    return pl.pallas_call(
        functools.partial(_ffn_out_kernel, tm=tm, row_start=row_start),
        grid=(D_MODEL // tn, n_rows // tm),
        in_specs=[pl.BlockSpec((tm, D_FF), lambda j, i: (t0 + i, 0)),
                  pl.BlockSpec((D_FF, tn), lambda j, i: (0, j)),
                  pl.BlockSpec((tm, tn), lambda j, i: (t0 + i, j)),
                  _mod_spec(layer, 5, tn, 0)],
        out_specs=pl.BlockSpec((tm, tn), lambda j, i: (i, j)),
        out_shape=jax.ShapeDtypeStruct((n_rows, D_MODEL), F32),
        compiler_params=_params("arbitrary", "arbitrary"),
        name="ffn_out",
    )(u, w_bf16, x, mod)


CONF_TR = 256
CONF_PAD = CONF_K // 2
CONF_RB = 128
CONF_CB = 128
CONF_LN_ROWS = 16


def _conformer_tile(ca_ref, cg_ref, pa_ref, pg_ref, na_ref, ng_ref, w_ref, b_ref, lg_ref, lb_ref, o_ref,
                    u_ref, v_ref, tile):
    tr = CONF_TR
    row0 = tile * tr
    seq_len = _seq_len(row0)
    pos0 = row0 & (seq_len - 1)

    def glu(a_ref, g_ref):
        return a_ref[...].astype(F32) * jax.nn.sigmoid(g_ref[...].astype(F32))

    u_ref[0:HALO, :] = jnp.where(pos0 == 0, 0.0, glu(pa_ref, pg_ref))
    u_ref[HALO:HALO + tr, :] = glu(ca_ref, cg_ref)
    u_ref[HALO + tr:, :] = jnp.where(pos0 + tr == seq_len, 0.0, glu(na_ref, ng_ref))
    base = HALO - CONF_PAD
    n_in = CONF_RB + 2 * HALO
    for cb in range(W_C // CONF_CB):
        cols = slice(cb * CONF_CB, (cb + 1) * CONF_CB)
        w = w_ref[:, cols]
        for rb in range(tr // CONF_RB):
            u_blk = u_ref[rb * CONF_RB:rb * CONF_RB + n_in, cols]
            acc = jnp.zeros((CONF_RB, CONF_CB), F32) + b_ref[:, cols]
            for r in range(8):
                u_r = pltpu.roll(u_blk, n_in - (base + r), 0)
                for a in range(len(range(r, CONF_K, 8))):
                    k = 8 * a + r
                    acc = acc + w[k:k + 1, :] * u_r[8 * a:8 * a + CONF_RB, :]
            v_ref[rb * CONF_RB:(rb + 1) * CONF_RB, cols] = acc
    for c in range(tr // CONF_LN_ROWS):
        rows = slice(c * CONF_LN_ROWS, (c + 1) * CONF_LN_ROWS)
        v = v_ref[rows, :]
        mu = jnp.mean(v, axis=-1, keepdims=True)
        d = v - mu
        var = jnp.mean(d * d, axis=-1, keepdims=True)
        y = d * lax.rsqrt(var + EPS) * lg_ref[...] + lb_ref[...]
        o_ref[rows, :] = (y * jax.nn.sigmoid(y)).astype(BF16)


def _conformer_kernel(*refs):
    _conformer_tile(*refs, pl.program_id(0))


def _conformer(proj, dw, dw_b, ln_g, ln_b):
    pa, na = _halo_specs(CONF_TR, W_C, 0)
    pg, ng = _halo_specs(CONF_TR, W_C, 1)
    row = pl.BlockSpec((1, W_C), lambda i: (0, 0))
    return pl.pallas_call(
        _conformer_kernel,
        grid=(N_TOK // CONF_TR,),
        in_specs=[pl.BlockSpec((CONF_TR, W_C), lambda i: (i, 0)),
                  pl.BlockSpec((CONF_TR, W_C), lambda i: (i, 1)),
                  pa, pg, na, ng,
                  pl.BlockSpec((CONF_K, W_C), lambda i: (0, 0)), row, row, row],
        out_specs=pl.BlockSpec((CONF_TR, W_C), lambda i: (i, 0)),
        out_shape=jax.ShapeDtypeStruct((N_TOK, W_C), BF16),
        scratch_shapes=[pltpu.VMEM((CONF_TR + 2 * HALO, W_C), F32), pltpu.VMEM((CONF_TR, W_C), F32)],
        compiler_params=_params("arbitrary"),
        name="conformer",
    )(proj, proj, proj, proj, proj, proj, dw, dw_b.reshape(1, W_C), ln_g.reshape(1, W_C), ln_b.reshape(1, W_C))


HY_COL0 = 2 * W_C // W_D


def _hy_short_kernel(*refs, tr):
    (x0_ref, x1_ref, vv_ref, p0_ref, p1_ref, pv_ref, n0_ref, n1_ref, nv_ref,
     w0_ref, w1_ref, wv_ref, b0_ref, b1_ref, bv_ref, x0o_ref, wo_ref, tmp_ref) = refs
    row0 = pl.program_id(0) * tr
    seq_len = _seq_len(row0)

    def conv(c_ref, p_ref, n_ref, w_ref, b_ref):
        p = c_ref[...].astype(F32)
        p_prev, p_next = _shift_rows(p, p_ref[HALO - 1:HALO, :].astype(F32), n_ref[0:1, :].astype(F32),
                                     row0, seq_len)
        w = w_ref[...]
        return w[0:1, :] * p_prev + w[1:2, :] * p + w[2:3, :] * p_next + b_ref[...]

    _store_even_odd(x0o_ref, tmp_ref, conv(x0_ref, p0_ref, n0_ref, w0_ref, b0_ref))
    x1 = conv(x1_ref, p1_ref, n1_ref, w1_ref, b1_ref)
    vv = conv(vv_ref, pv_ref, nv_ref, wv_ref, bv_ref)
    _store_even_odd(wo_ref, tmp_ref, vv * x1)


def _store_even_odd(o_ref, tmp_ref, val):
    n, width = val.shape
    for cb in range(width // LANES):
        cols = slice(cb * LANES, (cb + 1) * LANES)
        tmp_ref[cb] = val[:, cols]
        for parity in (0, 1):
            dst = slice(parity * width + cb * LANES, parity * width + (cb + 1) * LANES)
            o_ref[:, dst] = tmp_ref[cb, pl.ds(parity, n // 2, stride=2), :].astype(BF16)


def _hy_short(proj, hy_short, hy_short_b, tr=512):
    main = [pl.BlockSpec((tr, W_D), functools.partial(lambda i, c: (i, c), c=HY_COL0 + c)) for c in range(3)]
    halos = [_halo_specs(tr, W_D, HY_COL0 + c) for c in range(3)]
    wspec = [pl.BlockSpec((SHORT_K, W_D), functools.partial(lambda i, c: (0, c), c=c)) for c in range(3)]
    bspec = [pl.BlockSpec((1, W_D), functools.partial(lambda i, c: (0, c), c=c)) for c in range(3)]
    out = jax.ShapeDtypeStruct((N_TOK // 2, W2_D), BF16)
    ospec = pl.BlockSpec((tr // 2, W2_D), lambda i: (i, 0))
    b2 = hy_short_b.reshape(1, 3 * W_D)
    return pl.pallas_call(
        functools.partial(_hy_short_kernel, tr=tr),
        grid=(N_TOK // tr,),
        in_specs=main + [h[0] for h in halos] + [h[1] for h in halos] + wspec + bspec,
        out_specs=[ospec, ospec],
        out_shape=[out, out],
        scratch_shapes=[pltpu.VMEM((W_D // LANES, tr, LANES), F32)],
        compiler_params=_params("arbitrary"),
        name="hy_short",
    )(*([proj] * 9), hy_short, hy_short, hy_short, b2, b2, b2)


HIGHEST = lax.Precision.HIGHEST
HY_EMB_PAD = 128
W2_D = 2 * W_D


def _dft_gen_kernel(ct_ref, st_ref, cr_ref, sr_ref, c_ref, s_ref):
    t = pl.program_id(0)
    cr = cr_ref[pl.ds(t, 1), :]
    sr = sr_ref[pl.ds(t, 1), :]
    ct = ct_ref[...]
    st = st_ref[...]
    c_ref[...] = (cr * ct - sr * st).astype(BF16)
    s_ref[...] = (sr * ct + cr * st).astype(BF16)


def _dft_matrices(L):
    tr = min(L, 256)
    nt = L // tr
    nt_pad = -(-nt // 8) * 8
    n = jnp.arange(L, dtype=jnp.int32)[None, :]

    def tables(k):
        ang = ((k[:, None] * n) % (2 * L)).astype(F32) * (math.pi / L)
        return jnp.cos(ang), jnp.sin(ang)

    ct, st = tables(jnp.arange(tr, dtype=jnp.int32))
    cr, sr = tables(jnp.arange(nt_pad, dtype=jnp.int32) * tr)
    tile = pl.BlockSpec((tr, L), lambda t: (0, 0))
    rows = pl.BlockSpec((nt_pad, L), lambda t: (0, 0))
    out = pl.BlockSpec((tr, L), lambda t: (t, 0))
    return pl.pallas_call(
        _dft_gen_kernel,
        grid=(nt,),
        in_specs=[tile, tile, rows, rows],
        out_specs=[out, out],
        out_shape=[jax.ShapeDtypeStruct((L, L), BF16)] * 2,
        compiler_params=_params("arbitrary"),
        name="dft_gen",
    )(ct, st, cr, sr)


def _hy_features(L):
    t = jnp.linspace(0.0, 1.0, L, dtype=F32)[:, None]
    bands = (HY_EMB - 1) // 2
    ang = 2 * math.pi * jnp.arange(L, dtype=F32)[:, None] / L
    freqs = jnp.linspace(1e-4, bands - 1, bands, dtype=F32)[None, :]
    z = jnp.concatenate([t, jnp.cos(freqs * ang), -jnp.sin(freqs * ang)], axis=-1)
    return jnp.pad(z, ((0, 0), (0, HY_EMB_PAD - HY_EMB))), t


def _hy_filter_kernel(z_ref, t_ref, w1_ref, b1_ref, f1_ref, w2_ref, b2_ref, f2_ref, w3_ref, dl_ref,
                      fs_ref, fd_ref, hn_ref, tmp_ref, *, tr):
    i = pl.program_id(0)
    hid = jnp.sin(f1_ref[...] * (jnp.dot(z_ref[...], w1_ref[...], precision=HIGHEST,
                                         preferred_element_type=F32) + b1_ref[...]))
    hid = jnp.sin(f2_ref[...] * (jnp.dot(hid, w2_ref[...], precision=HIGHEST,
                                         preferred_element_type=F32) + b2_ref[...]))
    hf = jnp.dot(hid.astype(BF16), w3_ref[...].astype(BF16), preferred_element_type=F32)
    decay = jnp.exp(-t_ref[...] * jnp.abs(dl_ref[...]))
    ridx = i * tr + lax.broadcasted_iota(jnp.int32, (tr, 1), 0)
    fwd = hf[:, :W_D] * decay
    bwd = jnp.where(ridx == 0, 0.0, hf[:, W_D:] * decay)
    fsum = fwd + bwd
    fdif = bwd - fwd
    _store_even_odd(fs_ref, tmp_ref, fsum)
    _store_even_odd(fd_ref, tmp_ref, fdif)
    sgn = (1 - (ridx & 2)).astype(F32)
    even = (ridx & 1) == 0
    hc_part = jnp.sum(jnp.where(even, sgn, 0.0) * fsum, axis=0, keepdims=True)
    hs_part = jnp.sum(jnp.where(even, 0.0, sgn) * fdif, axis=0, keepdims=True)

    @pl.when(i == 0)
    def _():
        hn_ref[...] = jnp.zeros_like(hn_ref)

    hn_ref[0:1, :] += hc_part
    hn_ref[1:2, :] += hs_part


def _hy_filter(L, w1, b1, f1, w2, b2, f2, w3):
    tr = min(L, 512)
    z, t = _hy_features(L)
    max_decay = math.log(HY_TARGET) / HY_FAST_PCT
    min_decay = math.log(HY_TARGET) / HY_SLOW_PCT
    deltas = jnp.linspace(min_decay, max_decay, W_D, dtype=F32)[None, :]
    w1p = jnp.pad(w1, ((0, HY_EMB_PAD - HY_EMB), (0, 0)))
    full = lambda shape: pl.BlockSpec(shape, lambda i: (0, 0))
    return pl.pallas_call(
        functools.partial(_hy_filter_kernel, tr=tr),
        grid=(L // tr,),
        in_specs=[pl.BlockSpec((tr, HY_EMB_PAD), lambda i: (i, 0)),
                  pl.BlockSpec((tr, 1), lambda i: (i, 0)),
                  full((HY_EMB_PAD, HY_HIDDEN)), full((1, HY_HIDDEN)), full((1, HY_HIDDEN)),
                  full((HY_HIDDEN, HY_HIDDEN)), full((1, HY_HIDDEN)), full((1, HY_HIDDEN)),
                  full((HY_HIDDEN, 2 * W_D)), full((1, W_D))],
        out_specs=[pl.BlockSpec((tr // 2, W2_D), lambda i: (i, 0)),
                   pl.BlockSpec((tr // 2, W2_D), lambda i: (i, 0)),
                   pl.BlockSpec((8, W_D), lambda i: (0, 0))],
        out_shape=[jax.ShapeDtypeStruct((L // 2, W2_D), BF16),
                   jax.ShapeDtypeStruct((L // 2, W2_D), BF16),
                   jax.ShapeDtypeStruct((8, W_D), F32)],
        scratch_shapes=[pltpu.VMEM((W_D // LANES, tr, LANES), F32)],
        compiler_params=_params("arbitrary"),
        name="hy_filter",
    )(z, t, w1p, b1.reshape(1, -1), f1.reshape(1, -1), w2, b2.reshape(1, -1), f2.reshape(1, -1), w3, deltas)


def _twiddles(L):
    k = jnp.arange(L // 2, dtype=F32)[:, None] * (math.pi / L)
    return jnp.cos(k), jnp.sin(k)


def _butterfly(c_ref, s_ref, x2, twc, tws):
    xc2 = jnp.dot(c_ref[...], x2, preferred_element_type=F32)
    xs2 = jnp.dot(s_ref[...], x2, preferred_element_type=F32)
    ec, oc = xc2[:, :W_D], xc2[:, W_D:]
    es, os_ = xs2[:, :W_D], xs2[:, W_D:]
    tc = twc * oc - tws * os_
    ts = twc * os_ + tws * oc
    return ec + tc, es + ts, ec - tc, es - ts


def _hy_spec_kernel(c_ref, s_ref, fs_ref, fd_ref, twc_ref, tws_ref, hac_ref, has_ref, hrc_ref, hrs_ref):
    twc, tws = twc_ref[...], tws_ref[...]
    ac, _, bc, _ = _butterfly(c_ref, s_ref, fs_ref[...], twc, tws)
    _, as_, _, bs = _butterfly(c_ref, s_ref, fd_ref[...], twc, tws)
    hac_ref[...] = ac
    has_ref[...] = as_
    hrc_ref[...] = bc
    hrs_ref[...] = -bs


def _hy_spectrum(cmat, smat, fs2, fd2, twc, tws, tk):
    M = cmat.shape[0]
    tile = pl.BlockSpec((tk, M), lambda i: (i, 0))
    full = pl.BlockSpec((M, W2_D), lambda i: (0, 0), pipeline_mode=pl.Buffered(1))
    col = pl.BlockSpec((tk, 1), lambda i: (i, 0))
    out = pl.BlockSpec((tk, W_D), lambda i: (i, 0))
    return pl.pallas_call(
        _hy_spec_kernel,
        grid=(M // tk,),
        in_specs=[tile, tile, full, full, col, col],
        out_specs=[out] * 4,
        out_shape=[jax.ShapeDtypeStruct((M, W_D), F32)] * 4,
        compiler_params=_params("arbitrary"),
        name="hy_spectrum",
    )(cmat, smat, fs2, fd2, twc, tws)


def _hy_fwd_kernel(c_ref, s_ref, w_ref, hac_ref, has_ref, hrc_ref, hrs_ref, twc_ref, tws_ref, hn_ref,
                   vc_ref, vs_ref, yn_ref, *, L, tk):
    kt = pl.program_id(1)
    w2 = w_ref[...]
    twc, tws = twc_ref[...], tws_ref[...]
    ac, as_, bc, bs = _butterfly(c_ref, s_ref, w2, twc, tws)
    hac, has, hrc, hrs = hac_ref[...], has_ref[...], hrc_ref[...], hrs_ref[...]
    yac = ac * hac + as_ * has
    yai = ac * has - as_ * hac
    pc = bc * hrc - bs * hrs
    pi = -(bc * hrs + bs * hrc)
    dc, di = yac - pc, yai - pi
    kidx = kt * tk + lax.broadcasted_iota(jnp.int32, (tk, 1), 0)
    om = jnp.where(kidx == 0, 0.5 / L, 1.0 / L)
    vc_ref[...] = (om * jnp.concatenate([yac + pc, twc * dc - tws * di], axis=1)).astype(BF16)
    vs_ref[...] = ((-1.0 / L) * jnp.concatenate([yai + pi, twc * di + tws * dc], axis=1)).astype(BF16)

    @pl.when(kt == 0)
    def _():
        m = lax.broadcasted_iota(jnp.int32, (L // 2, 1), 0)
        sgn = (1 - 2 * (m & 1)).astype(F32)
        xn2 = jnp.sum(w2.astype(F32) * sgn, axis=0, keepdims=True)
        en, on = xn2[:, :W_D], xn2[:, W_D:]
        hc, hs = hn_ref[0:1, :], hn_ref[1:2, :]
        yn = jnp.concatenate([en * hc + on * hs, on * hc - en * hs], axis=1) * (1.0 / L)
        yn_ref[...] = jnp.broadcast_to(yn, yn_ref.shape)


def _hy_forward(cmat, smat, w2, h4, twc, tws, hn, L, nb, blk0, tk):
    M = L // 2
    tile = pl.BlockSpec((tk, M), lambda b, k: (k, 0))
    htile = pl.BlockSpec((tk, W_D), lambda b, k: (k, 0))
    col = pl.BlockSpec((tk, 1), lambda b, k: (k, 0))
    out = pl.BlockSpec((tk, W2_D), lambda b, k: (b * (M // tk) + k, 0))
    return pl.pallas_call(
        functools.partial(_hy_fwd_kernel, L=L, tk=tk),
        grid=(nb, M // tk),
        in_specs=[tile, tile, pl.BlockSpec((M, W2_D), lambda b, k: (blk0 + b, 0)),
                  htile, htile, htile, htile, col, col,
                  pl.BlockSpec((8, W_D), lambda b, k: (0, 0))],
        out_specs=[out, out, pl.BlockSpec((None, 8, W2_D), lambda b, k: (b, 0, 0))],
        out_shape=[jax.ShapeDtypeStruct((nb * M, W2_D), BF16),
                   jax.ShapeDtypeStruct((nb * M, W2_D), BF16),
                   jax.ShapeDtypeStruct((nb, 8, W2_D), F32)],
        compiler_params=_params("arbitrary", "arbitrary"),
        name="hy_forward",
    )(cmat, smat, w2, *h4, twc, tws, hn)


def _hy_inv_kernel(c_ref, s_ref, vc_ref, vs_ref, yn_ref, x0_ref, w_ref, bias_ref, z_ref, tmp_ref, *, tt):
    ti = pl.program_id(1)
    y = (jnp.dot(c_ref[...], vc_ref[...], preferred_element_type=F32)
         + jnp.dot(s_ref[...], vs_ref[...], preferred_element_type=F32))
    m = ti * tt + lax.broadcasted_iota(jnp.int32, (tt, 1), 0)
    sgn = (1 - 2 * (m & 1)).astype(F32)
    y = y + sgn * yn_ref[0:1, :] + w_ref[...].astype(F32) * bias_ref[...]
    z2 = x0_ref[...].astype(F32) * y
    for cb in range(W_D // LANES):
        cols = slice(cb * LANES, (cb + 1) * LANES)
        tmp_ref[cb, pl.ds(0, tt, stride=2), :] = z2[:, cols]
        tmp_ref[cb, pl.ds(1, tt, stride=2), :] = z2[:, W_D + cb * LANES:W_D + (cb + 1) * LANES]
        z_ref[:, cols] = tmp_ref[cb].astype(BF16)


def _hy_inverse(cmat, smat, vc, vs, yn, x02, w2, bias, L, nb, blk0, tt):
    M = L // 2
    per = M // tt
    tile = pl.BlockSpec((tt, M), lambda b, t: (t, 0))
    seq = pl.BlockSpec((M, W2_D), lambda b, t: (b, 0))
    rows_in = pl.BlockSpec((tt, W2_D), lambda b, t: ((blk0 + b) * per + t, 0))
    return pl.pallas_call(
        functools.partial(_hy_inv_kernel, tt=tt),
        grid=(nb, per),
        in_specs=[tile, tile, seq, seq,
                  pl.BlockSpec((None, 8, W2_D), lambda b, t: (b, 0, 0)),
                  rows_in, rows_in,
                  pl.BlockSpec((1, W2_D), lambda b, t: (0, 0))],
        out_specs=pl.BlockSpec((2 * tt, W_D), lambda b, t: (b * per + t, 0)),
        out_shape=jax.ShapeDtypeStruct((nb * L, W_D), BF16),
        scratch_shapes=[pltpu.VMEM((W_D // LANES, 2 * tt, LANES), F32)],
        compiler_params=_params("arbitrary", "arbitrary"),
        name="hy_inverse",
    )(cmat, smat, vc, vs, yn, x02, w2, jnp.concatenate([bias, bias]).reshape(1, W2_D))


def _hyena_long_conv(x02, w2, L, nb, blk0, filt_params, bias):
    M = L // 2
    tk = min(M, 256)
    cmat, smat = _dft_matrices(M)
    twc, tws = _twiddles(L)
    fs2, fd2, hn = _hy_filter(L, *filt_params)
    h4 = _hy_spectrum(cmat, smat, fs2, fd2, twc, tws, tk)
    vc, vs, yn = _hy_forward(cmat, smat, w2, h4, twc, tws, hn, L, nb, blk0, tk)
    return _hy_inverse(cmat, smat, vc, vs, yn, x02, w2, bias, L, nb, blk0, tk)


def kernel(x_prompt, x_sample, cache_k, cache_v, c, c_ctx, ada_w, ada_b, norm_mix, norm_ffn,
           e_w_in, e_conv_a, e_q_norm, e_k_norm, e_rpb, e_w_out,
           o_w_in, o_conf_dw, o_conf_dw_b, o_conf_ln_g, o_conf_ln_b, o_hy_short, o_hy_short_b,
           o_hy_w1, o_hy_b1, o_hy_f1, o_hy_w2, o_hy_b2, o_hy_f2, o_hy_w3, o_hy_bias, o_w_out,
           ffn_in, ffn_conv, ffn_out):
    x_parts = (x_prompt.reshape(N_P, D_MODEL), x_sample.reshape(N_S, D_MODEL))
    cvec = jnp.concatenate([c_ctx[None, :], c, jnp.zeros((N_SEG_PAD - 1 - DEC_BATCH, D_MODEL), F32)], axis=0)
    mod = _adaln(cvec, ada_w, ada_b)
    ks_new, vs_new = [], []
    for layer in range(DEPTH):
        j = layer // 2
        last = layer == DEPTH - 1
        h = _norm_mod_call(x_parts, norm_mix[layer], mod, layer, 0)
        if layer % 2 == 0:
            proj, w_out = _proj(h, e_w_in[j], e_w_out[j])
            ya = (_short_gated_conv(proj, e_conv_a[j]),)
            yb_p, k_p, v_p = _ctx_attention(proj, e_q_norm[j], e_k_norm[j])
            yb_s = _na_attention(proj, cache_k[:, j], cache_v[:, j], e_rpb[j], e_q_norm[j], e_k_norm[j])
            yb = (yb_p, yb_s)
            ks_new.append(k_p.reshape(BATCH, SEQ, NA_HEADS, NA_HEAD_DIM))
            vs_new.append(v_p.reshape(BATCH, SEQ, NA_HEADS, NA_HEAD_DIM))
        else:
            proj, w_out = _proj(h, o_w_in[j], o_w_out[j])
            ya = (_conformer(proj, o_conf_dw[j], o_conf_dw_b[j], o_conf_ln_g[j], o_conf_ln_b[j]),)
            x0, w = _hy_short(proj, o_hy_short[j], o_hy_short_b[j])
            fp = (o_hy_w1[j], o_hy_b1[j], o_hy_f1[j], o_hy_w2[j], o_hy_b2[j], o_hy_f2[j], o_hy_w3[j])
            z_p = _hyena_long_conv(x0, w, SEQ, BATCH, 0, fp, o_hy_bias[j])
            z_s = _hyena_long_conv(x0, w, DEC_SEQ, DEC_BATCH, N_P // DEC_SEQ, fp, o_hy_bias[j])
            yb = (z_p, z_s)
        x, h_ffn = _out_proj(ya, yb, w_out, x_parts, mod, layer, norm_ffn[layer])
        u, w_ffn_out = _ffn_in(h_ffn, ffn_in, ffn_conv, ffn_out, layer)
        if last:
            x_parts = (_ffn_out(u, w_ffn_out, x, mod, layer, 0, N_P),
                       _ffn_out(u, w_ffn_out, x, mod, layer, N_P, N_S))
        else:
            x_parts = (_ffn_out(u, w_ffn_out, x, mod, layer),)
    xp = x_parts[0].reshape(BATCH, SEQ, D_MODEL)
    xs = x_parts[1].reshape(DEC_BATCH, DEC_SEQ, D_MODEL)
    return (xp, xs, jnp.stack(ks_new, axis=1), jnp.stack(vs_new, axis=1))
```

```python
import functools
import math

import jax
import jax.numpy as jnp
from jax import lax
from jax.experimental import pallas as pl
from jax.experimental.pallas import tpu as pltpu

D_MODEL = 2048
BATCH = 16
SEQ = 256
DEPTH = 2
DEC_BATCH = 2
DEC_SEQ = 4096
PAST_LEN = 512
GRID_W = 64
W_A = 1024
NA_HEADS = 8
NA_HEAD_DIM = 128
W_B = NA_HEADS * NA_HEAD_DIM
W_C = 1024
W_D = 1024
NA_WIN_R = 8
NA_WIN_C = 16
SHORT_K = 3
CONF_K = 31
D_FF = 5632
HY_EMB = 33
HY_HIDDEN = 64
HY_FAST_PCT = 0.3
HY_SLOW_PCT = 1.5
HY_TARGET = 1e-2
EPS = 1e-6
NEG_INF = -1e30

N_P = BATCH * SEQ
N_S = DEC_BATCH * DEC_SEQ
N_TOK = N_P + N_S
SEG = DEC_SEQ
N_SEG_PAD = 8
N_MOD = 6 * D_MODEL
HALO = 16
LANES = 128
VMEM_LIMIT_BYTES = 60 * 1024 * 1024
BF16 = jnp.bfloat16
F32 = jnp.float32


def _params(*sem):
    return pltpu.CompilerParams(dimension_semantics=sem, vmem_limit_bytes=VMEM_LIMIT_BYTES)


def _seq_len(row0):
    return jnp.where(row0 < N_P, SEQ, DEC_SEQ)


def _mod_row(mod_ref, row0):
    return mod_ref[pl.ds(row0 // SEG, 1), :]


def _mod_spec(layer, blk, tn, index_pos):
    per = D_MODEL // tn

    def imap(*idx):
        j = idx[index_pos] if index_pos is not None else 0
        return (layer, 0, blk * per + j)

    return pl.BlockSpec((None, N_SEG_PAD, tn), imap)


def _adaln_kernel(c_ref, w_ref, b_ref, o_ref):
    c = c_ref[...]
    s = (c * jax.nn.sigmoid(c)).astype(BF16)
    o_ref[...] = jnp.dot(s, w_ref[...].astype(BF16), preferred_element_type=F32) + b_ref[...]


def _adaln(cvec, ada_w, ada_b, tn=1024):
    return pl.pallas_call(
        _adaln_kernel,
        grid=(DEPTH, N_MOD // tn),
        in_specs=[pl.BlockSpec((N_SEG_PAD, D_MODEL), lambda l, j: (0, 0)),
                  pl.BlockSpec((None, D_MODEL, tn), lambda l, j: (l, 0, j)),
                  pl.BlockSpec((None, 1, tn), lambda l, j: (l, 0, j))],
        out_specs=pl.BlockSpec((None, N_SEG_PAD, tn), lambda l, j: (l, 0, j)),
        out_shape=jax.ShapeDtypeStruct((DEPTH, N_SEG_PAD, N_MOD), F32),
        compiler_params=_params("arbitrary", "arbitrary"),
        name="adaln",
    )(cvec, ada_w, ada_b.reshape(DEPTH, 1, N_MOD))


def _norm_mod(x, g, shift, scale):
    y = x * lax.rsqrt(jnp.mean(x * x, axis=-1, keepdims=True) + EPS) * g
    return y * (1.0 + scale) + shift


def _stream_specs(x_parts, tr, width, pos):
    if len(x_parts) == 1:
        return [pl.BlockSpec((tr, width), lambda *idx: (idx[pos], 0))]
    n_pt = N_P // tr
    return [pl.BlockSpec((tr, width), lambda *idx: (jnp.minimum(idx[pos], n_pt - 1), 0)),
            pl.BlockSpec((tr, width), lambda *idx: (jnp.maximum(idx[pos] - n_pt, 0), 0))]


def _stream_tile(x_refs, row0):
    if len(x_refs) == 1:
        return x_refs[0][...]
    return jnp.where(row0 < N_P, x_refs[0][...], x_refs[1][...])


def _norm_mod_kernel(*refs, tr, n_parts):
    x_refs, (g_ref, sh_ref, sc_ref, o_ref) = refs[:n_parts], refs[n_parts:]
    row0 = pl.program_id(0) * tr
    x = _stream_tile(x_refs, row0)
    o_ref[...] = _norm_mod(x, g_ref[...], _mod_row(sh_ref, row0), _mod_row(sc_ref, row0)).astype(BF16)


def _norm_mod_call(x_parts, g, mod, layer, blk, tr=512):
    return pl.pallas_call(
        functools.partial(_norm_mod_kernel, tr=tr, n_parts=len(x_parts)),
        grid=(N_TOK // tr,),
        in_specs=_stream_specs(x_parts, tr, D_MODEL, 0) + [
            pl.BlockSpec((1, D_MODEL), lambda i: (0, 0)),
            _mod_spec(layer, blk, D_MODEL, None),
            _mod_spec(layer, blk + 1, D_MODEL, None)],
        out_specs=pl.BlockSpec((tr, D_MODEL), lambda i: (i, 0)),
        out_shape=jax.ShapeDtypeStruct((N_TOK, D_MODEL), BF16),
        compiler_params=_params("arbitrary"),
        name="norm_mod",
    )(*x_parts, g.reshape(1, D_MODEL), mod, mod)


CAST_ROWS = 128


def _proj_kernel(h_ref, w_ref, wo_ref, o_ref, wob_ref, wb_ref):
    @pl.when(pl.program_id(1) == 0)
    def _():
        wb_ref[...] = w_ref[...].astype(BF16)

    wob_ref[...] = wo_ref[...].astype(BF16)
    o_ref[...] = jnp.dot(h_ref[...], wb_ref[...], preferred_element_type=F32).astype(o_ref.dtype)


def _proj(h, w, w_out, tm=2048, tn=1024):
    m, k = h.shape
    n = w.shape[1]
    k2, n2 = w_out.shape
    n_i = m // tm
    n_cast = k2 // CAST_ROWS
    assert n_cast * CAST_ROWS == k2 and n_cast <= (n // tn) * n_i

    def cast_block(j, i):
        return (jnp.minimum(j * n_i + i, n_cast - 1), 0)

    return pl.pallas_call(
        _proj_kernel,
        grid=(n // tn, n_i),
        in_specs=[pl.BlockSpec((tm, k), lambda j, i: (i, 0)),
                  pl.BlockSpec((k, tn), lambda j, i: (0, j)),
                  pl.BlockSpec((CAST_ROWS, n2), cast_block)],
        out_specs=[pl.BlockSpec((tm, tn), lambda j, i: (i, j)),
                   pl.BlockSpec((CAST_ROWS, n2), cast_block)],
        out_shape=[jax.ShapeDtypeStruct((m, n), BF16),
                   jax.ShapeDtypeStruct((k2, n2), BF16)],
        scratch_shapes=[pltpu.VMEM((k, tn), BF16)],
        compiler_params=_params("arbitrary", "arbitrary"),
        name="proj",
    )(h, w, w_out)


def _shift_rows(p, prev_row, next_row, row0, seq_len):
    n = p.shape[0]
    ridx = lax.broadcasted_iota(jnp.int32, (n, 1), 0)
    pos = (row0 + ridx) & (seq_len - 1)
    p_prev = jnp.where(ridx == 0, prev_row, pltpu.roll(p, 1, 0))
    p_prev = jnp.where(pos == 0, 0.0, p_prev)
    p_next = jnp.where(ridx == n - 1, next_row, pltpu.roll(p, n - 1, 0))
    p_next = jnp.where(pos == seq_len - 1, 0.0, p_next)
    return p_prev, p_next


def _sconv_kernel(ab_ref, ac_ref, ax_ref, pc_ref, px_ref, nc_ref, nx_ref, w_ref, o_ref, *, tr):
    row0 = pl.program_id(0) * tr
    p = ac_ref[...].astype(F32) * ax_ref[...].astype(F32)
    prev_row = pc_ref[HALO - 1:HALO, :].astype(F32) * px_ref[HALO - 1:HALO, :].astype(F32)
    next_row = nc_ref[0:1, :].astype(F32) * nx_ref[0:1, :].astype(F32)
    p_prev, p_next = _shift_rows(p, prev_row, next_row, row0, _seq_len(row0))
    w = w_ref[...]
    conv = w[0:1, :] * p_prev + w[1:2, :] * p + w[2:3, :] * p_next
    o_ref[...] = (ab_ref[...].astype(F32) * conv).astype(BF16)


def _halo_specs(tr, width, col):
    per = tr // HALO
    last = N_TOK // HALO - 1
    prev = pl.BlockSpec((HALO, width), lambda i: (jnp.maximum(i * per - 1, 0), col))
    nxt = pl.BlockSpec((HALO, width), lambda i: (jnp.minimum((i + 1) * per, last), col))
    return prev, nxt


def _short_gated_conv(proj, conv_a, tr=512):
    pc, nc = _halo_specs(tr, W_A, 1)
    px, nx = _halo_specs(tr, W_A, 2)
    return pl.pallas_call(
        functools.partial(_sconv_kernel, tr=tr),
        grid=(N_TOK // tr,),
        in_specs=[pl.BlockSpec((tr, W_A), lambda i: (i, 0)),
                  pl.BlockSpec((tr, W_A), lambda i: (i, 1)),
                  pl.BlockSpec((tr, W_A), lambda i: (i, 2)),
                  pc, px, nc, nx,
                  pl.BlockSpec((SHORT_K, W_A), lambda i: (0, 0))],
        out_specs=pl.BlockSpec((tr, W_A), lambda i: (i, 0)),
        out_shape=jax.ShapeDtypeStruct((N_TOK, W_A), BF16),
        compiler_params=_params("arbitrary"),
        name="short_gated_conv",
    )(proj, proj, proj, proj, proj, proj, proj, conv_a)


ROWS = DEC_SEQ // GRID_W
NA_QROWS = 8
NA_KROWS = 16
NA_NQ = NA_QROWS * GRID_W
NA_NK = NA_KROWS * GRID_W
NA_STEPS = ROWS // NA_QROWS
QKV_COL0 = 3 * W_A // NA_HEAD_DIM
HEAD_SCALE = NA_HEAD_DIM ** -0.5
LOG2E = math.log2(math.e)
NA_CHUNK = 16
NT_DIMS = (((1,), (1,)), ((), ()))


def _head_rmsnorm(x, g):
    return x * lax.rsqrt(jnp.mean(x * x, axis=-1, keepdims=True) + EPS) * g


def _na_key_row0(j):
    return jnp.clip(j * NA_QROWS - NA_WIN_R // 2, 0, ROWS - NA_KROWS)


NA_SLOT_LO = NA_QROWS
NA_SLOTS = 2 * NA_KROWS


def _na_geometry(j):
    kr0 = min(max(j * NA_QROWS - NA_WIN_R // 2, 0), ROWS - NA_KROWS)
    per_a = []
    for a in range(NA_QROWS):
        r = j * NA_QROWS + a
        r0 = min(max(r - NA_WIN_R // 2, 0), ROWS - NA_WIN_R)
        per_a.append((kr0 - r + NA_WIN_R - 1 + NA_SLOT_LO, r0 - kr0))
    return kr0, per_a


def _na_bias_slots(rpb):
    n_dr, n_dc = rpb.shape[1], rpb.shape[2]
    out = jax.ShapeDtypeStruct((NA_HEADS, GRID_W, NA_SLOTS * GRID_W), F32)
    spec = pl.BlockSpec((None, GRID_W, NA_SLOTS * GRID_W), lambda h: (h, 0, 0))
    return pl.pallas_call(
        functools.partial(_na_slots_kernel, n_dr=n_dr, n_dc=n_dc),
        grid=(NA_HEADS,),
        in_specs=[pl.BlockSpec(memory_space=pltpu.SMEM)],
        out_specs=[spec, spec],
        out_shape=[out, out],
        compiler_params=_params("arbitrary"),
        name="na_bias_slots",
    )(rpb.astype(F32).reshape(NA_HEADS, n_dr * n_dc))


def _na_slots_kernel(rpb_ref, even_ref, odd_ref, *, n_dr, n_dc):
    h = pl.program_id(0)
    pair = 2 * GRID_W
    qc = lax.broadcasted_iota(jnp.int32, (GRID_W, pair), 0)
    lane = lax.broadcasted_iota(jnp.int32, (GRID_W, pair), 1)
    kc = lane & (GRID_W - 1)
    delta = kc - qc + NA_WIN_C - 1
    cs = jnp.clip(qc - NA_WIN_C // 2, 0, GRID_W - NA_WIN_C)
    col_ok = (kc >= cs) & (kc < cs + NA_WIN_C)
    upper = lax.broadcasted_iota(jnp.int32, (1, pair), 1) >= GRID_W
    for shift, ref in ((0, even_ref), (1, odd_ref)):
        for t in range(NA_SLOTS // 2):
            drs = [2 * t + shift + half - NA_SLOT_LO for half in (0, 1)]
            acc = jnp.full((GRID_W, pair), NEG_INF, F32)
            if any(0 <= dr < n_dr for dr in drs):
                for dc in range(n_dc):
                    lo, hi = [rpb_ref[h, dr * n_dc + dc] if 0 <= dr < n_dr else NEG_INF for dr in drs]
                    acc = jnp.where(delta == dc, jnp.where(upper, hi, lo), acc)
                acc = jnp.where(col_ok, acc, NEG_INF)
            ref[:, t * pair:(t + 1) * pair] = acc


def _na_build_bias(j_static, even_ref, odd_ref, bias_ref):
    _, per_a = _na_geometry(j_static)
    key_row = lax.broadcasted_iota(jnp.int32, (1, NA_NK), 1) // GRID_W
    for a, (m, lo) in enumerate(per_a):
        src = even_ref if m % 2 == 0 else odd_ref
        start = (m - m % 2) * GRID_W
        slab = src[:, start:start + NA_NK] * LOG2E
        ok = (key_row >= lo) & (key_row < lo + NA_WIN_R)
        bias_ref[a * GRID_W:(a + 1) * GRID_W, :] = jnp.where(ok, slab, NEG_INF)


def _with_ones_column(v):
    lane = lax.broadcasted_iota(jnp.int32, v.shape, 1)
    return jnp.concatenate([v.astype(BF16), jnp.where(lane == 0, 1.0, 0.0).astype(BF16)], axis=1)


def _na_kernel(q_ref, k_ref, v_ref, ck_ref, cv_ref, even_ref, odd_ref, qg_ref, kg_ref, o_ref,
               kn_ref, vx_ref, kall_ref, vall_ref, bias_ref, s_ref, p_ref):
    j = pl.program_id(2)
    hd = NA_HEAD_DIM

    @pl.when(j == 0)
    def _():
        kn_ref[...] = _head_rmsnorm(k_ref[...].astype(F32), kg_ref[...]).astype(BF16)
        vx_ref[...] = _with_ones_column(v_ref[...])
        kall_ref[NA_NK:, :] = ck_ref[...].astype(BF16)
        vall_ref[NA_NK:, :] = _with_ones_column(cv_ref[...])

    for j_static in (0, 1, NA_STEPS - 1):
        @pl.when(j == j_static)
        def _(j_static=j_static):
            _na_build_bias(j_static, even_ref, odd_ref, bias_ref)

    q = (_head_rmsnorm(q_ref[...].astype(F32), qg_ref[...]) * (HEAD_SCALE * LOG2E)).astype(BF16)
    tok0 = pl.multiple_of(_na_key_row0(j) * GRID_W, GRID_W)
    kall_ref[:NA_NK, :] = kn_ref[pl.ds(tok0, NA_NK), :]
    vall_ref[:NA_NK, :] = vx_ref[pl.ds(tok0, NA_NK), :]
    s_ref[...] = lax.dot_general(q, kall_ref[...], NT_DIMS, preferred_element_type=F32)
    for c in range(NA_NQ // NA_CHUNK):
        rows = slice(c * NA_CHUNK, (c + 1) * NA_CHUNK)
        s_loc = s_ref[rows, :NA_NK] + bias_ref[rows, :]
        s_ctx = s_ref[rows, NA_NK:]
        m = functools.reduce(jnp.maximum, [blk[:, t:t + hd] for blk in (s_loc, s_ctx)
                                           for t in range(0, blk.shape[1], hd)])
        m = jnp.max(m, axis=-1, keepdims=True)
        p_ref[rows, :NA_NK] = jnp.exp2(s_loc - m).astype(BF16)
        p_ref[rows, NA_NK:] = jnp.exp2(s_ctx - m).astype(BF16)
    o = jnp.dot(p_ref[...], vall_ref[...], preferred_element_type=F32)
    o_ref[...] = (o[:, :hd] / o[:, hd:hd + 1]).astype(BF16)


def _na_attention(proj, cache_k, cache_v, rpb, q_norm, k_norm):
    assert NA_STEPS >= 3 and all(_na_geometry(j)[1] == _na_geometry(1)[1] for j in range(1, NA_STEPS - 1))
    even, odd = _na_bias_slots(rpb)
    ck = cache_k.reshape(DEC_BATCH, PAST_LEN, W_B)
    cv = cache_v.reshape(DEC_BATCH, PAST_LEN, W_B)
    hd = NA_HEAD_DIM
    ctx = pl.BlockSpec((None, PAST_LEN, hd), lambda b, h, j: (b, 0, h))
    q_blk0 = N_P // NA_NQ
    kv_blk0 = N_P // DEC_SEQ
    slots = pl.BlockSpec((None, GRID_W, NA_SLOTS * GRID_W), lambda b, h, j: (h, 0, 0))
    return pl.pallas_call(
        _na_kernel,
        grid=(DEC_BATCH, NA_HEADS, NA_STEPS),
        in_specs=[
            pl.BlockSpec((NA_NQ, hd), lambda b, h, j: (q_blk0 + b * NA_STEPS + j, QKV_COL0 + h)),
            pl.BlockSpec((DEC_SEQ, hd), lambda b, h, j: (kv_blk0 + b, QKV_COL0 + NA_HEADS + h)),
            pl.BlockSpec((DEC_SEQ, hd), lambda b, h, j: (kv_blk0 + b, QKV_COL0 + 2 * NA_HEADS + h)),
            ctx, ctx,
            slots, slots,
            pl.BlockSpec((1, hd), lambda b, h, j: (0, 0)),
            pl.BlockSpec((1, hd), lambda b, h, j: (0, 0)),
        ],
        out_specs=pl.BlockSpec((NA_NQ, hd), lambda b, h, j: (b * NA_STEPS + j, h)),
        out_shape=jax.ShapeDtypeStruct((N_S, W_B), BF16),
        scratch_shapes=[pltpu.VMEM((DEC_SEQ, hd), BF16),
                        pltpu.VMEM((DEC_SEQ, 2 * hd), BF16),
                        pltpu.VMEM((NA_NK + PAST_LEN, hd), BF16),
                        pltpu.VMEM((NA_NK + PAST_LEN, 2 * hd), BF16),
                        pltpu.VMEM((NA_NQ, NA_NK), F32),
                        pltpu.VMEM((NA_NQ, NA_NK + PAST_LEN), F32),
                        pltpu.VMEM((NA_NQ, NA_NK + PAST_LEN), BF16)],
        compiler_params=_params("arbitrary", "arbitrary", "arbitrary"),
        name="na_attention",
    )(proj, proj, proj, ck, cv, even, odd, q_norm.reshape(1, hd), k_norm.reshape(1, hd))


def _ctx_kernel(q_ref, k_ref, v_ref, qg_ref, kg_ref, o_ref, kn_ref, vo_ref):
    vo_ref[...] = v_ref[...].astype(F32)
    for h in range(NA_HEADS):
        cols = slice(h * NA_HEAD_DIM, (h + 1) * NA_HEAD_DIM)
        q = (_head_rmsnorm(q_ref[:, cols].astype(F32), qg_ref[...]) * HEAD_SCALE).astype(BF16)
        kn = _head_rmsnorm(k_ref[:, cols].astype(F32), kg_ref[...])
        kn_ref[:, cols] = kn
        s = lax.dot_general(q, kn.astype(BF16), NT_DIMS, preferred_element_type=F32)
        p = jnp.exp(s - jnp.max(s, axis=-1, keepdims=True))
        denom = jnp.sum(p, axis=-1, keepdims=True)
        o = jnp.dot(p.astype(BF16), v_ref[:, cols], preferred_element_type=F32)
        o_ref[:, cols] = (o / denom).astype(BF16)


def _ctx_attention(proj, q_norm, k_norm):
    hd = NA_HEAD_DIM
    col0 = 3 * W_A // W_B
    ospec = pl.BlockSpec((SEQ, W_B), lambda b: (b, 0))
    return pl.pallas_call(
        _ctx_kernel,
        grid=(BATCH,),
        in_specs=[
            pl.BlockSpec((SEQ, W_B), lambda b: (b, col0)),
            pl.BlockSpec((SEQ, W_B), lambda b: (b, col0 + 1)),
            pl.BlockSpec((SEQ, W_B), lambda b: (b, col0 + 2)),
            pl.BlockSpec((1, hd), lambda b: (0, 0)),
            pl.BlockSpec((1, hd), lambda b: (0, 0)),
        ],
        out_specs=[ospec, ospec, ospec],
        out_shape=[jax.ShapeDtypeStruct((N_P, W_B), BF16),
                   jax.ShapeDtypeStruct((N_P, W_B), F32),
                   jax.ShapeDtypeStruct((N_P, W_B), F32)],
        compiler_params=_params("arbitrary"),
        name="ctx_attention",
    )(proj, proj, proj, q_norm.reshape(1, hd), k_norm.reshape(1, hd))


def _out_proj_kernel(*refs, tm, ka, n_ya, n_yb, n_x):
    ya_refs, refs = refs[:n_ya], refs[n_ya:]
    yb_refs, refs = refs[:n_yb], refs[n_yb:]
    w_ref, refs = refs[0], refs[1:]
    x_refs, refs = refs[:n_x], refs[n_x:]
    gate_ref, g_ref, sh_ref, sc_ref, xo_ref, ho_ref = refs
    row0 = pl.program_id(0) * tm
    o = (jnp.dot(_stream_tile(ya_refs, row0), w_ref[0:ka, :], preferred_element_type=F32)
         + jnp.dot(_stream_tile(yb_refs, row0), w_ref[ka:, :], preferred_element_type=F32))
    x_new = _stream_tile(x_refs, row0) + _mod_row(gate_ref, row0) * o
    xo_ref[...] = x_new
    ho_ref[...] = _norm_mod(x_new, g_ref[...], _mod_row(sh_ref, row0), _mod_row(sc_ref, row0)).astype(BF16)


def _out_proj(ya_parts, yb_parts, w_bf16, x_parts, mod, layer, g_ffn, tm=512):
    ka, kb = ya_parts[0].shape[1], yb_parts[0].shape[1]
    return pl.pallas_call(
        functools.partial(_out_proj_kernel, tm=tm, ka=ka, n_ya=len(ya_parts), n_yb=len(yb_parts),
                          n_x=len(x_parts)),
        grid=(N_TOK // tm,),
        in_specs=_stream_specs(ya_parts, tm, ka, 0)
        + _stream_specs(yb_parts, tm, kb, 0)
        + [pl.BlockSpec((ka + kb, D_MODEL), lambda i: (0, 0))]
        + _stream_specs(x_parts, tm, D_MODEL, 0) + [
            _mod_spec(layer, 2, D_MODEL, None),
            pl.BlockSpec((1, D_MODEL), lambda i: (0, 0)),
            _mod_spec(layer, 3, D_MODEL, None),
            _mod_spec(layer, 4, D_MODEL, None)],
        out_specs=[pl.BlockSpec((tm, D_MODEL), lambda i: (i, 0)),
                   pl.BlockSpec((tm, D_MODEL), lambda i: (i, 0))],
        out_shape=[jax.ShapeDtypeStruct((N_TOK, D_MODEL), F32),
                   jax.ShapeDtypeStruct((N_TOK, D_MODEL), BF16)],
        compiler_params=_params("arbitrary"),
        name="out_proj",
    )(*ya_parts, *yb_parts, w_bf16, *x_parts, mod, g_ffn.reshape(1, D_MODEL), mod, mod)


GELU_C = math.sqrt(2.0 / math.pi)


def _gelu_tanh(x):
    return 0.5 * x * (1.0 + jnp.tanh(GELU_C * (x + 0.044715 * (x * x * x))))


def _ffn_in_kernel(h_ref, hp_ref, hn_ref, wa_ref, wg_ref, cw_ref, wo_ref, o_ref, wob_ref, wab_ref, wgb_ref,
                   *, tm, tn):
    i = pl.program_id(1)

    @pl.when(i == 0)
    def _():
        wab_ref[...] = wa_ref[...].astype(BF16)
        wgb_ref[...] = wg_ref[...].astype(BF16)

    wob_ref[...] = wo_ref[...].astype(BF16)

    row0 = i * tm
    seq_len = _seq_len(row0)
    h = h_ref[...]
    h_ext = jnp.concatenate([hp_ref[...], h, hn_ref[...]], axis=0)
    n = tm + 2 * HALO
    pos = (row0 + lax.broadcasted_iota(jnp.int32, (tm, 1), 0)) & (seq_len - 1)
    first = pos == 0
    last = pos == seq_len - 1
    a_ext = jnp.dot(h_ext, wab_ref[...], preferred_element_type=F32)
    g = jnp.dot(h, wgb_ref[...], preferred_element_type=F32)
    a_mid = a_ext[HALO:HALO + tm, :]
    a_prev = jnp.where(first, 0.0, pltpu.roll(a_ext, 1, 0)[HALO:HALO + tm, :])
    a_next = jnp.where(last, 0.0, pltpu.roll(a_ext, n - 1, 0)[HALO:HALO + tm, :])
    cw = cw_ref[...]
    conv = cw[0:1, :] * a_prev + cw[1:2, :] * a_mid + cw[2:3, :] * a_next
    o_ref[...] = (_gelu_tanh(conv) * g).astype(BF16)


def _ffn_in(h, w_in, conv, w_out, layer, tm=1024, tn=512):
    per = tm // HALO
    last = N_TOK // HALO - 1
    ng = D_FF // tn
    n_i = N_TOK // tm
    per_j = D_FF // ng // CAST_ROWS
    assert per_j <= n_i and per_j * CAST_ROWS * ng == D_FF

    def cast_block(j, i):
        return j * per_j + jnp.minimum(i, per_j - 1)

    return pl.pallas_call(
        functools.partial(_ffn_in_kernel, tm=tm, tn=tn),
        grid=(ng, n_i),
        in_specs=[pl.BlockSpec((tm, D_MODEL), lambda j, i: (i, 0)),
                  pl.BlockSpec((HALO, D_MODEL), lambda j, i: (jnp.maximum(i * per - 1, 0), 0)),
                  pl.BlockSpec((HALO, D_MODEL), lambda j, i: (jnp.minimum((i + 1) * per, last), 0)),
                  pl.BlockSpec((None, D_MODEL, tn), lambda j, i: (layer, 0, j)),
                  pl.BlockSpec((None, D_MODEL, tn), lambda j, i: (layer, 0, ng + j)),
                  pl.BlockSpec((None, SHORT_K, tn), lambda j, i: (layer, 0, j)),
                  pl.BlockSpec((None, CAST_ROWS, D_MODEL), lambda j, i: (layer, cast_block(j, i), 0))],
        out_specs=[pl.BlockSpec((tm, tn), lambda j, i: (i, j)),
                   pl.BlockSpec((CAST_ROWS, D_MODEL), lambda j, i: (cast_block(j, i), 0))],
        out_shape=[jax.ShapeDtypeStruct((N_TOK, D_FF), BF16),
                   jax.ShapeDtypeStruct((D_FF, D_MODEL), BF16)],
        scratch_shapes=[pltpu.VMEM((D_MODEL, tn), BF16), pltpu.VMEM((D_MODEL, tn), BF16)],
        compiler_params=_params("arbitrary", "arbitrary"),
        name="ffn_in",
    )(h, h, h, w_in, w_in, conv, w_out)


def _ffn_out_kernel(u_ref, w_ref, x_ref, gate_ref, o_ref, *, tm, row_start):
    row0 = row_start + pl.program_id(1) * tm
    o = jnp.dot(u_ref[...], w_ref[...], preferred_element_type=F32)
    o_ref[...] = x_ref[...] + _mod_row(gate_ref, row0) * o


def _ffn_out(u, w_bf16, x, mod, layer, row_start=0, n_rows=N_TOK, tm=512, tn=1024):
    t0 = row_start // tm
    return pl.pallas_call(
        functools.partial(_ffn_out_kernel, tm=tm, row_start=row_start),
        grid=(D_MODEL // tn, n_rows // tm),
        in_specs=[pl.BlockSpec((tm, D_FF), lambda j, i: (t0 + i, 0)),
                  pl.BlockSpec((D_FF, tn), lambda j, i: (0, j)),
                  pl.BlockSpec((tm, tn), lambda j, i: (t0 + i, j)),
                  _mod_spec(layer, 5, tn, 0)],
        out_specs=pl.BlockSpec((tm, tn), lambda j, i: (i, j)),
        out_shape=jax.ShapeDtypeStruct((n_rows, D_MODEL), F32),
        compiler_params=_params("arbitrary", "arbitrary"),
        name="ffn_out",
    )(u, w_bf16, x, mod)


CONF_TR = 256
CONF_PAD = CONF_K // 2
CONF_RB = 128
CONF_CB = 128
CONF_LN_ROWS = 16


def _conformer_tile(ca_ref, cg_ref, pa_ref, pg_ref, na_ref, ng_ref, w_ref, b_ref, lg_ref, lb_ref, o_ref,
                    u_ref, v_ref, tile):
    tr = CONF_TR
    row0 = tile * tr
    seq_len = _seq_len(row0)
    pos0 = row0 & (seq_len - 1)

    def glu(a_ref, g_ref):
        return a_ref[...].astype(F32) * jax.nn.sigmoid(g_ref[...].astype(F32))

    u_ref[0:HALO, :] = jnp.where(pos0 == 0, 0.0, glu(pa_ref, pg_ref))
    u_ref[HALO:HALO + tr, :] = glu(ca_ref, cg_ref)
    u_ref[HALO + tr:, :] = jnp.where(pos0 + tr == seq_len, 0.0, glu(na_ref, ng_ref))
    base = HALO - CONF_PAD
    n_in = CONF_RB + 2 * HALO
    for cb in range(W_C // CONF_CB):
        cols = slice(cb * CONF_CB, (cb + 1) * CONF_CB)
        w = w_ref[:, cols]
        for rb in range(tr // CONF_RB):
            u_blk = u_ref[rb * CONF_RB:rb * CONF_RB + n_in, cols]
            acc = jnp.zeros((CONF_RB, CONF_CB), F32) + b_ref[:, cols]
            for r in range(8):
                u_r = pltpu.roll(u_blk, n_in - (base + r), 0)
                for a in range(len(range(r, CONF_K, 8))):
                    k = 8 * a + r
                    acc = acc + w[k:k + 1, :] * u_r[8 * a:8 * a + CONF_RB, :]
            v_ref[rb * CONF_RB:(rb + 1) * CONF_RB, cols] = acc
    for c in range(tr // CONF_LN_ROWS):
        rows = slice(c * CONF_LN_ROWS, (c + 1) * CONF_LN_ROWS)
        v = v_ref[rows, :]
        mu = jnp.mean(v, axis=-1, keepdims=True)
        d = v - mu
        var = jnp.mean(d * d, axis=-1, keepdims=True)
        y = d * lax.rsqrt(var + EPS) * lg_ref[...] + lb_ref[...]
        o_ref[rows, :] = (y * jax.nn.sigmoid(y)).astype(BF16)


def _conformer_kernel(*refs):
    _conformer_tile(*refs, pl.program_id(0))


def _conformer(proj, dw, dw_b, ln_g, ln_b):
    pa, na = _halo_specs(CONF_TR, W_C, 0)
    pg, ng = _halo_specs(CONF_TR, W_C, 1)
    row = pl.BlockSpec((1, W_C), lambda i: (0, 0))
    return pl.pallas_call(
        _conformer_kernel,
        grid=(N_TOK // CONF_TR,),
        in_specs=[pl.BlockSpec((CONF_TR, W_C), lambda i: (i, 0)),
                  pl.BlockSpec((CONF_TR, W_C), lambda i: (i, 1)),
                  pa, pg, na, ng,
                  pl.BlockSpec((CONF_K, W_C), lambda i: (0, 0)), row, row, row],
        out_specs=pl.BlockSpec((CONF_TR, W_C), lambda i: (i, 0)),
        out_shape=jax.ShapeDtypeStruct((N_TOK, W_C), BF16),
        scratch_shapes=[pltpu.VMEM((CONF_TR + 2 * HALO, W_C), F32), pltpu.VMEM((CONF_TR, W_C), F32)],
        compiler_params=_params("arbitrary"),
        name="conformer",
    )(proj, proj, proj, proj, proj, proj, dw, dw_b.reshape(1, W_C), ln_g.reshape(1, W_C), ln_b.reshape(1, W_C))


HY_COL0 = 2 * W_C // W_D


def _hy_short_kernel(*refs, tr):
    (x0_ref, x1_ref, vv_ref, p0_ref, p1_ref, pv_ref, n0_ref, n1_ref, nv_ref,
     w0_ref, w1_ref, wv_ref, b0_ref, b1_ref, bv_ref, x0o_ref, wo_ref, tmp_ref) = refs
    row0 = pl.program_id(0) * tr
    seq_len = _seq_len(row0)

    def conv(c_ref, p_ref, n_ref, w_ref, b_ref):
        p = c_ref[...].astype(F32)
        p_prev, p_next = _shift_rows(p, p_ref[HALO - 1:HALO, :].astype(F32), n_ref[0:1, :].astype(F32),
                                     row0, seq_len)
        w = w_ref[...]
        return w[0:1, :] * p_prev + w[1:2, :] * p + w[2:3, :] * p_next + b_ref[...]

    _store_even_odd(x0o_ref, tmp_ref, conv(x0_ref, p0_ref, n0_ref, w0_ref, b0_ref))
    x1 = conv(x1_ref, p1_ref, n1_ref, w1_ref, b1_ref)
    vv = conv(vv_ref, pv_ref, nv_ref, wv_ref, bv_ref)
    _store_even_odd(wo_ref, tmp_ref, vv * x1)


def _store_even_odd(o_ref, tmp_ref, val):
    n, width = val.shape
    for cb in range(width // LANES):
        cols = slice(cb * LANES, (cb + 1) * LANES)
        tmp_ref[cb] = val[:, cols]
        for parity in (0, 1):
            dst = slice(parity * width + cb * LANES, parity * width + (cb + 1) * LANES)
            o_ref[:, dst] = tmp_ref[cb, pl.ds(parity, n // 2, stride=2), :].astype(BF16)


def _hy_short(proj, hy_short, hy_short_b, tr=512):
    main = [pl.BlockSpec((tr, W_D), functools.partial(lambda i, c: (i, c), c=HY_COL0 + c)) for c in range(3)]
    halos = [_halo_specs(tr, W_D, HY_COL0 + c) for c in range(3)]
    wspec = [pl.BlockSpec((SHORT_K, W_D), functools.partial(lambda i, c: (0, c), c=c)) for c in range(3)]
    bspec = [pl.BlockSpec((1, W_D), functools.partial(lambda i, c: (0, c), c=c)) for c in range(3)]
    out = jax.ShapeDtypeStruct((N_TOK // 2, W2_D), BF16)
    ospec = pl.BlockSpec((tr // 2, W2_D), lambda i: (i, 0))
    b2 = hy_short_b.reshape(1, 3 * W_D)
    return pl.pallas_call(
        functools.partial(_hy_short_kernel, tr=tr),
        grid=(N_TOK // tr,),
        in_specs=main + [h[0] for h in halos] + [h[1] for h in halos] + wspec + bspec,
        out_specs=[ospec, ospec],
        out_shape=[out, out],
        scratch_shapes=[pltpu.VMEM((W_D // LANES, tr, LANES), F32)],
        compiler_params=_params("arbitrary"),
        name="hy_short",
    )(*([proj] * 9), hy_short, hy_short, hy_short, b2, b2, b2)


HIGHEST = lax.Precision.HIGHEST
HY_EMB_PAD = 128
W2_D = 2 * W_D


def _dft_gen_kernel(ct_ref, st_ref, cr_ref, sr_ref, c_ref, s_ref):
    t = pl.program_id(0)
    cr = cr_ref[pl.ds(t, 1), :]
    sr = sr_ref[pl.ds(t, 1), :]
    ct = ct_ref[...]
    st = st_ref[...]
    c_ref[...] = (cr * ct - sr * st).astype(BF16)
    s_ref[...] = (sr * ct + cr * st).astype(BF16)


def _dft_matrices(L):
    tr = min(L, 256)
    nt = L // tr
    nt_pad = -(-nt // 8) * 8
    n = jnp.arange(L, dtype=jnp.int32)[None, :]

    def tables(k):
        ang = ((k[:, None] * n) % (2 * L)).astype(F32) * (math.pi / L)
        return jnp.cos(ang), jnp.sin(ang)

    ct, st = tables(jnp.arange(tr, dtype=jnp.int32))
    cr, sr = tables(jnp.arange(nt_pad, dtype=jnp.int32) * tr)
    tile = pl.BlockSpec((tr, L), lambda t: (0, 0))
    rows = pl.BlockSpec((nt_pad, L), lambda t: (0, 0))
    out = pl.BlockSpec((tr, L), lambda t: (t, 0))
    return pl.pallas_call(
        _dft_gen_kernel,
        grid=(nt,),
        in_specs=[tile, tile, rows, rows],
        out_specs=[out, out],
        out_shape=[jax.ShapeDtypeStruct((L, L), BF16)] * 2,
        compiler_params=_params("arbitrary"),
        name="dft_gen",
    )(ct, st, cr, sr)


def _hy_features(L):
    t = jnp.linspace(0.0, 1.0, L, dtype=F32)[:, None]
    bands = (HY_EMB - 1) // 2
    ang = 2 * math.pi * jnp.arange(L, dtype=F32)[:, None] / L
    freqs = jnp.linspace(1e-4, bands - 1, bands, dtype=F32)[None, :]
    z = jnp.concatenate([t, jnp.cos(freqs * ang), -jnp.sin(freqs * ang)], axis=-1)
    return jnp.pad(z, ((0, 0), (0, HY_EMB_PAD - HY_EMB))), t


def _hy_filter_kernel(z_ref, t_ref, w1_ref, b1_ref, f1_ref, w2_ref, b2_ref, f2_ref, w3_ref, dl_ref,
                      fs_ref, fd_ref, hn_ref, tmp_ref, *, tr):
    i = pl.program_id(0)
    hid = jnp.sin(f1_ref[...] * (jnp.dot(z_ref[...], w1_ref[...], precision=HIGHEST,
                                         preferred_element_type=F32) + b1_ref[...]))
    hid = jnp.sin(f2_ref[...] * (jnp.dot(hid, w2_ref[...], precision=HIGHEST,
                                         preferred_element_type=F32) + b2_ref[...]))
    hf = jnp.dot(hid.astype(BF16), w3_ref[...].astype(BF16), preferred_element_type=F32)
    decay = jnp.exp(-t_ref[...] * jnp.abs(dl_ref[...]))
    ridx = i * tr + lax.broadcasted_iota(jnp.int32, (tr, 1), 0)
    fwd = hf[:, :W_D] * decay
    bwd = jnp.where(ridx == 0, 0.0, hf[:, W_D:] * decay)
    fsum = fwd + bwd
    fdif = bwd - fwd
    _store_even_odd(fs_ref, tmp_ref, fsum)
    _store_even_odd(fd_ref, tmp_ref, fdif)
    sgn = (1 - (ridx & 2)).astype(F32)
    even = (ridx & 1) == 0
    hc_part = jnp.sum(jnp.where(even, sgn, 0.0) * fsum, axis=0, keepdims=True)
    hs_part = jnp.sum(jnp.where(even, 0.0, sgn) * fdif, axis=0, keepdims=True)

    @pl.when(i == 0)
    def _():
        hn_ref[...] = jnp.zeros_like(hn_ref)

    hn_ref[0:1, :] += hc_part
    hn_ref[1:2, :] += hs_part


def _hy_filter(L, w1, b1, f1, w2, b2, f2, w3):
    tr = min(L, 512)
    z, t = _hy_features(L)
    max_decay = math.log(HY_TARGET) / HY_FAST_PCT
    min_decay = math.log(HY_TARGET) / HY_SLOW_PCT
    deltas = jnp.linspace(min_decay, max_decay, W_D, dtype=F32)[None, :]
    w1p = jnp.pad(w1, ((0, HY_EMB_PAD - HY_EMB), (0, 0)))
    full = lambda shape: pl.BlockSpec(shape, lambda i: (0, 0))
    return pl.pallas_call(
        functools.partial(_hy_filter_kernel, tr=tr),
        grid=(L // tr,),
        in_specs=[pl.BlockSpec((tr, HY_EMB_PAD), lambda i: (i, 0)),
                  pl.BlockSpec((tr, 1), lambda i: (i, 0)),
                  full((HY_EMB_PAD, HY_HIDDEN)), full((1, HY_HIDDEN)), full((1, HY_HIDDEN)),
                  full((HY_HIDDEN, HY_HIDDEN)), full((1, HY_HIDDEN)), full((1, HY_HIDDEN)),
                  full((HY_HIDDEN, 2 * W_D)), full((1, W_D))],
        out_specs=[pl.BlockSpec((tr // 2, W2_D), lambda i: (i, 0)),
                   pl.BlockSpec((tr // 2, W2_D), lambda i: (i, 0)),
                   pl.BlockSpec((8, W_D), lambda i: (0, 0))],
        out_shape=[jax.ShapeDtypeStruct((L // 2, W2_D), BF16),
                   jax.ShapeDtypeStruct((L // 2, W2_D), BF16),
                   jax.ShapeDtypeStruct((8, W_D), F32)],
        scratch_shapes=[pltpu.VMEM((W_D // LANES, tr, LANES), F32)],
        compiler_params=_params("arbitrary"),
        name="hy_filter",
    )(z, t, w1p, b1.reshape(1, -1), f1.reshape(1, -1), w2, b2.reshape(1, -1), f2.reshape(1, -1), w3, deltas)


def _twiddles(L):
    k = jnp.arange(L // 2, dtype=F32)[:, None] * (math.pi / L)
    return jnp.cos(k), jnp.sin(k)


def _butterfly(c_ref, s_ref, x2, twc, tws):
    xc2 = jnp.dot(c_ref[...], x2, preferred_element_type=F32)
    xs2 = jnp.dot(s_ref[...], x2, preferred_element_type=F32)
    ec, oc = xc2[:, :W_D], xc2[:, W_D:]
    es, os_ = xs2[:, :W_D], xs2[:, W_D:]
    tc = twc * oc - tws * os_
    ts = twc * os_ + tws * oc
    return ec + tc, es + ts, ec - tc, es - ts


def _hy_spec_kernel(c_ref, s_ref, fs_ref, fd_ref, twc_ref, tws_ref, hac_ref, has_ref, hrc_ref, hrs_ref):
    twc, tws = twc_ref[...], tws_ref[...]
    ac, _, bc, _ = _butterfly(c_ref, s_ref, fs_ref[...], twc, tws)
    _, as_, _, bs = _butterfly(c_ref, s_ref, fd_ref[...], twc, tws)
    hac_ref[...] = ac
    has_ref[...] = as_
    hrc_ref[...] = bc
    hrs_ref[...] = -bs


def _hy_spectrum(cmat, smat, fs2, fd2, twc, tws, tk):
    M = cmat.shape[0]
    tile = pl.BlockSpec((tk, M), lambda i: (i, 0))
    full = pl.BlockSpec((M, W2_D), lambda i: (0, 0), pipeline_mode=pl.Buffered(1))
    col = pl.BlockSpec((tk, 1), lambda i: (i, 0))
    out = pl.BlockSpec((tk, W_D), lambda i: (i, 0))
    return pl.pallas_call(
        _hy_spec_kernel,
        grid=(M // tk,),
        in_specs=[tile, tile, full, full, col, col],
        out_specs=[out] * 4,
        out_shape=[jax.ShapeDtypeStruct((M, W_D), F32)] * 4,
        compiler_params=_params("arbitrary"),
        name="hy_spectrum",
    )(cmat, smat, fs2, fd2, twc, tws)


def _hy_fwd_kernel(c_ref, s_ref, w_ref, hac_ref, has_ref, hrc_ref, hrs_ref, twc_ref, tws_ref, hn_ref,
                   vc_ref, vs_ref, yn_ref, *, L, tk, group):
    kt = pl.program_id(1)
    M = L // 2
    twc, tws = twc_ref[...], tws_ref[...]
    hac, has, hrc, hrs = hac_ref[...], has_ref[...], hrc_ref[...], hrs_ref[...]
    kidx = kt * tk + lax.broadcasted_iota(jnp.int32, (tk, 1), 0)
    om = jnp.where(kidx == 0, 0.5 / L, 1.0 / L)
    for g in range(group):
        w2 = w_ref[g * M:(g + 1) * M, :]
        rows = slice(g * tk, (g + 1) * tk)
        ac, as_, bc, bs = _butterfly(c_ref, s_ref, w2, twc, tws)
        yac = ac * hac + as_ * has
        yai = ac * has - as_ * hac
        pc = bc * hrc - bs * hrs
        pi = -(bc * hrs + bs * hrc)
        dc, di = yac - pc, yai - pi
        vc_ref[rows, :] = (om * jnp.concatenate([yac + pc, twc * dc - tws * di], axis=1)).astype(BF16)
        vs_ref[rows, :] = ((-1.0 / L) * jnp.concatenate([yai + pi, twc * di + tws * dc], axis=1)).astype(BF16)

    @pl.when(kt == 0)
    def _():
        m = lax.broadcasted_iota(jnp.int32, (M, 1), 0)
        sgn = (1 - 2 * (m & 1)).astype(F32)
        hc, hs = hn_ref[0:1, :], hn_ref[1:2, :]
        for g in range(group):
            xn2 = jnp.sum(w_ref[g * M:(g + 1) * M, :].astype(F32) * sgn, axis=0, keepdims=True)
            en, on = xn2[:, :W_D], xn2[:, W_D:]
            yn = jnp.concatenate([en * hc + on * hs, on * hc - en * hs], axis=1) * (1.0 / L)
            yn_ref[g] = jnp.broadcast_to(yn, yn_ref.shape[1:])


HY_GROUP = 4


def _hy_group(M, tk, nb, blk0):
    ok = M == tk and nb % HY_GROUP == 0 and blk0 % HY_GROUP == 0
    return HY_GROUP if ok else 1


def _hy_forward(cmat, smat, w2, h4, twc, tws, hn, L, nb, blk0, tk):
    M = L // 2
    g = _hy_group(M, tk, nb, blk0)
    tile = pl.BlockSpec((tk, M), lambda b, k: (k, 0))
    htile = pl.BlockSpec((tk, W_D), lambda b, k: (k, 0))
    col = pl.BlockSpec((tk, 1), lambda b, k: (k, 0))
    out = pl.BlockSpec((g * tk, W2_D), lambda b, k: (b * (M // tk) + k, 0))
    return pl.pallas_call(
        functools.partial(_hy_fwd_kernel, L=L, tk=tk, group=g),
        grid=(nb // g, M // tk),
        in_specs=[tile, tile, pl.BlockSpec((g * M, W2_D), lambda b, k: (blk0 // g + b, 0)),
                  htile, htile, htile, htile, col, col,
                  pl.BlockSpec((8, W_D), lambda b, k: (0, 0))],
        out_specs=[out, out, pl.BlockSpec((g, 8, W2_D), lambda b, k: (b, 0, 0))],
        out_shape=[jax.ShapeDtypeStruct((nb * M, W2_D), BF16),
                   jax.ShapeDtypeStruct((nb * M, W2_D), BF16),
                   jax.ShapeDtypeStruct((nb, 8, W2_D), F32)],
        compiler_params=_params("arbitrary", "arbitrary"),
        name="hy_forward",
    )(cmat, smat, w2, *h4, twc, tws, hn)


def _hy_inv_kernel(c_ref, s_ref, vc_ref, vs_ref, yn_ref, x0_ref, w_ref, bias_ref, z_ref, tmp_ref,
                   *, M, tt, group):
    ti = pl.program_id(1)
    m = ti * tt + lax.broadcasted_iota(jnp.int32, (tt, 1), 0)
    sgn = (1 - 2 * (m & 1)).astype(F32)
    for g in range(group):
        seq = slice(g * M, (g + 1) * M)
        rows = slice(g * tt, (g + 1) * tt)
        y = (jnp.dot(c_ref[...], vc_ref[seq, :], preferred_element_type=F32)
             + jnp.dot(s_ref[...], vs_ref[seq, :], preferred_element_type=F32))
        y = y + sgn * yn_ref[g, 0:1, :] + w_ref[rows, :].astype(F32) * bias_ref[...]
        z2 = x0_ref[rows, :].astype(F32) * y
        for cb in range(W_D // LANES):
            cols = slice(cb * LANES, (cb + 1) * LANES)
            tmp_ref[cb, pl.ds(0, tt, stride=2), :] = z2[:, cols]
            tmp_ref[cb, pl.ds(1, tt, stride=2), :] = z2[:, W_D + cb * LANES:W_D + (cb + 1) * LANES]
            z_ref[g * 2 * tt:(g + 1) * 2 * tt, cols] = tmp_ref[cb].astype(BF16)


def _hy_inverse(cmat, smat, vc, vs, yn, x02, w2, bias, L, nb, blk0, tt):
    M = L // 2
    per = M // tt
    g = _hy_group(M, tt, nb, blk0)
    tile = pl.BlockSpec((tt, M), lambda b, t: (t, 0))
    seq = pl.BlockSpec((g * M, W2_D), lambda b, t: (b, 0))
    rows_in = pl.BlockSpec((g * tt, W2_D), lambda b, t: ((blk0 // g + b) * per + t, 0))
    return pl.pallas_call(
        functools.partial(_hy_inv_kernel, M=M, tt=tt, group=g),
        grid=(nb // g, per),
        in_specs=[tile, tile, seq, seq,
                  pl.BlockSpec((g, 8, W2_D), lambda b, t: (b, 0, 0)),
                  rows_in, rows_in,
                  pl.BlockSpec((1, W2_D), lambda b, t: (0, 0))],
        out_specs=pl.BlockSpec((2 * g * tt, W_D), lambda b, t: (b * per + t, 0)),
        out_shape=jax.ShapeDtypeStruct((nb * L, W_D), BF16),
        scratch_shapes=[pltpu.VMEM((W_D // LANES, 2 * tt, LANES), F32)],
        compiler_params=_params("arbitrary", "arbitrary"),
        name="hy_inverse",
    )(cmat, smat, vc, vs, yn, x02, w2, jnp.concatenate([bias, bias]).reshape(1, W2_D))


def _hyena_long_conv(x02, w2, L, nb, blk0, filt_params, bias):
    M = L // 2
    tk = min(M, 256)
    cmat, smat = _dft_matrices(M)
    twc, tws = _twiddles(L)
    fs2, fd2, hn = _hy_filter(L, *filt_params)
    h4 = _hy_spectrum(cmat, smat, fs2, fd2, twc, tws, tk)
    vc, vs, yn = _hy_forward(cmat, smat, w2, h4, twc, tws, hn, L, nb, blk0, tk)
    return _hy_inverse(cmat, smat, vc, vs, yn, x02, w2, bias, L, nb, blk0, tk)


def kernel(x_prompt, x_sample, cache_k, cache_v, c, c_ctx, ada_w, ada_b, norm_mix, norm_ffn,
           e_w_in, e_conv_a, e_q_norm, e_k_norm, e_rpb, e_w_out,
           o_w_in, o_conf_dw, o_conf_dw_b, o_conf_ln_g, o_conf_ln_b, o_hy_short, o_hy_short_b,
           o_hy_w1, o_hy_b1, o_hy_f1, o_hy_w2, o_hy_b2, o_hy_f2, o_hy_w3, o_hy_bias, o_w_out,
           ffn_in, ffn_conv, ffn_out):
    x_parts = (x_prompt.reshape(N_P, D_MODEL), x_sample.reshape(N_S, D_MODEL))
    cvec = jnp.concatenate([c_ctx[None, :], c, jnp.zeros((N_SEG_PAD - 1 - DEC_BATCH, D_MODEL), F32)], axis=0)
    mod = _adaln(cvec, ada_w, ada_b)
    ks_new, vs_new = [], []
    for layer in range(DEPTH):
        j = layer // 2
        last = layer == DEPTH - 1
        h = _norm_mod_call(x_parts, norm_mix[layer], mod, layer, 0)
        if layer % 2 == 0:
            proj, w_out = _proj(h, e_w_in[j], e_w_out[j])
            ya = (_short_gated_conv(proj, e_conv_a[j]),)
            yb_p, k_p, v_p = _ctx_attention(proj, e_q_norm[j], e_k_norm[j])
            yb_s = _na_attention(proj, cache_k[:, j], cache_v[:, j], e_rpb[j], e_q_norm[j], e_k_norm[j])
            yb = (yb_p, yb_s)
            ks_new.append(k_p.reshape(BATCH, SEQ, NA_HEADS, NA_HEAD_DIM))
            vs_new.append(v_p.reshape(BATCH, SEQ, NA_HEADS, NA_HEAD_DIM))
        else:
            proj, w_out = _proj(h, o_w_in[j], o_w_out[j])
            ya = (_conformer(proj, o_conf_dw[j], o_conf_dw_b[j], o_conf_ln_g[j], o_conf_ln_b[j]),)
            x0, w = _hy_short(proj, o_hy_short[j], o_hy_short_b[j])
            fp = (o_hy_w1[j], o_hy_b1[j], o_hy_f1[j], o_hy_w2[j], o_hy_b2[j], o_hy_f2[j], o_hy_w3[j])
            z_p = _hyena_long_conv(x0, w, SEQ, BATCH, 0, fp, o_hy_bias[j])
            z_s = _hyena_long_conv(x0, w, DEC_SEQ, DEC_BATCH, N_P // DEC_SEQ, fp, o_hy_bias[j])
            yb = (z_p, z_s)
        x, h_ffn = _out_proj(ya, yb, w_out, x_parts, mod, layer, norm_ffn[layer])
        u, w_ffn_out = _ffn_in(h_ffn, ffn_in, ffn_conv, ffn_out, layer)
        if last:
            x_parts = (_ffn_out(u, w_ffn_out, x, mod, layer, 0, N_P),
                       _ffn_out(u, w_ffn_out, x, mod, layer, N_P, N_S))
        else:
            x_parts = (_ffn_out(u, w_ffn_out, x, mod, layer),)
    xp = x_parts[0].reshape(BATCH, SEQ, D_MODEL)
    xs = x_parts[1].reshape(DEC_BATCH, DEC_SEQ, D_MODEL)
    return (xp, xs, jnp.stack(ks_new, axis=1), jnp.stack(vs_new, axis=1))
```

```python
import functools
import math

import jax
import jax.numpy as jnp
from jax import lax
from jax.experimental import pallas as pl
from jax.experimental.pallas import tpu as pltpu

D_MODEL = 2048
BATCH = 16
SEQ = 256
DEPTH = 2
DEC_BATCH = 2
DEC_SEQ = 4096
PAST_LEN = 512
GRID_W = 64
W_A = 1024
NA_HEADS = 8
NA_HEAD_DIM = 128
W_B = NA_HEADS * NA_HEAD_DIM
W_C = 1024
W_D = 1024
NA_WIN_R = 8
NA_WIN_C = 16
SHORT_K = 3
CONF_K = 31
D_FF = 5632
HY_EMB = 33
HY_HIDDEN = 64
HY_FAST_PCT = 0.3
HY_SLOW_PCT = 1.5
HY_TARGET = 1e-2
EPS = 1e-6
NEG_INF = -1e30

N_P = BATCH * SEQ
N_S = DEC_BATCH * DEC_SEQ
N_TOK = N_P + N_S
SEG = DEC_SEQ
N_SEG_PAD = 8
N_MOD = 6 * D_MODEL
HALO = 16
LANES = 128
VMEM_LIMIT_BYTES = 60 * 1024 * 1024
BF16 = jnp.bfloat16
F32 = jnp.float32


def _params(*sem):
    return pltpu.CompilerParams(dimension_semantics=sem, vmem_limit_bytes=VMEM_LIMIT_BYTES)


def _seq_len(row0):
    return jnp.where(row0 < N_P, SEQ, DEC_SEQ)


def _mod_row(mod_ref, row0):
    return mod_ref[pl.ds(row0 // SEG, 1), :]


def _mod_spec(layer, blk, tn, index_pos):
    per = D_MODEL // tn

    def imap(*idx):
        j = idx[index_pos] if index_pos is not None else 0
        return (layer, 0, blk * per + j)

    return pl.BlockSpec((None, N_SEG_PAD, tn), imap)


def _adaln_kernel(c_ref, w_ref, b_ref, o_ref):
    c = c_ref[...]
    s = (c * jax.nn.sigmoid(c)).astype(BF16)
    o_ref[...] = jnp.dot(s, w_ref[...].astype(BF16), preferred_element_type=F32) + b_ref[...]


def _adaln(cvec, ada_w, ada_b, tn=1024):
    return pl.pallas_call(
        _adaln_kernel,
        grid=(DEPTH, N_MOD // tn),
        in_specs=[pl.BlockSpec((N_SEG_PAD, D_MODEL), lambda l, j: (0, 0)),
                  pl.BlockSpec((None, D_MODEL, tn), lambda l, j: (l, 0, j)),
                  pl.BlockSpec((None, 1, tn), lambda l, j: (l, 0, j))],
        out_specs=pl.BlockSpec((None, N_SEG_PAD, tn), lambda l, j: (l, 0, j)),
        out_shape=jax.ShapeDtypeStruct((DEPTH, N_SEG_PAD, N_MOD), F32),
        compiler_params=_params("arbitrary", "arbitrary"),
        name="adaln",
    )(cvec, ada_w, ada_b.reshape(DEPTH, 1, N_MOD))


def _norm_mod(x, g, shift, scale):
    y = x * lax.rsqrt(jnp.mean(x * x, axis=-1, keepdims=True) + EPS) * g
    return y * (1.0 + scale) + shift


def _stream_specs(x_parts, tr, width, pos):
    if len(x_parts) == 1:
        return [pl.BlockSpec((tr, width), lambda *idx: (idx[pos], 0))]
    n_pt = N_P // tr
    return [pl.BlockSpec((tr, width), lambda *idx: (jnp.minimum(idx[pos], n_pt - 1), 0)),
            pl.BlockSpec((tr, width), lambda *idx: (jnp.maximum(idx[pos] - n_pt, 0), 0))]


def _stream_tile(x_refs, row0):
    if len(x_refs) == 1:
        return x_refs[0][...]
    return jnp.where(row0 < N_P, x_refs[0][...], x_refs[1][...])


def _norm_mod_kernel(*refs, tr, n_parts):
    x_refs, (g_ref, sh_ref, sc_ref, o_ref) = refs[:n_parts], refs[n_parts:]
    row0 = pl.program_id(0) * tr
    x = _stream_tile(x_refs, row0)
    o_ref[...] = _norm_mod(x, g_ref[...], _mod_row(sh_ref, row0), _mod_row(sc_ref, row0)).astype(BF16)


def _norm_mod_call(x_parts, g, mod, layer, blk, tr=1024):
    return pl.pallas_call(
        functools.partial(_norm_mod_kernel, tr=tr, n_parts=len(x_parts)),
        grid=(N_TOK // tr,),
        in_specs=_stream_specs(x_parts, tr, D_MODEL, 0) + [
            pl.BlockSpec((1, D_MODEL), lambda i: (0, 0)),
            _mod_spec(layer, blk, D_MODEL, None),
            _mod_spec(layer, blk + 1, D_MODEL, None)],
        out_specs=pl.BlockSpec((tr, D_MODEL), lambda i: (i, 0)),
        out_shape=jax.ShapeDtypeStruct((N_TOK, D_MODEL), BF16),
        compiler_params=_params("arbitrary"),
        name="norm_mod",
    )(*x_parts, g.reshape(1, D_MODEL), mod, mod)


CAST_ROWS = 128


def _proj_kernel(h_ref, w_ref, wo_ref, o_ref, wob_ref, wb_ref):
    @pl.when(pl.program_id(1) == 0)
    def _():
        wb_ref[...] = w_ref[...].astype(BF16)

    wob_ref[...] = wo_ref[...].astype(BF16)
    o_ref[...] = jnp.dot(h_ref[...], wb_ref[...], preferred_element_type=F32).astype(o_ref.dtype)


def _proj(h, w, w_out, tm=2048, tn=1024):
    m, k = h.shape
    n = w.shape[1]
    k2, n2 = w_out.shape
    n_i = m // tm
    n_cast = k2 // CAST_ROWS
    assert n_cast * CAST_ROWS == k2 and n_cast <= (n // tn) * n_i

    def cast_block(j, i):
        return (jnp.minimum(j * n_i + i, n_cast - 1), 0)

    return pl.pallas_call(
        _proj_kernel,
        grid=(n // tn, n_i),
        in_specs=[pl.BlockSpec((tm, k), lambda j, i: (i, 0)),
                  pl.BlockSpec((k, tn), lambda j, i: (0, j)),
                  pl.BlockSpec((CAST_ROWS, n2), cast_block)],
        out_specs=[pl.BlockSpec((tm, tn), lambda j, i: (i, j)),
                   pl.BlockSpec((CAST_ROWS, n2), cast_block)],
        out_shape=[jax.ShapeDtypeStruct((m, n), BF16),
                   jax.ShapeDtypeStruct((k2, n2), BF16)],
        scratch_shapes=[pltpu.VMEM((k, tn), BF16)],
        compiler_params=_params("arbitrary", "arbitrary"),
        name="proj",
    )(h, w, w_out)


def _shift_rows(p, prev_row, next_row, row0, seq_len):
    n = p.shape[0]
    ridx = lax.broadcasted_iota(jnp.int32, (n, 1), 0)
    pos = (row0 + ridx) & (seq_len - 1)
    p_prev = jnp.where(ridx == 0, prev_row, pltpu.roll(p, 1, 0))
    p_prev = jnp.where(pos == 0, 0.0, p_prev)
    p_next = jnp.where(ridx == n - 1, next_row, pltpu.roll(p, n - 1, 0))
    p_next = jnp.where(pos == seq_len - 1, 0.0, p_next)
    return p_prev, p_next


def _sconv_kernel(ab_ref, ac_ref, ax_ref, pc_ref, px_ref, nc_ref, nx_ref, w_ref, o_ref, *, tr):
    row0 = pl.program_id(0) * tr
    p = ac_ref[...].astype(F32) * ax_ref[...].astype(F32)
    prev_row = pc_ref[HALO - 1:HALO, :].astype(F32) * px_ref[HALO - 1:HALO, :].astype(F32)
    next_row = nc_ref[0:1, :].astype(F32) * nx_ref[0:1, :].astype(F32)
    p_prev, p_next = _shift_rows(p, prev_row, next_row, row0, _seq_len(row0))
    w = w_ref[...]
    conv = w[0:1, :] * p_prev + w[1:2, :] * p + w[2:3, :] * p_next
    o_ref[...] = (ab_ref[...].astype(F32) * conv).astype(BF16)


def _halo_specs(tr, width, col):
    per = tr // HALO
    last = N_TOK // HALO - 1
    prev = pl.BlockSpec((HALO, width), lambda i: (jnp.maximum(i * per - 1, 0), col))
    nxt = pl.BlockSpec((HALO, width), lambda i: (jnp.minimum((i + 1) * per, last), col))
    return prev, nxt


def _short_gated_conv(proj, conv_a, tr=1024):
    pc, nc = _halo_specs(tr, W_A, 1)
    px, nx = _halo_specs(tr, W_A, 2)
    return pl.pallas_call(
        functools.partial(_sconv_kernel, tr=tr),
        grid=(N_TOK // tr,),
        in_specs=[pl.BlockSpec((tr, W_A), lambda i: (i, 0)),
                  pl.BlockSpec((tr, W_A), lambda i: (i, 1)),
                  pl.BlockSpec((tr, W_A), lambda i: (i, 2)),
                  pc, px, nc, nx,
                  pl.BlockSpec((SHORT_K, W_A), lambda i: (0, 0))],
        out_specs=pl.BlockSpec((tr, W_A), lambda i: (i, 0)),
        out_shape=jax.ShapeDtypeStruct((N_TOK, W_A), BF16),
        compiler_params=_params("arbitrary"),
        name="short_gated_conv",
    )(proj, proj, proj, proj, proj, proj, proj, conv_a)


ROWS = DEC_SEQ // GRID_W
NA_QROWS = 8
NA_KROWS = 16
NA_NQ = NA_QROWS * GRID_W
NA_NK = NA_KROWS * GRID_W
NA_STEPS = ROWS // NA_QROWS
QKV_COL0 = 3 * W_A // NA_HEAD_DIM
HEAD_SCALE = NA_HEAD_DIM ** -0.5
LOG2E = math.log2(math.e)
NA_CHUNK = 16
NT_DIMS = (((1,), (1,)), ((), ()))


def _head_rmsnorm(x, g):
    return x * lax.rsqrt(jnp.mean(x * x, axis=-1, keepdims=True) + EPS) * g


def _na_key_row0(j):
    return jnp.clip(j * NA_QROWS - NA_WIN_R // 2, 0, ROWS - NA_KROWS)


NA_SLOT_LO = NA_QROWS
NA_SLOTS = 2 * NA_KROWS


def _na_geometry(j):
    kr0 = min(max(j * NA_QROWS - NA_WIN_R // 2, 0), ROWS - NA_KROWS)
    per_a = []
    for a in range(NA_QROWS):
        r = j * NA_QROWS + a
        r0 = min(max(r - NA_WIN_R // 2, 0), ROWS - NA_WIN_R)
        per_a.append((kr0 - r + NA_WIN_R - 1 + NA_SLOT_LO, r0 - kr0))
    return kr0, per_a


def _na_bias_slots(rpb):
    n_dr, n_dc = rpb.shape[1], rpb.shape[2]
    out = jax.ShapeDtypeStruct((NA_HEADS, GRID_W, NA_SLOTS * GRID_W), F32)
    spec = pl.BlockSpec((None, GRID_W, NA_SLOTS * GRID_W), lambda h: (h, 0, 0))
    return pl.pallas_call(
        functools.partial(_na_slots_kernel, n_dr=n_dr, n_dc=n_dc),
        grid=(NA_HEADS,),
        in_specs=[pl.BlockSpec(memory_space=pltpu.SMEM)],
        out_specs=[spec, spec],
        out_shape=[out, out],
        compiler_params=_params("arbitrary"),
        name="na_bias_slots",
    )(rpb.astype(F32).reshape(NA_HEADS, n_dr * n_dc))


def _na_slots_kernel(rpb_ref, even_ref, odd_ref, *, n_dr, n_dc):
    h = pl.program_id(0)
    pair = 2 * GRID_W
    qc = lax.broadcasted_iota(jnp.int32, (GRID_W, pair), 0)
    lane = lax.broadcasted_iota(jnp.int32, (GRID_W, pair), 1)
    kc = lane & (GRID_W - 1)
    delta = kc - qc + NA_WIN_C - 1
    cs = jnp.clip(qc - NA_WIN_C // 2, 0, GRID_W - NA_WIN_C)
    col_ok = (kc >= cs) & (kc < cs + NA_WIN_C)
    upper = lax.broadcasted_iota(jnp.int32, (1, pair), 1) >= GRID_W
    for shift, ref in ((0, even_ref), (1, odd_ref)):
        for t in range(NA_SLOTS // 2):
            drs = [2 * t + shift + half - NA_SLOT_LO for half in (0, 1)]
            acc = jnp.full((GRID_W, pair), NEG_INF, F32)
            if any(0 <= dr < n_dr for dr in drs):
                for dc in range(n_dc):
                    lo, hi = [rpb_ref[h, dr * n_dc + dc] if 0 <= dr < n_dr else NEG_INF for dr in drs]
                    acc = jnp.where(delta == dc, jnp.where(upper, hi, lo), acc)
                acc = jnp.where(col_ok, acc, NEG_INF)
            ref[:, t * pair:(t + 1) * pair] = acc


def _na_build_bias(j_static, even_ref, odd_ref, bias_ref):
    _, per_a = _na_geometry(j_static)
    key_row = lax.broadcasted_iota(jnp.int32, (1, NA_NK), 1) // GRID_W
    for a, (m, lo) in enumerate(per_a):
        src = even_ref if m % 2 == 0 else odd_ref
        start = (m - m % 2) * GRID_W
        slab = src[:, start:start + NA_NK] * LOG2E
        ok = (key_row >= lo) & (key_row < lo + NA_WIN_R)
        bias_ref[a * GRID_W:(a + 1) * GRID_W, :] = jnp.where(ok, slab, NEG_INF)


def _with_ones_column(v):
    lane = lax.broadcasted_iota(jnp.int32, v.shape, 1)
    return jnp.concatenate([v.astype(BF16), jnp.where(lane == 0, 1.0, 0.0).astype(BF16)], axis=1)


def _na_kernel(q_ref, k_ref, v_ref, ck_ref, cv_ref, even_ref, odd_ref, qg_ref, kg_ref, o_ref,
               kn_ref, vx_ref, kall_ref, vall_ref, bias_ref, s_ref, p_ref):
    j = pl.program_id(2)
    hd = NA_HEAD_DIM

    @pl.when(j == 0)
    def _():
        kn_ref[...] = _head_rmsnorm(k_ref[...].astype(F32), kg_ref[...]).astype(BF16)
        vx_ref[...] = _with_ones_column(v_ref[...])
        kall_ref[NA_NK:, :] = ck_ref[...].astype(BF16)
        vall_ref[NA_NK:, :] = _with_ones_column(cv_ref[...])

    for j_static in (0, 1, NA_STEPS - 1):
        @pl.when(j == j_static)
        def _(j_static=j_static):
            _na_build_bias(j_static, even_ref, odd_ref, bias_ref)

    q = (_head_rmsnorm(q_ref[...].astype(F32), qg_ref[...]) * (HEAD_SCALE * LOG2E)).astype(BF16)
    tok0 = pl.multiple_of(_na_key_row0(j) * GRID_W, GRID_W)
    kall_ref[:NA_NK, :] = kn_ref[pl.ds(tok0, NA_NK), :]
    vall_ref[:NA_NK, :] = vx_ref[pl.ds(tok0, NA_NK), :]
    s_ref[...] = lax.dot_general(q, kall_ref[...], NT_DIMS, preferred_element_type=F32)
    for c in range(NA_NQ // NA_CHUNK):
        rows = slice(c * NA_CHUNK, (c + 1) * NA_CHUNK)
        s_loc = s_ref[rows, :NA_NK] + bias_ref[rows, :]
        s_ctx = s_ref[rows, NA_NK:]
        m = functools.reduce(jnp.maximum, [blk[:, t:t + hd] for blk in (s_loc, s_ctx)
                                           for t in range(0, blk.shape[1], hd)])
        m = jnp.max(m, axis=-1, keepdims=True)
        p_ref[rows, :NA_NK] = jnp.exp2(s_loc - m).astype(BF16)
        p_ref[rows, NA_NK:] = jnp.exp2(s_ctx - m).astype(BF16)
    o = jnp.dot(p_ref[...], vall_ref[...], preferred_element_type=F32)
    o_ref[...] = (o[:, :hd] / o[:, hd:hd + 1]).astype(BF16)


def _na_attention(proj, cache_k, cache_v, rpb, q_norm, k_norm):
    assert NA_STEPS >= 3 and all(_na_geometry(j)[1] == _na_geometry(1)[1] for j in range(1, NA_STEPS - 1))
    even, odd = _na_bias_slots(rpb)
    ck = cache_k.reshape(DEC_BATCH, PAST_LEN, W_B)
    cv = cache_v.reshape(DEC_BATCH, PAST_LEN, W_B)
    hd = NA_HEAD_DIM
    ctx = pl.BlockSpec((None, PAST_LEN, hd), lambda b, h, j: (b, 0, h))
    q_blk0 = N_P // NA_NQ
    kv_blk0 = N_P // DEC_SEQ
    slots = pl.BlockSpec((None, GRID_W, NA_SLOTS * GRID_W), lambda b, h, j: (h, 0, 0))
    return pl.pallas_call(
        _na_kernel,
        grid=(DEC_BATCH, NA_HEADS, NA_STEPS),
        in_specs=[
            pl.BlockSpec((NA_NQ, hd), lambda b, h, j: (q_blk0 + b * NA_STEPS + j, QKV_COL0 + h)),
            pl.BlockSpec((DEC_SEQ, hd), lambda b, h, j: (kv_blk0 + b, QKV_COL0 + NA_HEADS + h)),
            pl.BlockSpec((DEC_SEQ, hd), lambda b, h, j: (kv_blk0 + b, QKV_COL0 + 2 * NA_HEADS + h)),
            ctx, ctx,
            slots, slots,
            pl.BlockSpec((1, hd), lambda b, h, j: (0, 0)),
            pl.BlockSpec((1, hd), lambda b, h, j: (0, 0)),
        ],
        out_specs=pl.BlockSpec((NA_NQ, hd), lambda b, h, j: (b * NA_STEPS + j, h)),
        out_shape=jax.ShapeDtypeStruct((N_S, W_B), BF16),
        scratch_shapes=[pltpu.VMEM((DEC_SEQ, hd), BF16),
                        pltpu.VMEM((DEC_SEQ, 2 * hd), BF16),
                        pltpu.VMEM((NA_NK + PAST_LEN, hd), BF16),
                        pltpu.VMEM((NA_NK + PAST_LEN, 2 * hd), BF16),
                        pltpu.VMEM((NA_NQ, NA_NK), F32),
                        pltpu.VMEM((NA_NQ, NA_NK + PAST_LEN), F32),
                        pltpu.VMEM((NA_NQ, NA_NK + PAST_LEN), BF16)],
        compiler_params=_params("arbitrary", "arbitrary", "arbitrary"),
        name="na_attention",
    )(proj, proj, proj, ck, cv, even, odd, q_norm.reshape(1, hd), k_norm.reshape(1, hd))


def _ctx_kernel(q_ref, k_ref, v_ref, qg_ref, kg_ref, o_ref, kn_ref, vo_ref):
    vo_ref[...] = v_ref[...].astype(F32)
    for h in range(NA_HEADS):
        cols = slice(h * NA_HEAD_DIM, (h + 1) * NA_HEAD_DIM)
        q = (_head_rmsnorm(q_ref[:, cols].astype(F32), qg_ref[...]) * HEAD_SCALE).astype(BF16)
        kn = _head_rmsnorm(k_ref[:, cols].astype(F32), kg_ref[...])
        kn_ref[:, cols] = kn
        s = lax.dot_general(q, kn.astype(BF16), NT_DIMS, preferred_element_type=F32)
        p = jnp.exp(s - jnp.max(s, axis=-1, keepdims=True))
        denom = jnp.sum(p, axis=-1, keepdims=True)
        o = jnp.dot(p.astype(BF16), v_ref[:, cols], preferred_element_type=F32)
        o_ref[:, cols] = (o / denom).astype(BF16)


def _ctx_attention(proj, q_norm, k_norm):
    hd = NA_HEAD_DIM
    col0 = 3 * W_A // W_B
    ospec = pl.BlockSpec((SEQ, W_B), lambda b: (b, 0))
    return pl.pallas_call(
        _ctx_kernel,
        grid=(BATCH,),
        in_specs=[
            pl.BlockSpec((SEQ, W_B), lambda b: (b, col0)),
            pl.BlockSpec((SEQ, W_B), lambda b: (b, col0 + 1)),
            pl.BlockSpec((SEQ, W_B), lambda b: (b, col0 + 2)),
            pl.BlockSpec((1, hd), lambda b: (0, 0)),
            pl.BlockSpec((1, hd), lambda b: (0, 0)),
        ],
        out_specs=[ospec, ospec, ospec],
        out_shape=[jax.ShapeDtypeStruct((N_P, W_B), BF16),
                   jax.ShapeDtypeStruct((N_P, W_B), F32),
                   jax.ShapeDtypeStruct((N_P, W_B), F32)],
        compiler_params=_params("arbitrary"),
        name="ctx_attention",
    )(proj, proj, proj, q_norm.reshape(1, hd), k_norm.reshape(1, hd))


def _out_proj_kernel(*refs, tm, ka, n_ya, n_yb, n_x):
    ya_refs, refs = refs[:n_ya], refs[n_ya:]
    yb_refs, refs = refs[:n_yb], refs[n_yb:]
    w_ref, refs = refs[0], refs[1:]
    x_refs, refs = refs[:n_x], refs[n_x:]
    gate_ref, g_ref, sh_ref, sc_ref, xo_ref, ho_ref = refs
    row0 = pl.program_id(0) * tm
    o = (jnp.dot(_stream_tile(ya_refs, row0), w_ref[0:ka, :], preferred_element_type=F32)
         + jnp.dot(_stream_tile(yb_refs, row0), w_ref[ka:, :], preferred_element_type=F32))
    x_new = _stream_tile(x_refs, row0) + _mod_row(gate_ref, row0) * o
    xo_ref[...] = x_new
    ho_ref[...] = _norm_mod(x_new, g_ref[...], _mod_row(sh_ref, row0), _mod_row(sc_ref, row0)).astype(BF16)


def _out_proj(ya_parts, yb_parts, w_bf16, x_parts, mod, layer, g_ffn, tm=512):
    ka, kb = ya_parts[0].shape[1], yb_parts[0].shape[1]
    return pl.pallas_call(
        functools.partial(_out_proj_kernel, tm=tm, ka=ka, n_ya=len(ya_parts), n_yb=len(yb_parts),
                          n_x=len(x_parts)),
        grid=(N_TOK // tm,),
        in_specs=_stream_specs(ya_parts, tm, ka, 0)
        + _stream_specs(yb_parts, tm, kb, 0)
        + [pl.BlockSpec((ka + kb, D_MODEL), lambda i: (0, 0))]
        + _stream_specs(x_parts, tm, D_MODEL, 0) + [
            _mod_spec(layer, 2, D_MODEL, None),
            pl.BlockSpec((1, D_MODEL), lambda i: (0, 0)),
            _mod_spec(layer, 3, D_MODEL, None),
            _mod_spec(layer, 4, D_MODEL, None)],
        out_specs=[pl.BlockSpec((tm, D_MODEL), lambda i: (i, 0)),
                   pl.BlockSpec((tm, D_MODEL), lambda i: (i, 0))],
        out_shape=[jax.ShapeDtypeStruct((N_TOK, D_MODEL), F32),
                   jax.ShapeDtypeStruct((N_TOK, D_MODEL), BF16)],
        compiler_params=_params("arbitrary"),
        name="out_proj",
    )(*ya_parts, *yb_parts, w_bf16, *x_parts, mod, g_ffn.reshape(1, D_MODEL), mod, mod)


GELU_C = math.sqrt(2.0 / math.pi)


def _gelu_tanh(x):
    return 0.5 * x * (1.0 + jnp.tanh(GELU_C * (x + 0.044715 * (x * x * x))))


def _ffn_in_kernel(h_ref, hp_ref, hn_ref, wa_ref, wg_ref, cw_ref, wo_ref, o_ref, wob_ref, wab_ref, wgb_ref,
                   *, tm, tn):
    i = pl.program_id(1)

    @pl.when(i == 0)
    def _():
        wab_ref[...] = wa_ref[...].astype(BF16)
        wgb_ref[...] = wg_ref[...].astype(BF16)

    wob_ref[...] = wo_ref[...].astype(BF16)

    row0 = i * tm
    seq_len = _seq_len(row0)
    h = h_ref[...]
    h_ext = jnp.concatenate([hp_ref[...], h, hn_ref[...]], axis=0)
    n = tm + 2 * HALO
    pos = (row0 + lax.broadcasted_iota(jnp.int32, (tm, 1), 0)) & (seq_len - 1)
    first = pos == 0
    last = pos == seq_len - 1
    a_ext = jnp.dot(h_ext, wab_ref[...], preferred_element_type=F32)
    g = jnp.dot(h, wgb_ref[...], preferred_element_type=F32)
    a_mid = a_ext[HALO:HALO + tm, :]
    a_prev = jnp.where(first, 0.0, pltpu.roll(a_ext, 1, 0)[HALO:HALO + tm, :])
    a_next = jnp.where(last, 0.0, pltpu.roll(a_ext, n - 1, 0)[HALO:HALO + tm, :])
    cw = cw_ref[...]
    conv = cw[0:1, :] * a_prev + cw[1:2, :] * a_mid + cw[2:3, :] * a_next
    o_ref[...] = (_gelu_tanh(conv) * g).astype(BF16)


def _ffn_in(h, w_in, conv, w_out, layer, tm=1024, tn=512):
    per = tm // HALO
    last = N_TOK // HALO - 1
    ng = D_FF // tn
    n_i = N_TOK // tm
    per_j = D_FF // ng // CAST_ROWS
    assert per_j <= n_i and per_j * CAST_ROWS * ng == D_FF

    def cast_block(j, i):
        return j * per_j + jnp.minimum(i, per_j - 1)

    return pl.pallas_call(
        functools.partial(_ffn_in_kernel, tm=tm, tn=tn),
        grid=(ng, n_i),
        in_specs=[pl.BlockSpec((tm, D_MODEL), lambda j, i: (i, 0)),
                  pl.BlockSpec((HALO, D_MODEL), lambda j, i: (jnp.maximum(i * per - 1, 0), 0)),
                  pl.BlockSpec((HALO, D_MODEL), lambda j, i: (jnp.minimum((i + 1) * per, last), 0)),
                  pl.BlockSpec((None, D_MODEL, tn), lambda j, i: (layer, 0, j)),
                  pl.BlockSpec((None, D_MODEL, tn), lambda j, i: (layer, 0, ng + j)),
                  pl.BlockSpec((None, SHORT_K, tn), lambda j, i: (layer, 0, j)),
                  pl.BlockSpec((None, CAST_ROWS, D_MODEL), lambda j, i: (layer, cast_block(j, i), 0))],
        out_specs=[pl.BlockSpec((tm, tn), lambda j, i: (i, j)),
                   pl.BlockSpec((CAST_ROWS, D_MODEL), lambda j, i: (cast_block(j, i), 0))],
        out_shape=[jax.ShapeDtypeStruct((N_TOK, D_FF), BF16),
                   jax.ShapeDtypeStruct((D_FF, D_MODEL), BF16)],
        scratch_shapes=[pltpu.VMEM((D_MODEL, tn), BF16), pltpu.VMEM((D_MODEL, tn), BF16)],
        compiler_params=_params("arbitrary", "arbitrary"),
        name="ffn_in",
    )(h, h, h, w_in, w_in, conv, w_out)


def _ffn_out_kernel(u_ref, w_ref, x_ref, gate_ref, o_ref, *, tm, row_start):
    row0 = row_start + pl.program_id(1) * tm
    o = jnp.dot(u_ref[...], w_ref[...], preferred_element_type=F32)
    o_ref[...] = x_ref[...] + _mod_row(gate_ref, row0) * o


def _ffn_out(u, w_bf16, x, mod, layer, row_start=0, n_rows=N_TOK, tm=512, tn=1024):
    t0 = row_start // tm
    return pl.pallas_call(
        functools.partial(_ffn_out_kernel, tm=tm, row_start=row_start),
        grid=(D_MODEL // tn, n_rows // tm),
        in_specs=[pl.BlockSpec((tm, D_FF), lambda j, i: (t0 + i, 0)),
                  pl.BlockSpec((D_FF, tn), lambda j, i: (0, j)),
                  pl.BlockSpec((tm, tn), lambda j, i: (t0 + i, j)),
                  _mod_spec(layer, 5, tn, 0)],
        out_specs=pl.BlockSpec((tm, tn), lambda j, i: (i, j)),
        out_shape=jax.ShapeDtypeStruct((n_rows, D_MODEL), F32),
        compiler_params=_params("arbitrary", "arbitrary"),
        name="ffn_out",
    )(u, w_bf16, x, mod)


CONF_TR = 256
CONF_PAD = CONF_K // 2
CONF_RB = 128
CONF_CB = 128
CONF_LN_ROWS = 16


def _conformer_tile(ca_ref, cg_ref, pa_ref, pg_ref, na_ref, ng_ref, w_ref, b_ref, lg_ref, lb_ref, o_ref,
                    u_ref, v_ref, tile):
    tr = CONF_TR
    row0 = tile * tr
    seq_len = _seq_len(row0)
    pos0 = row0 & (seq_len - 1)

    def glu(a_ref, g_ref):
        return a_ref[...].astype(F32) * jax.nn.sigmoid(g_ref[...].astype(F32))

    u_ref[0:HALO, :] = jnp.where(pos0 == 0, 0.0, glu(pa_ref, pg_ref))
    u_ref[HALO:HALO + tr, :] = glu(ca_ref, cg_ref)
    u_ref[HALO + tr:, :] = jnp.where(pos0 + tr == seq_len, 0.0, glu(na_ref, ng_ref))
    base = HALO - CONF_PAD
    n_in = CONF_RB + 2 * HALO
    for cb in range(W_C // CONF_CB):
        cols = slice(cb * CONF_CB, (cb + 1) * CONF_CB)
        w = w_ref[:, cols]
        for rb in range(tr // CONF_RB):
            u_blk = u_ref[rb * CONF_RB:rb * CONF_RB + n_in, cols]
            acc = jnp.zeros((CONF_RB, CONF_CB), F32) + b_ref[:, cols]
            for r in range(8):
                u_r = pltpu.roll(u_blk, n_in - (base + r), 0)
                for a in range(len(range(r, CONF_K, 8))):
                    k = 8 * a + r
                    acc = acc + w[k:k + 1, :] * u_r[8 * a:8 * a + CONF_RB, :]
            v_ref[rb * CONF_RB:(rb + 1) * CONF_RB, cols] = acc
    for c in range(tr // CONF_LN_ROWS):
        rows = slice(c * CONF_LN_ROWS, (c + 1) * CONF_LN_ROWS)
        v = v_ref[rows, :]
        mu = jnp.mean(v, axis=-1, keepdims=True)
        d = v - mu
        var = jnp.mean(d * d, axis=-1, keepdims=True)
        y = d * lax.rsqrt(var + EPS) * lg_ref[...] + lb_ref[...]
        o_ref[rows, :] = (y * jax.nn.sigmoid(y)).astype(BF16)


def _conformer_kernel(*refs):
    _conformer_tile(*refs, pl.program_id(0))


def _conformer(proj, dw, dw_b, ln_g, ln_b):
    pa, na = _halo_specs(CONF_TR, W_C, 0)
    pg, ng = _halo_specs(CONF_TR, W_C, 1)
    row = pl.BlockSpec((1, W_C), lambda i: (0, 0))
    return pl.pallas_call(
        _conformer_kernel,
        grid=(N_TOK // CONF_TR,),
        in_specs=[pl.BlockSpec((CONF_TR, W_C), lambda i: (i, 0)),
                  pl.BlockSpec((CONF_TR, W_C), lambda i: (i, 1)),
                  pa, pg, na, ng,
                  pl.BlockSpec((CONF_K, W_C), lambda i: (0, 0)), row, row, row],
        out_specs=pl.BlockSpec((CONF_TR, W_C), lambda i: (i, 0)),
        out_shape=jax.ShapeDtypeStruct((N_TOK, W_C), BF16),
        scratch_shapes=[pltpu.VMEM((CONF_TR + 2 * HALO, W_C), F32), pltpu.VMEM((CONF_TR, W_C), F32)],
        compiler_params=_params("arbitrary"),
        name="conformer",
    )(proj, proj, proj, proj, proj, proj, dw, dw_b.reshape(1, W_C), ln_g.reshape(1, W_C), ln_b.reshape(1, W_C))


HY_COL0 = 2 * W_C // W_D


def _hy_short_kernel(*refs, tr):
    (x0_ref, x1_ref, vv_ref, p0_ref, p1_ref, pv_ref, n0_ref, n1_ref, nv_ref,
     w0_ref, w1_ref, wv_ref, b0_ref, b1_ref, bv_ref, x0o_ref, wo_ref, tmp_ref) = refs
    row0 = pl.program_id(0) * tr
    seq_len = _seq_len(row0)

    def conv(c_ref, p_ref, n_ref, w_ref, b_ref):
        p = c_ref[...].astype(F32)
        p_prev, p_next = _shift_rows(p, p_ref[HALO - 1:HALO, :].astype(F32), n_ref[0:1, :].astype(F32),
                                     row0, seq_len)
        w = w_ref[...]
        return w[0:1, :] * p_prev + w[1:2, :] * p + w[2:3, :] * p_next + b_ref[...]

    _store_even_odd(x0o_ref, tmp_ref, conv(x0_ref, p0_ref, n0_ref, w0_ref, b0_ref))
    x1 = conv(x1_ref, p1_ref, n1_ref, w1_ref, b1_ref)
    vv = conv(vv_ref, pv_ref, nv_ref, wv_ref, bv_ref)
    _store_even_odd(wo_ref, tmp_ref, vv * x1)


def _store_even_odd(o_ref, tmp_ref, val):
    n, width = val.shape
    for cb in range(width // LANES):
        cols = slice(cb * LANES, (cb + 1) * LANES)
        tmp_ref[cb] = val[:, cols]
        for parity in (0, 1):
            dst = slice(parity * width + cb * LANES, parity * width + (cb + 1) * LANES)
            o_ref[:, dst] = tmp_ref[cb, pl.ds(parity, n // 2, stride=2), :].astype(BF16)


def _hy_short(proj, hy_short, hy_short_b, tr=1024):
    main = [pl.BlockSpec((tr, W_D), functools.partial(lambda i, c: (i, c), c=HY_COL0 + c)) for c in range(3)]
    halos = [_halo_specs(tr, W_D, HY_COL0 + c) for c in range(3)]
    wspec = [pl.BlockSpec((SHORT_K, W_D), functools.partial(lambda i, c: (0, c), c=c)) for c in range(3)]
    bspec = [pl.BlockSpec((1, W_D), functools.partial(lambda i, c: (0, c), c=c)) for c in range(3)]
    out = jax.ShapeDtypeStruct((N_TOK // 2, W2_D), BF16)
    ospec = pl.BlockSpec((tr // 2, W2_D), lambda i: (i, 0))
    b2 = hy_short_b.reshape(1, 3 * W_D)
    return pl.pallas_call(
        functools.partial(_hy_short_kernel, tr=tr),
        grid=(N_TOK // tr,),
        in_specs=main + [h[0] for h in halos] + [h[1] for h in halos] + wspec + bspec,
        out_specs=[ospec, ospec],
        out_shape=[out, out],
        scratch_shapes=[pltpu.VMEM((W_D // LANES, tr, LANES), F32)],
        compiler_params=_params("arbitrary"),
        name="hy_short",
    )(*([proj] * 9), hy_short, hy_short, hy_short, b2, b2, b2)


HIGHEST = lax.Precision.HIGHEST
HY_EMB_PAD = 128
W2_D = 2 * W_D


def _dft_gen_kernel(ct_ref, st_ref, cr_ref, sr_ref, c_ref, s_ref):
    t = pl.program_id(0)
    cr = cr_ref[pl.ds(t, 1), :]
    sr = sr_ref[pl.ds(t, 1), :]
    ct = ct_ref[...]
    st = st_ref[...]
    c_ref[...] = (cr * ct - sr * st).astype(BF16)
    s_ref[...] = (sr * ct + cr * st).astype(BF16)


def _dft_matrices(L):
    tr = min(L, 256)
    nt = L // tr
    nt_pad = -(-nt // 8) * 8
    n = jnp.arange(L, dtype=jnp.int32)[None, :]

    def tables(k):
        ang = ((k[:, None] * n) % (2 * L)).astype(F32) * (math.pi / L)
        return jnp.cos(ang), jnp.sin(ang)

    ct, st = tables(jnp.arange(tr, dtype=jnp.int32))
    cr, sr = tables(jnp.arange(nt_pad, dtype=jnp.int32) * tr)
    tile = pl.BlockSpec((tr, L), lambda t: (0, 0))
    rows = pl.BlockSpec((nt_pad, L), lambda t: (0, 0))
    out = pl.BlockSpec((tr, L), lambda t: (t, 0))
    return pl.pallas_call(
        _dft_gen_kernel,
        grid=(nt,),
        in_specs=[tile, tile, rows, rows],
        out_specs=[out, out],
        out_shape=[jax.ShapeDtypeStruct((L, L), BF16)] * 2,
        compiler_params=_params("arbitrary"),
        name="dft_gen",
    )(ct, st, cr, sr)


def _hy_features(L):
    t = jnp.linspace(0.0, 1.0, L, dtype=F32)[:, None]
    bands = (HY_EMB - 1) // 2
    ang = 2 * math.pi * jnp.arange(L, dtype=F32)[:, None] / L
    freqs = jnp.linspace(1e-4, bands - 1, bands, dtype=F32)[None, :]
    z = jnp.concatenate([t, jnp.cos(freqs * ang), -jnp.sin(freqs * ang)], axis=-1)
    return jnp.pad(z, ((0, 0), (0, HY_EMB_PAD - HY_EMB))), t


def _hy_filter_kernel(z_ref, t_ref, w1_ref, b1_ref, f1_ref, w2_ref, b2_ref, f2_ref, w3_ref, dl_ref,
                      fs_ref, fd_ref, hn_ref, tmp_ref, *, tr):
    i = pl.program_id(0)
    hid = jnp.sin(f1_ref[...] * (jnp.dot(z_ref[...], w1_ref[...], precision=HIGHEST,
                                         preferred_element_type=F32) + b1_ref[...]))
    hid = jnp.sin(f2_ref[...] * (jnp.dot(hid, w2_ref[...], precision=HIGHEST,
                                         preferred_element_type=F32) + b2_ref[...]))
    hf = jnp.dot(hid.astype(BF16), w3_ref[...].astype(BF16), preferred_element_type=F32)
    decay = jnp.exp(-t_ref[...] * jnp.abs(dl_ref[...]))
    ridx = i * tr + lax.broadcasted_iota(jnp.int32, (tr, 1), 0)
    fwd = hf[:, :W_D] * decay
    bwd = jnp.where(ridx == 0, 0.0, hf[:, W_D:] * decay)
    fsum = fwd + bwd
    fdif = bwd - fwd
    _store_even_odd(fs_ref, tmp_ref, fsum)
    _store_even_odd(fd_ref, tmp_ref, fdif)
    sgn = (1 - (ridx & 2)).astype(F32)
    even = (ridx & 1) == 0
    hc_part = jnp.sum(jnp.where(even, sgn, 0.0) * fsum, axis=0, keepdims=True)
    hs_part = jnp.sum(jnp.where(even, 0.0, sgn) * fdif, axis=0, keepdims=True)

    @pl.when(i == 0)
    def _():
        hn_ref[...] = jnp.zeros_like(hn_ref)

    hn_ref[0:1, :] += hc_part
    hn_ref[1:2, :] += hs_part


def _hy_filter(L, w1, b1, f1, w2, b2, f2, w3):
    tr = min(L, 512)
    z, t = _hy_features(L)
    max_decay = math.log(HY_TARGET) / HY_FAST_PCT
    min_decay = math.log(HY_TARGET) / HY_SLOW_PCT
    deltas = jnp.linspace(min_decay, max_decay, W_D, dtype=F32)[None, :]
    w1p = jnp.pad(w1, ((0, HY_EMB_PAD - HY_EMB), (0, 0)))
    full = lambda shape: pl.BlockSpec(shape, lambda i: (0, 0))
    return pl.pallas_call(
        functools.partial(_hy_filter_kernel, tr=tr),
        grid=(L // tr,),
        in_specs=[pl.BlockSpec((tr, HY_EMB_PAD), lambda i: (i, 0)),
                  pl.BlockSpec((tr, 1), lambda i: (i, 0)),
                  full((HY_EMB_PAD, HY_HIDDEN)), full((1, HY_HIDDEN)), full((1, HY_HIDDEN)),
                  full((HY_HIDDEN, HY_HIDDEN)), full((1, HY_HIDDEN)), full((1, HY_HIDDEN)),
                  full((HY_HIDDEN, 2 * W_D)), full((1, W_D))],
        out_specs=[pl.BlockSpec((tr // 2, W2_D), lambda i: (i, 0)),
                   pl.BlockSpec((tr // 2, W2_D), lambda i: (i, 0)),
                   pl.BlockSpec((8, W_D), lambda i: (0, 0))],
        out_shape=[jax.ShapeDtypeStruct((L // 2, W2_D), BF16),
                   jax.ShapeDtypeStruct((L // 2, W2_D), BF16),
                   jax.ShapeDtypeStruct((8, W_D), F32)],
        scratch_shapes=[pltpu.VMEM((W_D // LANES, tr, LANES), F32)],
        compiler_params=_params("arbitrary"),
        name="hy_filter",
    )(z, t, w1p, b1.reshape(1, -1), f1.reshape(1, -1), w2, b2.reshape(1, -1), f2.reshape(1, -1), w3, deltas)


def _twiddles(L):
    k = jnp.arange(L // 2, dtype=F32)[:, None] * (math.pi / L)
    return jnp.cos(k), jnp.sin(k)


def _butterfly(c_ref, s_ref, x2, twc, tws):
    xc2 = jnp.dot(c_ref[...], x2, preferred_element_type=F32)
    xs2 = jnp.dot(s_ref[...], x2, preferred_element_type=F32)
    ec, oc = xc2[:, :W_D], xc2[:, W_D:]
    es, os_ = xs2[:, :W_D], xs2[:, W_D:]
    tc = twc * oc - tws * os_
    ts = twc * os_ + tws * oc
    return ec + tc, es + ts, ec - tc, es - ts


def _hy_spec_kernel(c_ref, s_ref, fs_ref, fd_ref, twc_ref, tws_ref, hac_ref, has_ref, hrc_ref, hrs_ref):
    twc, tws = twc_ref[...], tws_ref[...]
    ac, _, bc, _ = _butterfly(c_ref, s_ref, fs_ref[...], twc, tws)
    _, as_, _, bs = _butterfly(c_ref, s_ref, fd_ref[...], twc, tws)
    hac_ref[...] = ac
    has_ref[...] = as_
    hrc_ref[...] = bc
    hrs_ref[...] = -bs


def _hy_spectrum(cmat, smat, fs2, fd2, twc, tws, tk):
    M = cmat.shape[0]
    tile = pl.BlockSpec((tk, M), lambda i: (i, 0))
    full = pl.BlockSpec((M, W2_D), lambda i: (0, 0), pipeline_mode=pl.Buffered(1))
    col = pl.BlockSpec((tk, 1), lambda i: (i, 0))
    out = pl.BlockSpec((tk, W_D), lambda i: (i, 0))
    return pl.pallas_call(
        _hy_spec_kernel,
        grid=(M // tk,),
        in_specs=[tile, tile, full, full, col, col],
        out_specs=[out] * 4,
        out_shape=[jax.ShapeDtypeStruct((M, W_D), F32)] * 4,
        compiler_params=_params("arbitrary"),
        name="hy_spectrum",
    )(cmat, smat, fs2, fd2, twc, tws)


def _hy_fwd_kernel(c_ref, s_ref, w_ref, hac_ref, has_ref, hrc_ref, hrs_ref, twc_ref, tws_ref, hn_ref,
                   vc_ref, vs_ref, yn_ref, *, L, tk, group):
    kt = pl.program_id(1)
    M = L // 2
    twc, tws = twc_ref[...], tws_ref[...]
    hac, has, hrc, hrs = hac_ref[...], has_ref[...], hrc_ref[...], hrs_ref[...]
    kidx = kt * tk + lax.broadcasted_iota(jnp.int32, (tk, 1), 0)
    om = jnp.where(kidx == 0, 0.5 / L, 1.0 / L)
    for g in range(group):
        w2 = w_ref[g * M:(g + 1) * M, :]
        rows = slice(g * tk, (g + 1) * tk)
        ac, as_, bc, bs = _butterfly(c_ref, s_ref, w2, twc, tws)
        yac = ac * hac + as_ * has
        yai = ac * has - as_ * hac
        pc = bc * hrc - bs * hrs
        pi = -(bc * hrs + bs * hrc)
        dc, di = yac - pc, yai - pi
        vc_ref[rows, :] = (om * jnp.concatenate([yac + pc, twc * dc - tws * di], axis=1)).astype(BF16)
        vs_ref[rows, :] = ((-1.0 / L) * jnp.concatenate([yai + pi, twc * di + tws * dc], axis=1)).astype(BF16)

    @pl.when(kt == 0)
    def _():
        m = lax.broadcasted_iota(jnp.int32, (M, 1), 0)
        sgn = (1 - 2 * (m & 1)).astype(F32)
        hc, hs = hn_ref[0:1, :], hn_ref[1:2, :]
        for g in range(group):
            xn2 = jnp.sum(w_ref[g * M:(g + 1) * M, :].astype(F32) * sgn, axis=0, keepdims=True)
            en, on = xn2[:, :W_D], xn2[:, W_D:]
            yn = jnp.concatenate([en * hc + on * hs, on * hc - en * hs], axis=1) * (1.0 / L)
            yn_ref[g] = jnp.broadcast_to(yn, yn_ref.shape[1:])


HY_GROUP = 4


def _hy_group(M, tk, nb, blk0):
    ok = M == tk and nb % HY_GROUP == 0 and blk0 % HY_GROUP == 0
    return HY_GROUP if ok else 1


def _hy_forward(cmat, smat, w2, h4, twc, tws, hn, L, nb, blk0, tk):
    M = L // 2
    g = _hy_group(M, tk, nb, blk0)
    tile = pl.BlockSpec((tk, M), lambda b, k: (k, 0))
    htile = pl.BlockSpec((tk, W_D), lambda b, k: (k, 0))
    col = pl.BlockSpec((tk, 1), lambda b, k: (k, 0))
    out = pl.BlockSpec((g * tk, W2_D), lambda b, k: (b * (M // tk) + k, 0))
    return pl.pallas_call(
        functools.partial(_hy_fwd_kernel, L=L, tk=tk, group=g),
        grid=(nb // g, M // tk),
        in_specs=[tile, tile, pl.BlockSpec((g * M, W2_D), lambda b, k: (blk0 // g + b, 0)),
                  htile, htile, htile, htile, col, col,
                  pl.BlockSpec((8, W_D), lambda b, k: (0, 0))],
        out_specs=[out, out, pl.BlockSpec((g, 8, W2_D), lambda b, k: (b, 0, 0))],
        out_shape=[jax.ShapeDtypeStruct((nb * M, W2_D), BF16),
                   jax.ShapeDtypeStruct((nb * M, W2_D), BF16),
                   jax.ShapeDtypeStruct((nb, 8, W2_D), F32)],
        compiler_params=_params("arbitrary", "arbitrary"),
        name="hy_forward",
    )(cmat, smat, w2, *h4, twc, tws, hn)


def _hy_inv_kernel(c_ref, s_ref, vc_ref, vs_ref, yn_ref, x0_ref, w_ref, bias_ref, z_ref, tmp_ref,
                   *, M, tt, group):
    ti = pl.program_id(1)
    m = ti * tt + lax.broadcasted_iota(jnp.int32, (tt, 1), 0)
    sgn = (1 - 2 * (m & 1)).astype(F32)
    for g in range(group):
        seq = slice(g * M, (g + 1) * M)
        rows = slice(g * tt, (g + 1) * tt)
        y = (jnp.dot(c_ref[...], vc_ref[seq, :], preferred_element_type=F32)
             + jnp.dot(s_ref[...], vs_ref[seq, :], preferred_element_type=F32))
        y = y + sgn * yn_ref[g, 0:1, :] + w_ref[rows, :].astype(F32) * bias_ref[...]
        z2 = x0_ref[rows, :].astype(F32) * y
        for cb in range(W_D // LANES):
            cols = slice(cb * LANES, (cb + 1) * LANES)
            tmp_ref[cb, pl.ds(0, tt, stride=2), :] = z2[:, cols]
            tmp_ref[cb, pl.ds(1, tt, stride=2), :] = z2[:, W_D + cb * LANES:W_D + (cb + 1) * LANES]
            z_ref[g * 2 * tt:(g + 1) * 2 * tt, cols] = tmp_ref[cb].astype(BF16)


def _hy_inverse(cmat, smat, vc, vs, yn, x02, w2, bias, L, nb, blk0, tt):
    M = L // 2
    per = M // tt
    g = _hy_group(M, tt, nb, blk0)
    tile = pl.BlockSpec((tt, M), lambda b, t: (t, 0))
    seq = pl.BlockSpec((g * M, W2_D), lambda b, t: (b, 0))
    rows_in = pl.BlockSpec((g * tt, W2_D), lambda b, t: ((blk0 // g + b) * per + t, 0))
    return pl.pallas_call(
        functools.partial(_hy_inv_kernel, M=M, tt=tt, group=g),
        grid=(nb // g, per),
        in_specs=[tile, tile, seq, seq,
                  pl.BlockSpec((g, 8, W2_D), lambda b, t: (b, 0, 0)),
                  rows_in, rows_in,
                  pl.BlockSpec((1, W2_D), lambda b, t: (0, 0))],
        out_specs=pl.BlockSpec((2 * g * tt, W_D), lambda b, t: (b * per + t, 0)),
        out_shape=jax.ShapeDtypeStruct((nb * L, W_D), BF16),
        scratch_shapes=[pltpu.VMEM((W_D // LANES, 2 * tt, LANES), F32)],
        compiler_params=_params("arbitrary", "arbitrary"),
        name="hy_inverse",
    )(cmat, smat, vc, vs, yn, x02, w2, jnp.concatenate([bias, bias]).reshape(1, W2_D))


def _hyena_long_conv(x02, w2, L, nb, blk0, filt_params, bias):
    M = L // 2
    tk = min(M, 256)
    cmat, smat = _dft_matrices(M)
    twc, tws = _twiddles(L)
    fs2, fd2, hn = _hy_filter(L, *filt_params)
    h4 = _hy_spectrum(cmat, smat, fs2, fd2, twc, tws, tk)
    vc, vs, yn = _hy_forward(cmat, smat, w2, h4, twc, tws, hn, L, nb, blk0, tk)
    return _hy_inverse(cmat, smat, vc, vs, yn, x02, w2, bias, L, nb, blk0, tk)


def kernel(x_prompt, x_sample, cache_k, cache_v, c, c_ctx, ada_w, ada_b, norm_mix, norm_ffn,
           e_w_in, e_conv_a, e_q_norm, e_k_norm, e_rpb, e_w_out,
           o_w_in, o_conf_dw, o_conf_dw_b, o_conf_ln_g, o_conf_ln_b, o_hy_short, o_hy_short_b,
           o_hy_w1, o_hy_b1, o_hy_f1, o_hy_w2, o_hy_b2, o_hy_f2, o_hy_w3, o_hy_bias, o_w_out,
           ffn_in, ffn_conv, ffn_out):
    x_parts = (x_prompt.reshape(N_P, D_MODEL), x_sample.reshape(N_S, D_MODEL))
    cvec = jnp.concatenate([c_ctx[None, :], c, jnp.zeros((N_SEG_PAD - 1 - DEC_BATCH, D_MODEL), F32)], axis=0)
    mod = _adaln(cvec, ada_w, ada_b)
    ks_new, vs_new = [], []
    for layer in range(DEPTH):
        j = layer // 2
        last = layer == DEPTH - 1
        h = _norm_mod_call(x_parts, norm_mix[layer], mod, layer, 0)
        if layer % 2 == 0:
            proj, w_out = _proj(h, e_w_in[j], e_w_out[j])
            ya = (_short_gated_conv(proj, e_conv_a[j]),)
            yb_p, k_p, v_p = _ctx_attention(proj, e_q_norm[j], e_k_norm[j])
            yb_s = _na_attention(proj, cache_k[:, j], cache_v[:, j], e_rpb[j], e_q_norm[j], e_k_norm[j])
            yb = (yb_p, yb_s)
            ks_new.append(k_p.reshape(BATCH, SEQ, NA_HEADS, NA_HEAD_DIM))
            vs_new.append(v_p.reshape(BATCH, SEQ, NA_HEADS, NA_HEAD_DIM))
        else:
            proj, w_out = _proj(h, o_w_in[j], o_w_out[j])
            ya = (_conformer(proj, o_conf_dw[j], o_conf_dw_b[j], o_conf_ln_g[j], o_conf_ln_b[j]),)
            x0, w = _hy_short(proj, o_hy_short[j], o_hy_short_b[j])
            fp = (o_hy_w1[j], o_hy_b1[j], o_hy_f1[j], o_hy_w2[j], o_hy_b2[j], o_hy_f2[j], o_hy_w3[j])
            z_p = _hyena_long_conv(x0, w, SEQ, BATCH, 0, fp, o_hy_bias[j])
            z_s = _hyena_long_conv(x0, w, DEC_SEQ, DEC_BATCH, N_P // DEC_SEQ, fp, o_hy_bias[j])
            yb = (z_p, z_s)
        x, h_ffn = _out_proj(ya, yb, w_out, x_parts, mod, layer, norm_ffn[layer])
        u, w_ffn_out = _ffn_in(h_ffn, ffn_in, ffn_conv, ffn_out, layer)
        if last:
            x_parts = (_ffn_out(u, w_ffn_out, x, mod, layer, 0, N_P),
                       _ffn_out(u, w_ffn_out, x, mod, layer, N_P, N_S))
        else:
            x_parts = (_ffn_out(u, w_ffn_out, x, mod, layer),)
    xp = x_parts[0].reshape(BATCH, SEQ, D_MODEL)
    xs = x_parts[1].reshape(DEC_BATCH, DEC_SEQ, D_MODEL)
    return (xp, xs, jnp.stack(ks_new, axis=1), jnp.stack(vs_new, axis=1))
```

```python
import functools
import math

import jax
import jax.numpy as jnp
from jax import lax
from jax.experimental import pallas as pl
from jax.experimental.pallas import tpu as pltpu

D_MODEL = 2048
BATCH = 16
SEQ = 256
DEPTH = 2
DEC_BATCH = 2
DEC_SEQ = 4096
PAST_LEN = 512
GRID_W = 64
W_A = 1024
NA_HEADS = 8
NA_HEAD_DIM = 128
W_B = NA_HEADS * NA_HEAD_DIM
W_C = 1024
W_D = 1024
NA_WIN_R = 8
NA_WIN_C = 16
SHORT_K = 3
CONF_K = 31
D_FF = 5632
HY_EMB = 33
HY_HIDDEN = 64
HY_FAST_PCT = 0.3
HY_SLOW_PCT = 1.5
HY_TARGET = 1e-2
EPS = 1e-6
NEG_INF = -1e30

N_P = BATCH * SEQ
N_S = DEC_BATCH * DEC_SEQ
N_TOK = N_P + N_S
SEG = DEC_SEQ
N_SEG_PAD = 8
N_MOD = 6 * D_MODEL
HALO = 16
LANES = 128
VMEM_LIMIT_BYTES = 60 * 1024 * 1024
BF16 = jnp.bfloat16
F32 = jnp.float32


def _params(*sem):
    return pltpu.CompilerParams(dimension_semantics=sem, vmem_limit_bytes=VMEM_LIMIT_BYTES)


def _seq_len(row0):
    return jnp.where(row0 < N_P, SEQ, DEC_SEQ)


def _mod_row(mod_ref, row0):
    return mod_ref[pl.ds(row0 // SEG, 1), :]


def _mod_spec(layer, blk, tn, index_pos):
    per = D_MODEL // tn

    def imap(*idx):
        j = idx[index_pos] if index_pos is not None else 0
        return (layer, 0, blk * per + j)

    return pl.BlockSpec((None, N_SEG_PAD, tn), imap)


def _adaln_kernel(c_ref, w_ref, b_ref, o_ref):
    c = c_ref[...]
    s = (c * jax.nn.sigmoid(c)).astype(BF16)
    o_ref[...] = jnp.dot(s, w_ref[...].astype(BF16), preferred_element_type=F32) + b_ref[...]


def _adaln(cvec, ada_w, ada_b, tn=1024):
    return pl.pallas_call(
        _adaln_kernel,
        grid=(DEPTH, N_MOD // tn),
        in_specs=[pl.BlockSpec((N_SEG_PAD, D_MODEL), lambda l, j: (0, 0)),
                  pl.BlockSpec((None, D_MODEL, tn), lambda l, j: (l, 0, j)),
                  pl.BlockSpec((None, 1, tn), lambda l, j: (l, 0, j))],
        out_specs=pl.BlockSpec((None, N_SEG_PAD, tn), lambda l, j: (l, 0, j)),
        out_shape=jax.ShapeDtypeStruct((DEPTH, N_SEG_PAD, N_MOD), F32),
        compiler_params=_params("arbitrary", "arbitrary"),
        name="adaln",
    )(cvec, ada_w, ada_b.reshape(DEPTH, 1, N_MOD))


def _norm_mod(x, g, shift, scale):
    y = x * lax.rsqrt(jnp.mean(x * x, axis=-1, keepdims=True) + EPS) * g
    return y * (1.0 + scale) + shift


def _stream_specs(x_parts, tr, width, pos):
    if len(x_parts) == 1:
        return [pl.BlockSpec((tr, width), lambda *idx: (idx[pos], 0))]
    n_pt = N_P // tr
    return [pl.BlockSpec((tr, width), lambda *idx: (jnp.minimum(idx[pos], n_pt - 1), 0)),
            pl.BlockSpec((tr, width), lambda *idx: (jnp.maximum(idx[pos] - n_pt, 0), 0))]


def _stream_tile(x_refs, row0):
    if len(x_refs) == 1:
        return x_refs[0][...]
    return jnp.where(row0 < N_P, x_refs[0][...], x_refs[1][...])


def _norm_mod_kernel(*refs, tr, n_parts):
    x_refs, (g_ref, sh_ref, sc_ref, o_ref) = refs[:n_parts], refs[n_parts:]
    row0 = pl.program_id(0) * tr
    x = _stream_tile(x_refs, row0)
    o_ref[...] = _norm_mod(x, g_ref[...], _mod_row(sh_ref, row0), _mod_row(sc_ref, row0)).astype(BF16)


def _norm_mod_call(x_parts, g, mod, layer, blk, tr=1024):
    return pl.pallas_call(
        functools.partial(_norm_mod_kernel, tr=tr, n_parts=len(x_parts)),
        grid=(N_TOK // tr,),
        in_specs=_stream_specs(x_parts, tr, D_MODEL, 0) + [
            pl.BlockSpec((1, D_MODEL), lambda i: (0, 0)),
            _mod_spec(layer, blk, D_MODEL, None),
            _mod_spec(layer, blk + 1, D_MODEL, None)],
        out_specs=pl.BlockSpec((tr, D_MODEL), lambda i: (i, 0)),
        out_shape=jax.ShapeDtypeStruct((N_TOK, D_MODEL), BF16),
        compiler_params=_params("arbitrary"),
        name="norm_mod",
    )(*x_parts, g.reshape(1, D_MODEL), mod, mod)


CAST_ROWS = 128


def _proj_kernel(h_ref, w_ref, wo_ref, o_ref, wob_ref, wb_ref):
    @pl.when(pl.program_id(1) == 0)
    def _():
        wb_ref[...] = w_ref[...].astype(BF16)

    wob_ref[...] = wo_ref[...].astype(BF16)
    o_ref[...] = jnp.dot(h_ref[...], wb_ref[...], preferred_element_type=F32).astype(o_ref.dtype)


def _proj(h, w, w_out, tm=2048, tn=1024):
    m, k = h.shape
    n = w.shape[1]
    k2, n2 = w_out.shape
    n_i = m // tm
    n_cast = k2 // CAST_ROWS
    assert n_cast * CAST_ROWS == k2 and n_cast <= (n // tn) * n_i

    def cast_block(j, i):
        return (jnp.minimum(j * n_i + i, n_cast - 1), 0)

    return pl.pallas_call(
        _proj_kernel,
        grid=(n // tn, n_i),
        in_specs=[pl.BlockSpec((tm, k), lambda j, i: (i, 0)),
                  pl.BlockSpec((k, tn), lambda j, i: (0, j)),
                  pl.BlockSpec((CAST_ROWS, n2), cast_block)],
        out_specs=[pl.BlockSpec((tm, tn), lambda j, i: (i, j)),
                   pl.BlockSpec((CAST_ROWS, n2), cast_block)],
        out_shape=[jax.ShapeDtypeStruct((m, n), BF16),
                   jax.ShapeDtypeStruct((k2, n2), BF16)],
        scratch_shapes=[pltpu.VMEM((k, tn), BF16)],
        compiler_params=_params("arbitrary", "arbitrary"),
        name="proj",
    )(h, w, w_out)


def _shift_rows(p, prev_row, next_row, row0, seq_len):
    n = p.shape[0]
    ridx = lax.broadcasted_iota(jnp.int32, (n, 1), 0)
    pos = (row0 + ridx) & (seq_len - 1)
    p_prev = jnp.where(ridx == 0, prev_row, pltpu.roll(p, 1, 0))
    p_prev = jnp.where(pos == 0, 0.0, p_prev)
    p_next = jnp.where(ridx == n - 1, next_row, pltpu.roll(p, n - 1, 0))
    p_next = jnp.where(pos == seq_len - 1, 0.0, p_next)
    return p_prev, p_next


def _sconv_kernel(ab_ref, ac_ref, ax_ref, pc_ref, px_ref, nc_ref, nx_ref, w_ref, o_ref, *, tr):
    row0 = pl.program_id(0) * tr
    p = ac_ref[...].astype(F32) * ax_ref[...].astype(F32)
    prev_row = pc_ref[HALO - 1:HALO, :].astype(F32) * px_ref[HALO - 1:HALO, :].astype(F32)
    next_row = nc_ref[0:1, :].astype(F32) * nx_ref[0:1, :].astype(F32)
    p_prev, p_next = _shift_rows(p, prev_row, next_row, row0, _seq_len(row0))
    w = w_ref[...]
    conv = w[0:1, :] * p_prev + w[1:2, :] * p + w[2:3, :] * p_next
    o_ref[...] = (ab_ref[...].astype(F32) * conv).astype(BF16)


def _halo_specs(tr, width, col):
    per = tr // HALO
    last = N_TOK // HALO - 1
    prev = pl.BlockSpec((HALO, width), lambda i: (jnp.maximum(i * per - 1, 0), col))
    nxt = pl.BlockSpec((HALO, width), lambda i: (jnp.minimum((i + 1) * per, last), col))
    return prev, nxt


def _short_gated_conv(proj, conv_a, tr=1024):
    pc, nc = _halo_specs(tr, W_A, 1)
    px, nx = _halo_specs(tr, W_A, 2)
    return pl.pallas_call(
        functools.partial(_sconv_kernel, tr=tr),
        grid=(N_TOK // tr,),
        in_specs=[pl.BlockSpec((tr, W_A), lambda i: (i, 0)),
                  pl.BlockSpec((tr, W_A), lambda i: (i, 1)),
                  pl.BlockSpec((tr, W_A), lambda i: (i, 2)),
                  pc, px, nc, nx,
                  pl.BlockSpec((SHORT_K, W_A), lambda i: (0, 0))],
        out_specs=pl.BlockSpec((tr, W_A), lambda i: (i, 0)),
        out_shape=jax.ShapeDtypeStruct((N_TOK, W_A), BF16),
        compiler_params=_params("arbitrary"),
        name="short_gated_conv",
    )(proj, proj, proj, proj, proj, proj, proj, conv_a)


ROWS = DEC_SEQ // GRID_W
NA_QROWS = 8
NA_KROWS = 16
NA_NQ = NA_QROWS * GRID_W
NA_NK = NA_KROWS * GRID_W
NA_STEPS = ROWS // NA_QROWS
QKV_COL0 = 3 * W_A // NA_HEAD_DIM
HEAD_SCALE = NA_HEAD_DIM ** -0.5
LOG2E = math.log2(math.e)
NA_CHUNK = 16
NT_DIMS = (((1,), (1,)), ((), ()))


def _head_rmsnorm(x, g):
    return x * lax.rsqrt(jnp.mean(x * x, axis=-1, keepdims=True) + EPS) * g


def _na_key_row0(j):
    return jnp.clip(j * NA_QROWS - NA_WIN_R // 2, 0, ROWS - NA_KROWS)


NA_SLOT_LO = NA_QROWS
NA_SLOTS = 2 * NA_KROWS


def _na_geometry(j):
    kr0 = min(max(j * NA_QROWS - NA_WIN_R // 2, 0), ROWS - NA_KROWS)
    per_a = []
    for a in range(NA_QROWS):
        r = j * NA_QROWS + a
        r0 = min(max(r - NA_WIN_R // 2, 0), ROWS - NA_WIN_R)
        per_a.append((kr0 - r + NA_WIN_R - 1 + NA_SLOT_LO, r0 - kr0))
    return kr0, per_a


def _na_bias_slots(rpb):
    n_dr, n_dc = rpb.shape[1], rpb.shape[2]
    out = jax.ShapeDtypeStruct((NA_HEADS, GRID_W, NA_SLOTS * GRID_W), F32)
    spec = pl.BlockSpec((None, GRID_W, NA_SLOTS * GRID_W), lambda h: (h, 0, 0))
    return pl.pallas_call(
        functools.partial(_na_slots_kernel, n_dr=n_dr, n_dc=n_dc),
        grid=(NA_HEADS,),
        in_specs=[pl.BlockSpec(memory_space=pltpu.SMEM)],
        out_specs=[spec, spec],
        out_shape=[out, out],
        compiler_params=_params("arbitrary"),
        name="na_bias_slots",
    )(rpb.astype(F32).reshape(NA_HEADS, n_dr * n_dc))


def _na_slots_kernel(rpb_ref, even_ref, odd_ref, *, n_dr, n_dc):
    h = pl.program_id(0)
    pair = 2 * GRID_W
    qc = lax.broadcasted_iota(jnp.int32, (GRID_W, pair), 0)
    lane = lax.broadcasted_iota(jnp.int32, (GRID_W, pair), 1)
    kc = lane & (GRID_W - 1)
    delta = kc - qc + NA_WIN_C - 1
    cs = jnp.clip(qc - NA_WIN_C // 2, 0, GRID_W - NA_WIN_C)
    col_ok = (kc >= cs) & (kc < cs + NA_WIN_C)
    upper = lax.broadcasted_iota(jnp.int32, (1, pair), 1) >= GRID_W
    for shift, ref in ((0, even_ref), (1, odd_ref)):
        for t in range(NA_SLOTS // 2):
            drs = [2 * t + shift + half - NA_SLOT_LO for half in (0, 1)]
            acc = jnp.full((GRID_W, pair), NEG_INF, F32)
            if any(0 <= dr < n_dr for dr in drs):
                for dc in range(n_dc):
                    lo, hi = [rpb_ref[h, dr * n_dc + dc] if 0 <= dr < n_dr else NEG_INF for dr in drs]
                    acc = jnp.where(delta == dc, jnp.where(upper, hi, lo), acc)
                acc = jnp.where(col_ok, acc, NEG_INF)
            ref[:, t * pair:(t + 1) * pair] = acc


def _na_build_bias(j_static, even_ref, odd_ref, bias_ref):
    _, per_a = _na_geometry(j_static)
    key_row = lax.broadcasted_iota(jnp.int32, (1, NA_NK), 1) // GRID_W
    for a, (m, lo) in enumerate(per_a):
        src = even_ref if m % 2 == 0 else odd_ref
        start = (m - m % 2) * GRID_W
        slab = src[:, start:start + NA_NK] * LOG2E
        ok = (key_row >= lo) & (key_row < lo + NA_WIN_R)
        bias_ref[a * GRID_W:(a + 1) * GRID_W, :] = jnp.where(ok, slab, NEG_INF)


def _with_ones_column(v):
    lane = lax.broadcasted_iota(jnp.int32, v.shape, 1)
    return jnp.concatenate([v.astype(BF16), jnp.where(lane == 0, 1.0, 0.0).astype(BF16)], axis=1)


def _na_kernel(q_ref, k_ref, v_ref, ck_ref, cv_ref, even_ref, odd_ref, qg_ref, kg_ref, o_ref,
               kn_ref, vx_ref, kall_ref, vall_ref, bias_ref, s_ref, p_ref):
    j = pl.program_id(2)
    hd = NA_HEAD_DIM

    @pl.when(j == 0)
    def _():
        kn_ref[...] = _head_rmsnorm(k_ref[...].astype(F32), kg_ref[...]).astype(BF16)
        vx_ref[...] = _with_ones_column(v_ref[...])
        kall_ref[NA_NK:, :] = ck_ref[...].astype(BF16)
        vall_ref[NA_NK:, :] = _with_ones_column(cv_ref[...])

    for j_static in (0, 1, NA_STEPS - 1):
        @pl.when(j == j_static)
        def _(j_static=j_static):
            _na_build_bias(j_static, even_ref, odd_ref, bias_ref)

    q = (_head_rmsnorm(q_ref[...].astype(F32), qg_ref[...]) * (HEAD_SCALE * LOG2E)).astype(BF16)
    tok0 = pl.multiple_of(_na_key_row0(j) * GRID_W, GRID_W)
    kall_ref[:NA_NK, :] = kn_ref[pl.ds(tok0, NA_NK), :]
    vall_ref[:NA_NK, :] = vx_ref[pl.ds(tok0, NA_NK), :]
    s_ref[...] = lax.dot_general(q, kall_ref[...], NT_DIMS, preferred_element_type=F32)
    for c in range(NA_NQ // NA_CHUNK):
        rows = slice(c * NA_CHUNK, (c + 1) * NA_CHUNK)
        s_loc = s_ref[rows, :NA_NK] + bias_ref[rows, :]
        s_ctx = s_ref[rows, NA_NK:]
        m = functools.reduce(jnp.maximum, [blk[:, t:t + hd] for blk in (s_loc, s_ctx)
                                           for t in range(0, blk.shape[1], hd)])
        m = jnp.max(m, axis=-1, keepdims=True)
        p_ref[rows, :NA_NK] = jnp.exp2(s_loc - m).astype(BF16)
        p_ref[rows, NA_NK:] = jnp.exp2(s_ctx - m).astype(BF16)
    o = jnp.dot(p_ref[...], vall_ref[...], preferred_element_type=F32)
    o_ref[...] = (o[:, :hd] / o[:, hd:hd + 1]).astype(BF16)


def _na_attention(proj, cache_k, cache_v, rpb, q_norm, k_norm):
    assert NA_STEPS >= 3 and all(_na_geometry(j)[1] == _na_geometry(1)[1] for j in range(1, NA_STEPS - 1))
    even, odd = _na_bias_slots(rpb)
    ck = cache_k.reshape(DEC_BATCH, PAST_LEN, W_B)
    cv = cache_v.reshape(DEC_BATCH, PAST_LEN, W_B)
    hd = NA_HEAD_DIM
    ctx = pl.BlockSpec((None, PAST_LEN, hd), lambda b, h, j: (b, 0, h))
    q_blk0 = N_P // NA_NQ
    kv_blk0 = N_P // DEC_SEQ
    slots = pl.BlockSpec((None, GRID_W, NA_SLOTS * GRID_W), lambda b, h, j: (h, 0, 0))
    return pl.pallas_call(
        _na_kernel,
        grid=(DEC_BATCH, NA_HEADS, NA_STEPS),
        in_specs=[
            pl.BlockSpec((NA_NQ, hd), lambda b, h, j: (q_blk0 + b * NA_STEPS + j, QKV_COL0 + h)),
            pl.BlockSpec((DEC_SEQ, hd), lambda b, h, j: (kv_blk0 + b, QKV_COL0 + NA_HEADS + h)),
            pl.BlockSpec((DEC_SEQ, hd), lambda b, h, j: (kv_blk0 + b, QKV_COL0 + 2 * NA_HEADS + h)),
            ctx, ctx,
            slots, slots,
            pl.BlockSpec((1, hd), lambda b, h, j: (0, 0)),
            pl.BlockSpec((1, hd), lambda b, h, j: (0, 0)),
        ],
        out_specs=pl.BlockSpec((NA_NQ, hd), lambda b, h, j: (b * NA_STEPS + j, h)),
        out_shape=jax.ShapeDtypeStruct((N_S, W_B), BF16),
        scratch_shapes=[pltpu.VMEM((DEC_SEQ, hd), BF16),
                        pltpu.VMEM((DEC_SEQ, 2 * hd), BF16),
                        pltpu.VMEM((NA_NK + PAST_LEN, hd), BF16),
                        pltpu.VMEM((NA_NK + PAST_LEN, 2 * hd), BF16),
                        pltpu.VMEM((NA_NQ, NA_NK), F32),
                        pltpu.VMEM((NA_NQ, NA_NK + PAST_LEN), F32),
                        pltpu.VMEM((NA_NQ, NA_NK + PAST_LEN), BF16)],
        compiler_params=_params("arbitrary", "arbitrary", "arbitrary"),
        name="na_attention",
    )(proj, proj, proj, ck, cv, even, odd, q_norm.reshape(1, hd), k_norm.reshape(1, hd))


def _ctx_kernel(q_ref, k_ref, v_ref, qg_ref, kg_ref, o_ref, kn_ref, vo_ref):
    vo_ref[...] = v_ref[...].astype(F32)
    for h in range(NA_HEADS):
        cols = slice(h * NA_HEAD_DIM, (h + 1) * NA_HEAD_DIM)
        q = (_head_rmsnorm(q_ref[:, cols].astype(F32), qg_ref[...]) * HEAD_SCALE).astype(BF16)
        kn = _head_rmsnorm(k_ref[:, cols].astype(F32), kg_ref[...])
        kn_ref[:, cols] = kn
        s = lax.dot_general(q, kn.astype(BF16), NT_DIMS, preferred_element_type=F32)
        p = jnp.exp(s - jnp.max(s, axis=-1, keepdims=True))
        denom = jnp.sum(p, axis=-1, keepdims=True)
        o = jnp.dot(p.astype(BF16), v_ref[:, cols], preferred_element_type=F32)
        o_ref[:, cols] = (o / denom).astype(BF16)


def _ctx_attention(proj, q_norm, k_norm):
    hd = NA_HEAD_DIM
    col0 = 3 * W_A // W_B
    ospec = pl.BlockSpec((SEQ, W_B), lambda b: (b, 0))
    return pl.pallas_call(
        _ctx_kernel,
        grid=(BATCH,),
        in_specs=[
            pl.BlockSpec((SEQ, W_B), lambda b: (b, col0)),
            pl.BlockSpec((SEQ, W_B), lambda b: (b, col0 + 1)),
            pl.BlockSpec((SEQ, W_B), lambda b: (b, col0 + 2)),
            pl.BlockSpec((1, hd), lambda b: (0, 0)),
            pl.BlockSpec((1, hd), lambda b: (0, 0)),
        ],
        out_specs=[ospec, ospec, ospec],
        out_shape=[jax.ShapeDtypeStruct((N_P, W_B), BF16),
                   jax.ShapeDtypeStruct((N_P, W_B), F32),
                   jax.ShapeDtypeStruct((N_P, W_B), F32)],
        compiler_params=_params("arbitrary"),
        name="ctx_attention",
    )(proj, proj, proj, q_norm.reshape(1, hd), k_norm.reshape(1, hd))


def _out_proj_kernel(*refs, tm, ka, n_ya, n_yb, n_x):
    ya_refs, refs = refs[:n_ya], refs[n_ya:]
    yb_refs, refs = refs[:n_yb], refs[n_yb:]
    w_ref, refs = refs[0], refs[1:]
    x_refs, refs = refs[:n_x], refs[n_x:]
    gate_ref, g_ref, sh_ref, sc_ref, xo_ref, ho_ref = refs
    row0 = pl.program_id(0) * tm
    o = (jnp.dot(_stream_tile(ya_refs, row0), w_ref[0:ka, :], preferred_element_type=F32)
         + jnp.dot(_stream_tile(yb_refs, row0), w_ref[ka:, :], preferred_element_type=F32))
    xo_ref[...] = _stream_tile(x_refs, row0) + _mod_row(gate_ref, row0) * o
    g, shift, scale = g_ref[...], _mod_row(sh_ref, row0), _mod_row(sc_ref, row0)
    for c in range(tm // HALO):
        rows = slice(c * HALO, (c + 1) * HALO)
        ho_ref[rows, :] = _norm_mod(xo_ref[rows, :], g, shift, scale).astype(BF16)


def _out_proj(ya_parts, yb_parts, w_bf16, x_parts, mod, layer, g_ffn, tm=512):
    ka, kb = ya_parts[0].shape[1], yb_parts[0].shape[1]
    return pl.pallas_call(
        functools.partial(_out_proj_kernel, tm=tm, ka=ka, n_ya=len(ya_parts), n_yb=len(yb_parts),
                          n_x=len(x_parts)),
        grid=(N_TOK // tm,),
        in_specs=_stream_specs(ya_parts, tm, ka, 0)
        + _stream_specs(yb_parts, tm, kb, 0)
        + [pl.BlockSpec((ka + kb, D_MODEL), lambda i: (0, 0))]
        + _stream_specs(x_parts, tm, D_MODEL, 0) + [
            _mod_spec(layer, 2, D_MODEL, None),
            pl.BlockSpec((1, D_MODEL), lambda i: (0, 0)),
            _mod_spec(layer, 3, D_MODEL, None),
            _mod_spec(layer, 4, D_MODEL, None)],
        out_specs=[pl.BlockSpec((tm, D_MODEL), lambda i: (i, 0)),
                   pl.BlockSpec((tm, D_MODEL), lambda i: (i, 0))],
        out_shape=[jax.ShapeDtypeStruct((N_TOK, D_MODEL), F32),
                   jax.ShapeDtypeStruct((N_TOK, D_MODEL), BF16)],
        compiler_params=_params("arbitrary"),
        name="out_proj",
    )(*ya_parts, *yb_parts, w_bf16, *x_parts, mod, g_ffn.reshape(1, D_MODEL), mod, mod)


GELU_C = math.sqrt(2.0 / math.pi)


def _gelu_tanh(x):
    return 0.5 * x * (1.0 + jnp.tanh(GELU_C * (x + 0.044715 * (x * x * x))))


def _ffn_in_kernel(h_ref, hp_ref, hn_ref, wa_ref, wg_ref, cw_ref, wo_ref, o_ref, wob_ref, wab_ref, wgb_ref,
                   *, tm, tn):
    i = pl.program_id(1)

    @pl.when(i == 0)
    def _():
        wab_ref[...] = wa_ref[...].astype(BF16)
        wgb_ref[...] = wg_ref[...].astype(BF16)

    wob_ref[...] = wo_ref[...].astype(BF16)

    row0 = i * tm
    seq_len = _seq_len(row0)
    h = h_ref[...]
    h_ext = jnp.concatenate([hp_ref[...], h, hn_ref[...]], axis=0)
    n = tm + 2 * HALO
    pos = (row0 + lax.broadcasted_iota(jnp.int32, (tm, 1), 0)) & (seq_len - 1)
    first = pos == 0
    last = pos == seq_len - 1
    a_ext = jnp.dot(h_ext, wab_ref[...], preferred_element_type=F32)
    g = jnp.dot(h, wgb_ref[...], preferred_element_type=F32)
    a_mid = a_ext[HALO:HALO + tm, :]
    a_prev = jnp.where(first, 0.0, pltpu.roll(a_ext, 1, 0)[HALO:HALO + tm, :])
    a_next = jnp.where(last, 0.0, pltpu.roll(a_ext, n - 1, 0)[HALO:HALO + tm, :])
    cw = cw_ref[...]
    conv = cw[0:1, :] * a_prev + cw[1:2, :] * a_mid + cw[2:3, :] * a_next
    o_ref[...] = (_gelu_tanh(conv) * g).astype(BF16)


def _ffn_in(h, w_in, conv, w_out, layer, tm=1024, tn=512):
    per = tm // HALO
    last = N_TOK // HALO - 1
    ng = D_FF // tn
    n_i = N_TOK // tm
    per_j = D_FF // ng // CAST_ROWS
    assert per_j <= n_i and per_j * CAST_ROWS * ng == D_FF

    def cast_block(j, i):
        return j * per_j + jnp.minimum(i, per_j - 1)

    return pl.pallas_call(
        functools.partial(_ffn_in_kernel, tm=tm, tn=tn),
        grid=(ng, n_i),
        in_specs=[pl.BlockSpec((tm, D_MODEL), lambda j, i: (i, 0)),
                  pl.BlockSpec((HALO, D_MODEL), lambda j, i: (jnp.maximum(i * per - 1, 0), 0)),
                  pl.BlockSpec((HALO, D_MODEL), lambda j, i: (jnp.minimum((i + 1) * per, last), 0)),
                  pl.BlockSpec((None, D_MODEL, tn), lambda j, i: (layer, 0, j)),
                  pl.BlockSpec((None, D_MODEL, tn), lambda j, i: (layer, 0, ng + j)),
                  pl.BlockSpec((None, SHORT_K, tn), lambda j, i: (layer, 0, j)),
                  pl.BlockSpec((None, CAST_ROWS, D_MODEL), lambda j, i: (layer, cast_block(j, i), 0))],
        out_specs=[pl.BlockSpec((tm, tn), lambda j, i: (i, j)),
                   pl.BlockSpec((CAST_ROWS, D_MODEL), lambda j, i: (cast_block(j, i), 0))],
        out_shape=[jax.ShapeDtypeStruct((N_TOK, D_FF), BF16),
                   jax.ShapeDtypeStruct((D_FF, D_MODEL), BF16)],
        scratch_shapes=[pltpu.VMEM((D_MODEL, tn), BF16), pltpu.VMEM((D_MODEL, tn), BF16)],
        compiler_params=_params("arbitrary", "arbitrary"),
        name="ffn_in",
    )(h, h, h, w_in, w_in, conv, w_out)


def _ffn_out_kernel(u_ref, w_ref, x_ref, gate_ref, o_ref, *, tm, row_start):
    row0 = row_start + pl.program_id(1) * tm
    o = jnp.dot(u_ref[...], w_ref[...], preferred_element_type=F32)
    o_ref[...] = x_ref[...] + _mod_row(gate_ref, row0) * o


def _ffn_out(u, w_bf16, x, mod, layer, row_start=0, n_rows=N_TOK, tm=512, tn=1024):
    t0 = row_start // tm
    return pl.pallas_call(
        functools.partial(_ffn_out_kernel, tm=tm, row_start=row_start),
        grid=(D_MODEL // tn, n_rows // tm),
        in_specs=[pl.BlockSpec((tm, D_FF), lambda j, i: (t0 + i, 0)),
                  pl.BlockSpec((D_FF, tn), lambda j, i: (0, j)),
                  pl.BlockSpec((tm, tn), lambda j, i: (t0 + i, j)),
                  _mod_spec(layer, 5, tn, 0)],
        out_specs=pl.BlockSpec((tm, tn), lambda j, i: (i, j)),
        out_shape=jax.ShapeDtypeStruct((n_rows, D_MODEL), F32),
        compiler_params=_params("arbitrary", "arbitrary"),
        name="ffn_out",
    )(u, w_bf16, x, mod)


CONF_TR = 256
CONF_PAD = CONF_K // 2
CONF_RB = 128
CONF_CB = 128
CONF_LN_ROWS = 16


def _conformer_tile(ca_ref, cg_ref, pa_ref, pg_ref, na_ref, ng_ref, w_ref, b_ref, lg_ref, lb_ref, o_ref,
                    u_ref, v_ref, tile):
    tr = CONF_TR
    row0 = tile * tr
    seq_len = _seq_len(row0)
    pos0 = row0 & (seq_len - 1)

    def glu(a_ref, g_ref):
        return a_ref[...].astype(F32) * jax.nn.sigmoid(g_ref[...].astype(F32))

    u_ref[0:HALO, :] = jnp.where(pos0 == 0, 0.0, glu(pa_ref, pg_ref))
    u_ref[HALO:HALO + tr, :] = glu(ca_ref, cg_ref)
    u_ref[HALO + tr:, :] = jnp.where(pos0 + tr == seq_len, 0.0, glu(na_ref, ng_ref))
    base = HALO - CONF_PAD
    n_in = CONF_RB + 2 * HALO
    for cb in range(W_C // CONF_CB):
        cols = slice(cb * CONF_CB, (cb + 1) * CONF_CB)
        w = w_ref[:, cols]
        for rb in range(tr // CONF_RB):
            u_blk = u_ref[rb * CONF_RB:rb * CONF_RB + n_in, cols]
            acc = jnp.zeros((CONF_RB, CONF_CB), F32) + b_ref[:, cols]
            for r in range(8):
                u_r = pltpu.roll(u_blk, n_in - (base + r), 0)
                for a in range(len(range(r, CONF_K, 8))):
                    k = 8 * a + r
                    acc = acc + w[k:k + 1, :] * u_r[8 * a:8 * a + CONF_RB, :]
            v_ref[rb * CONF_RB:(rb + 1) * CONF_RB, cols] = acc
    for c in range(tr // CONF_LN_ROWS):
        rows = slice(c * CONF_LN_ROWS, (c + 1) * CONF_LN_ROWS)
        v = v_ref[rows, :]
        mu = jnp.mean(v, axis=-1, keepdims=True)
        d = v - mu
        var = jnp.mean(d * d, axis=-1, keepdims=True)
        y = d * lax.rsqrt(var + EPS) * lg_ref[...] + lb_ref[...]
        o_ref[rows, :] = (y * jax.nn.sigmoid(y)).astype(BF16)


def _conformer_kernel(*refs):
    _conformer_tile(*refs, pl.program_id(0))


def _conformer(proj, dw, dw_b, ln_g, ln_b):
    pa, na = _halo_specs(CONF_TR, W_C, 0)
    pg, ng = _halo_specs(CONF_TR, W_C, 1)
    row = pl.BlockSpec((1, W_C), lambda i: (0, 0))
    return pl.pallas_call(
        _conformer_kernel,
        grid=(N_TOK // CONF_TR,),
        in_specs=[pl.BlockSpec((CONF_TR, W_C), lambda i: (i, 0)),
                  pl.BlockSpec((CONF_TR, W_C), lambda i: (i, 1)),
                  pa, pg, na, ng,
                  pl.BlockSpec((CONF_K, W_C), lambda i: (0, 0)), row, row, row],
        out_specs=pl.BlockSpec((CONF_TR, W_C), lambda i: (i, 0)),
        out_shape=jax.ShapeDtypeStruct((N_TOK, W_C), BF16),
        scratch_shapes=[pltpu.VMEM((CONF_TR + 2 * HALO, W_C), F32), pltpu.VMEM((CONF_TR, W_C), F32)],
        compiler_params=_params("arbitrary"),
        name="conformer",
    )(proj, proj, proj, proj, proj, proj, dw, dw_b.reshape(1, W_C), ln_g.reshape(1, W_C), ln_b.reshape(1, W_C))


HY_COL0 = 2 * W_C // W_D


def _hy_short_kernel(*refs, tr):
    (x0_ref, x1_ref, vv_ref, p0_ref, p1_ref, pv_ref, n0_ref, n1_ref, nv_ref,
     w0_ref, w1_ref, wv_ref, b0_ref, b1_ref, bv_ref, x0o_ref, wo_ref, tmp_ref) = refs
    row0 = pl.program_id(0) * tr
    seq_len = _seq_len(row0)

    def conv(c_ref, p_ref, n_ref, w_ref, b_ref):
        p = c_ref[...].astype(F32)
        p_prev, p_next = _shift_rows(p, p_ref[HALO - 1:HALO, :].astype(F32), n_ref[0:1, :].astype(F32),
                                     row0, seq_len)
        w = w_ref[...]
        return w[0:1, :] * p_prev + w[1:2, :] * p + w[2:3, :] * p_next + b_ref[...]

    _store_even_odd(x0o_ref, tmp_ref, conv(x0_ref, p0_ref, n0_ref, w0_ref, b0_ref))
    x1 = conv(x1_ref, p1_ref, n1_ref, w1_ref, b1_ref)
    vv = conv(vv_ref, pv_ref, nv_ref, wv_ref, bv_ref)
    _store_even_odd(wo_ref, tmp_ref, vv * x1)


def _store_even_odd(o_ref, tmp_ref, val):
    n, width = val.shape
    for cb in range(width // LANES):
        cols = slice(cb * LANES, (cb + 1) * LANES)
        tmp_ref[cb] = val[:, cols]
        for parity in (0, 1):
            dst = slice(parity * width + cb * LANES, parity * width + (cb + 1) * LANES)
            o_ref[:, dst] = tmp_ref[cb, pl.ds(parity, n // 2, stride=2), :].astype(BF16)


def _hy_short(proj, hy_short, hy_short_b, tr=1024):
    main = [pl.BlockSpec((tr, W_D), functools.partial(lambda i, c: (i, c), c=HY_COL0 + c)) for c in range(3)]
    halos = [_halo_specs(tr, W_D, HY_COL0 + c) for c in range(3)]
    wspec = [pl.BlockSpec((SHORT_K, W_D), functools.partial(lambda i, c: (0, c), c=c)) for c in range(3)]
    bspec = [pl.BlockSpec((1, W_D), functools.partial(lambda i, c: (0, c), c=c)) for c in range(3)]
    out = jax.ShapeDtypeStruct((N_TOK // 2, W2_D), BF16)
    ospec = pl.BlockSpec((tr // 2, W2_D), lambda i: (i, 0))
    b2 = hy_short_b.reshape(1, 3 * W_D)
    return pl.pallas_call(
        functools.partial(_hy_short_kernel, tr=tr),
        grid=(N_TOK // tr,),
        in_specs=main + [h[0] for h in halos] + [h[1] for h in halos] + wspec + bspec,
        out_specs=[ospec, ospec],
        out_shape=[out, out],
        scratch_shapes=[pltpu.VMEM((W_D // LANES, tr, LANES), F32)],
        compiler_params=_params("arbitrary"),
        name="hy_short",
    )(*([proj] * 9), hy_short, hy_short, hy_short, b2, b2, b2)


HIGHEST = lax.Precision.HIGHEST
HY_EMB_PAD = 128
W2_D = 2 * W_D


def _dft_gen_kernel(ct_ref, st_ref, cr_ref, sr_ref, c_ref, s_ref):
    t = pl.program_id(0)
    cr = cr_ref[pl.ds(t, 1), :]
    sr = sr_ref[pl.ds(t, 1), :]
    ct = ct_ref[...]
    st = st_ref[...]
    c_ref[...] = (cr * ct - sr * st).astype(BF16)
    s_ref[...] = (sr * ct + cr * st).astype(BF16)


def _dft_matrices(L):
    tr = min(L, 256)
    nt = L // tr
    nt_pad = -(-nt // 8) * 8
    n = jnp.arange(L, dtype=jnp.int32)[None, :]

    def tables(k):
        ang = ((k[:, None] * n) % (2 * L)).astype(F32) * (math.pi / L)
        return jnp.cos(ang), jnp.sin(ang)

    ct, st = tables(jnp.arange(tr, dtype=jnp.int32))
    cr, sr = tables(jnp.arange(nt_pad, dtype=jnp.int32) * tr)
    tile = pl.BlockSpec((tr, L), lambda t: (0, 0))
    rows = pl.BlockSpec((nt_pad, L), lambda t: (0, 0))
    out = pl.BlockSpec((tr, L), lambda t: (t, 0))
    return pl.pallas_call(
        _dft_gen_kernel,
        grid=(nt,),
        in_specs=[tile, tile, rows, rows],
        out_specs=[out, out],
        out_shape=[jax.ShapeDtypeStruct((L, L), BF16)] * 2,
        compiler_params=_params("arbitrary"),
        name="dft_gen",
    )(ct, st, cr, sr)


def _hy_features(L):
    t = jnp.linspace(0.0, 1.0, L, dtype=F32)[:, None]
    bands = (HY_EMB - 1) // 2
    ang = 2 * math.pi * jnp.arange(L, dtype=F32)[:, None] / L
    freqs = jnp.linspace(1e-4, bands - 1, bands, dtype=F32)[None, :]
    z = jnp.concatenate([t, jnp.cos(freqs * ang), -jnp.sin(freqs * ang)], axis=-1)
    return jnp.pad(z, ((0, 0), (0, HY_EMB_PAD - HY_EMB))), t


def _hy_filter_kernel(z_ref, t_ref, w1_ref, b1_ref, f1_ref, w2_ref, b2_ref, f2_ref, w3_ref, dl_ref,
                      fs_ref, fd_ref, hn_ref, tmp_ref, *, tr):
    i = pl.program_id(0)
    hid = jnp.sin(f1_ref[...] * (jnp.dot(z_ref[...], w1_ref[...], precision=HIGHEST,
                                         preferred_element_type=F32) + b1_ref[...]))
    hid = jnp.sin(f2_ref[...] * (jnp.dot(hid, w2_ref[...], precision=HIGHEST,
                                         preferred_element_type=F32) + b2_ref[...]))
    hf = jnp.dot(hid.astype(BF16), w3_ref[...].astype(BF16), preferred_element_type=F32)
    decay = jnp.exp(-t_ref[...] * jnp.abs(dl_ref[...]))
    ridx = i * tr + lax.broadcasted_iota(jnp.int32, (tr, 1), 0)
    fwd = hf[:, :W_D] * decay
    bwd = jnp.where(ridx == 0, 0.0, hf[:, W_D:] * decay)
    fsum = fwd + bwd
    fdif = bwd - fwd
    _store_even_odd(fs_ref, tmp_ref, fsum)
    _store_even_odd(fd_ref, tmp_ref, fdif)
    sgn = (1 - (ridx & 2)).astype(F32)
    even = (ridx & 1) == 0
    hc_part = jnp.sum(jnp.where(even, sgn, 0.0) * fsum, axis=0, keepdims=True)
    hs_part = jnp.sum(jnp.where(even, 0.0, sgn) * fdif, axis=0, keepdims=True)

    @pl.when(i == 0)
    def _():
        hn_ref[...] = jnp.zeros_like(hn_ref)

    hn_ref[0:1, :] += hc_part
    hn_ref[1:2, :] += hs_part


def _hy_filter(L, w1, b1, f1, w2, b2, f2, w3):
    tr = min(L, 512)
    z, t = _hy_features(L)
    max_decay = math.log(HY_TARGET) / HY_FAST_PCT
    min_decay = math.log(HY_TARGET) / HY_SLOW_PCT
    deltas = jnp.linspace(min_decay, max_decay, W_D, dtype=F32)[None, :]
    w1p = jnp.pad(w1, ((0, HY_EMB_PAD - HY_EMB), (0, 0)))
    full = lambda shape: pl.BlockSpec(shape, lambda i: (0, 0))
    return pl.pallas_call(
        functools.partial(_hy_filter_kernel, tr=tr),
        grid=(L // tr,),
        in_specs=[pl.BlockSpec((tr, HY_EMB_PAD), lambda i: (i, 0)),
                  pl.BlockSpec((tr, 1), lambda i: (i, 0)),
                  full((HY_EMB_PAD, HY_HIDDEN)), full((1, HY_HIDDEN)), full((1, HY_HIDDEN)),
                  full((HY_HIDDEN, HY_HIDDEN)), full((1, HY_HIDDEN)), full((1, HY_HIDDEN)),
                  full((HY_HIDDEN, 2 * W_D)), full((1, W_D))],
        out_specs=[pl.BlockSpec((tr // 2, W2_D), lambda i: (i, 0)),
                   pl.BlockSpec((tr // 2, W2_D), lambda i: (i, 0)),
                   pl.BlockSpec((8, W_D), lambda i: (0, 0))],
        out_shape=[jax.ShapeDtypeStruct((L // 2, W2_D), BF16),
                   jax.ShapeDtypeStruct((L // 2, W2_D), BF16),
                   jax.ShapeDtypeStruct((8, W_D), F32)],
        scratch_shapes=[pltpu.VMEM((W_D // LANES, tr, LANES), F32)],
        compiler_params=_params("arbitrary"),
        name="hy_filter",
    )(z, t, w1p, b1.reshape(1, -1), f1.reshape(1, -1), w2, b2.reshape(1, -1), f2.reshape(1, -1), w3, deltas)


def _twiddles(L):
    k = jnp.arange(L // 2, dtype=F32)[:, None] * (math.pi / L)
    return jnp.cos(k), jnp.sin(k)


def _butterfly(c_ref, s_ref, x2, twc, tws):
    xc2 = jnp.dot(c_ref[...], x2, preferred_element_type=F32)
    xs2 = jnp.dot(s_ref[...], x2, preferred_element_type=F32)
    ec, oc = xc2[:, :W_D], xc2[:, W_D:]
    es, os_ = xs2[:, :W_D], xs2[:, W_D:]
    tc = twc * oc - tws * os_
    ts = twc * os_ + tws * oc
    return ec + tc, es + ts, ec - tc, es - ts


def _hy_spec_kernel(c_ref, s_ref, fs_ref, fd_ref, twc_ref, tws_ref, hac_ref, has_ref, hrc_ref, hrs_ref):
    twc, tws = twc_ref[...], tws_ref[...]
    ac, _, bc, _ = _butterfly(c_ref, s_ref, fs_ref[...], twc, tws)
    _, as_, _, bs = _butterfly(c_ref, s_ref, fd_ref[...], twc, tws)
    hac_ref[...] = ac
    has_ref[...] = as_
    hrc_ref[...] = bc
    hrs_ref[...] = -bs


def _hy_spectrum(cmat, smat, fs2, fd2, twc, tws, tk):
    M = cmat.shape[0]
    tile = pl.BlockSpec((tk, M), lambda i: (i, 0))
    full = pl.BlockSpec((M, W2_D), lambda i: (0, 0), pipeline_mode=pl.Buffered(1))
    col = pl.BlockSpec((tk, 1), lambda i: (i, 0))
    out = pl.BlockSpec((tk, W_D), lambda i: (i, 0))
    return pl.pallas_call(
        _hy_spec_kernel,
        grid=(M // tk,),
        in_specs=[tile, tile, full, full, col, col],
        out_specs=[out] * 4,
        out_shape=[jax.ShapeDtypeStruct((M, W_D), F32)] * 4,
        compiler_params=_params("arbitrary"),
        name="hy_spectrum",
    )(cmat, smat, fs2, fd2, twc, tws)


def _hy_fwd_kernel(c_ref, s_ref, w_ref, hac_ref, has_ref, hrc_ref, hrs_ref, twc_ref, tws_ref, hn_ref,
                   vc_ref, vs_ref, yn_ref, *, L, tk, group):
    kt = pl.program_id(1)
    M = L // 2
    twc, tws = twc_ref[...], tws_ref[...]
    hac, has, hrc, hrs = hac_ref[...], has_ref[...], hrc_ref[...], hrs_ref[...]
    kidx = kt * tk + lax.broadcasted_iota(jnp.int32, (tk, 1), 0)
    om = jnp.where(kidx == 0, 0.5 / L, 1.0 / L)
    for g in range(group):
        w2 = w_ref[g * M:(g + 1) * M, :]
        rows = slice(g * tk, (g + 1) * tk)
        ac, as_, bc, bs = _butterfly(c_ref, s_ref, w2, twc, tws)
        yac = ac * hac + as_ * has
        yai = ac * has - as_ * hac
        pc = bc * hrc - bs * hrs
        pi = -(bc * hrs + bs * hrc)
        dc, di = yac - pc, yai - pi
        vc_ref[rows, :] = (om * jnp.concatenate([yac + pc, twc * dc - tws * di], axis=1)).astype(BF16)
        vs_ref[rows, :] = ((-1.0 / L) * jnp.concatenate([yai + pi, twc * di + tws * dc], axis=1)).astype(BF16)

    @pl.when(kt == 0)
    def _():
        m = lax.broadcasted_iota(jnp.int32, (M, 1), 0)
        sgn = (1 - 2 * (m & 1)).astype(F32)
        hc, hs = hn_ref[0:1, :], hn_ref[1:2, :]
        for g in range(group):
            xn2 = jnp.sum(w_ref[g * M:(g + 1) * M, :].astype(F32) * sgn, axis=0, keepdims=True)
            en, on = xn2[:, :W_D], xn2[:, W_D:]
            yn = jnp.concatenate([en * hc + on * hs, on * hc - en * hs], axis=1) * (1.0 / L)
            yn_ref[g] = jnp.broadcast_to(yn, yn_ref.shape[1:])


HY_GROUP = 4


def _hy_group(M, tk, nb, blk0):
    ok = M == tk and nb % HY_GROUP == 0 and blk0 % HY_GROUP == 0
    return HY_GROUP if ok else 1


def _hy_forward(cmat, smat, w2, h4, twc, tws, hn, L, nb, blk0, tk):
    M = L // 2
    g = _hy_group(M, tk, nb, blk0)
    tile = pl.BlockSpec((tk, M), lambda b, k: (k, 0))
    htile = pl.BlockSpec((tk, W_D), lambda b, k: (k, 0))
    col = pl.BlockSpec((tk, 1), lambda b, k: (k, 0))
    out = pl.BlockSpec((g * tk, W2_D), lambda b, k: (b * (M // tk) + k, 0))
    return pl.pallas_call(
        functools.partial(_hy_fwd_kernel, L=L, tk=tk, group=g),
        grid=(nb // g, M // tk),
        in_specs=[tile, tile, pl.BlockSpec((g * M, W2_D), lambda b, k: (blk0 // g + b, 0)),
                  htile, htile, htile, htile, col, col,
                  pl.BlockSpec((8, W_D), lambda b, k: (0, 0))],
        out_specs=[out, out, pl.BlockSpec((g, 8, W2_D), lambda b, k: (b, 0, 0))],
        out_shape=[jax.ShapeDtypeStruct((nb * M, W2_D), BF16),
                   jax.ShapeDtypeStruct((nb * M, W2_D), BF16),
                   jax.ShapeDtypeStruct((nb, 8, W2_D), F32)],
        compiler_params=_params("arbitrary", "arbitrary"),
        name="hy_forward",
    )(cmat, smat, w2, *h4, twc, tws, hn)


def _hy_inv_kernel(c_ref, s_ref, vc_ref, vs_ref, yn_ref, x0_ref, w_ref, bias_ref, z_ref, tmp_ref,
                   *, M, tt, group):
    ti = pl.program_id(1)
    m = ti * tt + lax.broadcasted_iota(jnp.int32, (tt, 1), 0)
    sgn = (1 - 2 * (m & 1)).astype(F32)
    for g in range(group):
        seq = slice(g * M, (g + 1) * M)
        rows = slice(g * tt, (g + 1) * tt)
        y = (jnp.dot(c_ref[...], vc_ref[seq, :], preferred_element_type=F32)
             + jnp.dot(s_ref[...], vs_ref[seq, :], preferred_element_type=F32))
        y = y + sgn * yn_ref[g, 0:1, :] + w_ref[rows, :].astype(F32) * bias_ref[...]
        z2 = x0_ref[rows, :].astype(F32) * y
        for cb in range(W_D // LANES):
            cols = slice(cb * LANES, (cb + 1) * LANES)
            tmp_ref[cb, pl.ds(0, tt, stride=2), :] = z2[:, cols]
            tmp_ref[cb, pl.ds(1, tt, stride=2), :] = z2[:, W_D + cb * LANES:W_D + (cb + 1) * LANES]
            z_ref[g * 2 * tt:(g + 1) * 2 * tt, cols] = tmp_ref[cb].astype(BF16)


def _hy_inverse(cmat, smat, vc, vs, yn, x02, w2, bias, L, nb, blk0, tt):
    M = L // 2
    per = M // tt
    g = _hy_group(M, tt, nb, blk0)
    tile = pl.BlockSpec((tt, M), lambda b, t: (t, 0))
    seq = pl.BlockSpec((g * M, W2_D), lambda b, t: (b, 0))
    rows_in = pl.BlockSpec((g * tt, W2_D), lambda b, t: ((blk0 // g + b) * per + t, 0))
    return pl.pallas_call(
        functools.partial(_hy_inv_kernel, M=M, tt=tt, group=g),
        grid=(nb // g, per),
        in_specs=[tile, tile, seq, seq,
                  pl.BlockSpec((g, 8, W2_D), lambda b, t: (b, 0, 0)),
                  rows_in, rows_in,
                  pl.BlockSpec((1, W2_D), lambda b, t: (0, 0))],
        out_specs=pl.BlockSpec((2 * g * tt, W_D), lambda b, t: (b * per + t, 0)),
        out_shape=jax.ShapeDtypeStruct((nb * L, W_D), BF16),
        scratch_shapes=[pltpu.VMEM((W_D // LANES, 2 * tt, LANES), F32)],
        compiler_params=_params("arbitrary", "arbitrary"),
        name="hy_inverse",
    )(cmat, smat, vc, vs, yn, x02, w2, jnp.concatenate([bias, bias]).reshape(1, W2_D))


def _hyena_long_conv(x02, w2, L, nb, blk0, filt_params, bias):
    M = L // 2
    tk = min(M, 256)
    cmat, smat = _dft_matrices(M)
    twc, tws = _twiddles(L)
    fs2, fd2, hn = _hy_filter(L, *filt_params)
    h4 = _hy_spectrum(cmat, smat, fs2, fd2, twc, tws, tk)
    vc, vs, yn = _hy_forward(cmat, smat, w2, h4, twc, tws, hn, L, nb, blk0, tk)
    return _hy_inverse(cmat, smat, vc, vs, yn, x02, w2, bias, L, nb, blk0, tk)


def kernel(x_prompt, x_sample, cache_k, cache_v, c, c_ctx, ada_w, ada_b, norm_mix, norm_ffn,
           e_w_in, e_conv_a, e_q_norm, e_k_norm, e_rpb, e_w_out,
           o_w_in, o_conf_dw, o_conf_dw_b, o_conf_ln_g, o_conf_ln_b, o_hy_short, o_hy_short_b,
           o_hy_w1, o_hy_b1, o_hy_f1, o_hy_w2, o_hy_b2, o_hy_f2, o_hy_w3, o_hy_bias, o_w_out,
           ffn_in, ffn_conv, ffn_out):
    x_parts = (x_prompt.reshape(N_P, D_MODEL), x_sample.reshape(N_S, D_MODEL))
    cvec = jnp.concatenate([c_ctx[None, :], c, jnp.zeros((N_SEG_PAD - 1 - DEC_BATCH, D_MODEL), F32)], axis=0)
    mod = _adaln(cvec, ada_w, ada_b)
    ks_new, vs_new = [], []
    for layer in range(DEPTH):
        j = layer // 2
        last = layer == DEPTH - 1
        h = _norm_mod_call(x_parts, norm_mix[layer], mod, layer, 0)
        if layer % 2 == 0:
            proj, w_out = _proj(h, e_w_in[j], e_w_out[j])
            ya = (_short_gated_conv(proj, e_conv_a[j]),)
            yb_p, k_p, v_p = _ctx_attention(proj, e_q_norm[j], e_k_norm[j])
            yb_s = _na_attention(proj, cache_k[:, j], cache_v[:, j], e_rpb[j], e_q_norm[j], e_k_norm[j])
            yb = (yb_p, yb_s)
            ks_new.append(k_p.reshape(BATCH, SEQ, NA_HEADS, NA_HEAD_DIM))
            vs_new.append(v_p.reshape(BATCH, SEQ, NA_HEADS, NA_HEAD_DIM))
        else:
            proj, w_out = _proj(h, o_w_in[j], o_w_out[j])
            ya = (_conformer(proj, o_conf_dw[j], o_conf_dw_b[j], o_conf_ln_g[j], o_conf_ln_b[j]),)
            x0, w = _hy_short(proj, o_hy_short[j], o_hy_short_b[j])
            fp = (o_hy_w1[j], o_hy_b1[j], o_hy_f1[j], o_hy_w2[j], o_hy_b2[j], o_hy_f2[j], o_hy_w3[j])
            z_p = _hyena_long_conv(x0, w, SEQ, BATCH, 0, fp, o_hy_bias[j])
            z_s = _hyena_long_conv(x0, w, DEC_SEQ, DEC_BATCH, N_P // DEC_SEQ, fp, o_hy_bias[j])
            yb = (z_p, z_s)
        x, h_ffn = _out_proj(ya, yb, w_out, x_parts, mod, layer, norm_ffn[layer])
        u, w_ffn_out = _ffn_in(h_ffn, ffn_in, ffn_conv, ffn_out, layer)
        if last:
            x_parts = (_ffn_out(u, w_ffn_out, x, mod, layer, 0, N_P),
                       _ffn_out(u, w_ffn_out, x, mod, layer, N_P, N_S))
        else:
            x_parts = (_ffn_out(u, w_ffn_out, x, mod, layer),)
    xp = x_parts[0].reshape(BATCH, SEQ, D_MODEL)
    xs = x_parts[1].reshape(DEC_BATCH, DEC_SEQ, D_MODEL)
    return (xp, xs, jnp.stack(ks_new, axis=1), jnp.stack(vs_new, axis=1))
```
